```python
import jax, jax.numpy as jnp
from jax import lax
import numpy as np

D_MODEL = 2048
BATCH = 1
SEQ = 16384
DEPTH = 1

CHUNK = 64
RMS_EPS = 1e-5
A_HEADS = 4
A_QK = 128
A_V = 256
CONV_W = 4
B_HEADS = 8
B_HEAD_DIM = 128
Q_BLOCK = 128
N_EXPERTS = 32
TOP_K = 4
D_FF = 2048
SWIGLU_LIMIT = 7.0
SWIGLU_ALPHA = 1.702
MOE_BLOCK = 256

A_QK_W = A_HEADS * A_QK
A_V_W = A_HEADS * A_V
B_W = B_HEADS * B_HEAD_DIM
IN_COLS = 2 * A_QK_W + 2 * A_V_W + 2 * A_HEADS + 3 * B_W
SPLITS = (A_QK_W, 2 * A_QK_W, 2 * A_QK_W + A_V_W, 2 * A_QK_W + 2 * A_V_W,
          2 * A_QK_W + 2 * A_V_W + 2 * A_HEADS, 2 * A_QK_W + 2 * A_V_W + 2 * A_HEADS + B_W,
          2 * A_QK_W + 2 * A_V_W + 2 * A_HEADS + 2 * B_W)

kernel_name = "hybrid_mlstm_stickbreaking_moe"


def rmsnorm(x, g):
    xf = x.astype(jnp.float32)
    y = xf * lax.rsqrt(jnp.mean(xf * xf, axis=-1, keepdims=True) + RMS_EPS)
    return (y * g.astype(jnp.float32)).astype(x.dtype)


def causal_dwconv(u, w, b):
    k_w, s = w.shape[0], u.shape[1]
    up = jnp.pad(u, ((0, 0), (k_w - 1, 0), (0, 0)))
    out = b
    for j in range(k_w):
        out = out + up[:, j:j + s, :] * w[j]
    return out


def mlstm(q, k, v, i_pre, f_pre):
    bn, s, h_n = q.shape[:3]
    nc = s // CHUNK

    def chunked(t):
        t = t.astype(jnp.float32).reshape((bn, nc, CHUNK) + t.shape[2:])
        return jnp.moveaxis(t, 3, 1)

    qc = chunked(q)
    kc = chunked(k) * (A_QK ** -0.5)
    vc = chunked(v)
    ig = chunked(i_pre)
    lf = jax.nn.log_sigmoid(chunked(f_pre))
    a = jnp.cumsum(lf, axis=-1)
    g = a[..., -1]

    w = g[..., None] - a + ig
    m_loc = jnp.max(w, axis=-1)
    e = jnp.exp(w - m_loc[..., None])
    c_loc = jnp.einsum('bhcl,bhclv,bhclk->bhcvk', e, vc, kc)
    n_loc = jnp.einsum('bhcl,bhclk->bhck', e, kc)

    def step(carry, xs):
        c_st, n_st, m_st = carry
        c_l, n_l, m_l, g_c = xs
        m_new = jnp.maximum(g_c + m_st, m_l)
        s_old = jnp.exp(g_c + m_st - m_new)
        s_new = jnp.exp(m_l - m_new)
        c_next = s_old[..., None, None] * c_st + s_new[..., None, None] * c_l
        n_next = s_old[..., None] * n_st + s_new[..., None] * n_l
        return (c_next, n_next, m_new), (c_st, n_st, m_st)

    init = (jnp.zeros((bn, h_n, A_V, A_QK), jnp.float32),
            jnp.zeros((bn, h_n, A_QK), jnp.float32),
            jnp.zeros((bn, h_n), jnp.float32))
    xs = (jnp.moveaxis(c_loc, 2, 0), jnp.moveaxis(n_loc, 2, 0),
          jnp.moveaxis(m_loc, 2, 0), jnp.moveaxis(g, 2, 0))
    _, (c_prev, n_prev, m_prev) = lax.scan(step, init, xs)
    c_prev = jnp.moveaxis(c_prev, 0, 2)
    n_prev = jnp.moveaxis(n_prev, 0, 2)
    m_prev = jnp.moveaxis(m_prev, 0, 2)

    m_inter = a + m_prev[..., None]
    causal = jnp.tril(jnp.ones((CHUNK, CHUNK), dtype=bool))
    d_log = a[..., :, None] - a[..., None, :] + ig[..., None, :]
    d_log = jnp.where(causal, d_log, -jnp.inf)
    m_t = jnp.maximum(m_inter, jnp.max(d_log, axis=-1))
    p = jnp.exp(d_log - m_t[..., None]) * jnp.einsum('bhctk,bhcsk->bhcts', qc, kc)
    s_inter = jnp.exp(m_inter - m_t)
    num = (s_inter[..., None] * jnp.einsum('bhcvk,bhctk->bhctv', c_prev, qc)
           + jnp.einsum('bhcts,bhcsv->bhctv', p, vc))
    den = s_inter * jnp.einsum('bhck,bhctk->bhct', n_prev, qc) + jnp.sum(p, axis=-1)
    h = num / jnp.maximum(jnp.abs(den), jnp.exp(-m_t))[..., None]
    return jnp.moveaxis(h, 1, 3).reshape(bn, s, h_n, A_V)


def head_rmsnorm(y, g):
    y = y * lax.rsqrt(jnp.mean(y * y, axis=-1, keepdims=True) + RMS_EPS)
    return y.reshape(y.shape[:2] + (-1,)) * g.astype(jnp.float32)


def stick_breaking(q, k, v):
    bn, s, h_n, d = q.shape
    nq = s // Q_BLOCK
    scale = d ** -0.5
    kf = k.astype(jnp.float32)
    vf = v.astype(jnp.float32)
    qb = jnp.moveaxis(q.astype(jnp.float32).reshape(bn, nq, Q_BLOCK, h_n, d), 1, 0)
    kpos = jnp.arange(s)

    def block(args):
        q_blk, blk = args
        qpos = blk * Q_BLOCK + jnp.arange(Q_BLOCK)
        z = jnp.einsum('bqhd,bshd->bhqs', q_blk, kf) * scale
        past = kpos[None, :] < qpos[:, None]
        log_beta = jax.nn.log_sigmoid(z)
        log_keep = jnp.where(past, jax.nn.log_sigmoid(-z), 0.0)
        rev = lax.cumsum(log_keep, axis=3, reverse=True)
        r = jnp.concatenate([rev[..., 1:], jnp.zeros_like(rev[..., :1])], axis=-1)
        att = jnp.where(past, jnp.exp(log_beta + r), 0.0)
        return jnp.einsum('bhqs,bshd->bqhd', att, vf)

    out = lax.map(block, (qb, jnp.arange(nq)))
    return jnp.moveaxis(out, 0, 1).reshape(bn, s, h_n * d)


def moe(h, w_router, b_router, w_gate_up, b_gate_up, w_down, b_down):
    bn, s, d = h.shape
    t_n = bn * s
    xt = h.reshape(t_n, d)
    logits = (xt @ w_router + b_router).astype(jnp.float32)
    top_val, top_idx = lax.top_k(logits, TOP_K)
    gates = jax.nn.softmax(top_val, axis=-1)
    p_n = t_n * TOP_K
    e_flat = top_idx.reshape(p_n)
    order = jnp.argsort(e_flat)
    e_sorted = e_flat[order]
    tok_sorted = order // TOP_K
    gate_sorted = gates.reshape(p_n)[order]
    counts = jnp.zeros((N_EXPERTS,), jnp.int32).at[e_flat].add(1)
    padded = (counts + MOE_BLOCK - 1) // MOE_BLOCK * MOE_BLOCK
    start = jnp.cumsum(counts) - counts
    pad_end = jnp.cumsum(padded)
    pad_start = pad_end - padded
    dest = pad_start[e_sorted] + jnp.arange(p_n) - start[e_sorted]
    nb = -(-p_n // MOE_BLOCK) + N_EXPERTS
    x_pad = jnp.zeros((nb * MOE_BLOCK, d), xt.dtype).at[dest].set(xt[tok_sorted])
    block_expert = jnp.minimum(
        jnp.searchsorted(pad_end, jnp.arange(nb) * MOE_BLOCK, side='right'), N_EXPERTS - 1)

    def expert_block(args):
        xb, e = args
        gu = xb @ w_gate_up[e] + b_gate_up[e]
        gate = jnp.minimum(gu[:, :D_FF], SWIGLU_LIMIT)
        up = jnp.clip(gu[:, D_FF:], -SWIGLU_LIMIT, SWIGLU_LIMIT)
        act = (up + 1.0) * gate * jax.nn.sigmoid(SWIGLU_ALPHA * gate)
        return act @ w_down[e] + b_down[e]

    y_pad = lax.map(expert_block, (x_pad.reshape(nb, MOE_BLOCK, d), block_expert))
    y_pad = y_pad.reshape(nb * MOE_BLOCK, d)
    y = jnp.zeros((t_n, d), jnp.float32).at[tok_sorted].add(
        gate_sorted[:, None] * y_pad[dest].astype(jnp.float32))
    return y.astype(h.dtype).reshape(bn, s, d)


def setup_inputs(seed: int = 0) -> dict:
    key = jax.random.key(seed)
    ks = jax.random.split(key, 24)
    n = lambda k, shape, sc: jax.random.normal(k, shape, jnp.float32) * sc
    i_bias = -2.0 + n(ks[4], (DEPTH, A_HEADS), 0.1)
    f_bias = jnp.linspace(3.0, 6.0, A_HEADS)[None, :] + n(ks[5], (DEPTH, A_HEADS), 0.1)
    return {
        "x": n(ks[0], (BATCH, SEQ, D_MODEL), 1.0),
        "norm_mix": 1.0 + n(ks[1], (DEPTH, D_MODEL), 0.02),
        "w_in": n(ks[2], (DEPTH, D_MODEL, IN_COLS), D_MODEL ** -0.5),
        "conv_w": n(ks[3], (DEPTH, CONV_W, 2 * A_QK_W), CONV_W ** -0.5),
        "conv_b": n(ks[6], (DEPTH, 2 * A_QK_W), 0.01),
        "b_gates_if": jnp.concatenate([i_bias, f_bias], axis=-1),
        "norm_head": 1.0 + n(ks[7], (DEPTH, A_V_W), 0.02),
        "w_proj_a": n(ks[8], (DEPTH, A_V_W, D_MODEL), A_V_W ** -0.5),
        "w_proj_b": n(ks[9], (DEPTH, B_W, D_MODEL), B_W ** -0.5),
        "w_merge_gate": n(ks[10], (DEPTH, D_MODEL, 2 * D_MODEL), D_MODEL ** -0.5),
        "b_merge_gate": n(ks[11], (DEPTH, 2 * D_MODEL), 0.01),
        "w_out": n(ks[12], (DEPTH, D_MODEL, D_MODEL), D_MODEL ** -0.5),
        "norm_ffn": 1.0 + n(ks[13], (DEPTH, D_MODEL), 0.02),
        "w_router": n(ks[14], (DEPTH, D_MODEL, N_EXPERTS), D_MODEL ** -0.5),
        "b_router": n(ks[15], (DEPTH, N_EXPERTS), 0.01),
        "w_gate_up": n(ks[16], (DEPTH, N_EXPERTS, D_MODEL, 2 * D_FF), D_MODEL ** -0.5),
        "b_gate_up": n(ks[17], (DEPTH, N_EXPERTS, 2 * D_FF), 0.01),
        "w_down": n(ks[18], (DEPTH, N_EXPERTS, D_FF, D_MODEL), D_FF ** -0.5),
        "b_down": n(ks[19], (DEPTH, N_EXPERTS, D_MODEL), 0.01),
        "norm_final": 1.0 + n(ks[20], (D_MODEL,), 0.02),
    }


def reference(x, norm_mix, w_in, conv_w, conv_b, b_gates_if, norm_head, w_proj_a, w_proj_b,
              w_merge_gate, b_merge_gate, w_out, norm_ffn, w_router, b_router, w_gate_up,
              b_gate_up, w_down, b_down, norm_final):
    bn, s, _ = x.shape
    for l in range(DEPTH):
        h = rmsnorm(x, norm_mix[l])
        proj = h @ w_in[l]
        a_q, a_k, a_v, a_o, a_if, b_q, b_k, b_v = jnp.split(proj, list(SPLITS), axis=-1)
        a_qk = jax.nn.silu(causal_dwconv(jnp.concatenate([a_q, a_k], axis=-1), conv_w[l], conv_b[l]))
        a_q, a_k = a_qk[..., :A_QK_W], a_qk[..., A_QK_W:]
        a_if = a_if + b_gates_if[l]
        y_a = mlstm(a_q.reshape(bn, s, A_HEADS, A_QK), a_k.reshape(bn, s, A_HEADS, A_QK),
                    a_v.reshape(bn, s, A_HEADS, A_V), a_if[..., :A_HEADS], a_if[..., A_HEADS:])
        y_a = (head_rmsnorm(y_a, norm_head[l]) * jax.nn.sigmoid(a_o.astype(jnp.float32))).astype(x.dtype)
        y_b = stick_breaking(b_q.reshape(bn, s, B_HEADS, B_HEAD_DIM),
                             b_k.reshape(bn, s, B_HEADS, B_HEAD_DIM),
                             b_v.reshape(bn, s, B_HEADS, B_HEAD_DIM)).astype(x.dtype)
        gates = jax.nn.sigmoid(h @ w_merge_gate[l] + b_merge_gate[l])
        mixed = (gates[..., :D_MODEL] * (y_a @ w_proj_a[l])
                 + gates[..., D_MODEL:] * (y_b @ w_proj_b[l]))
        x = x + mixed @ w_out[l]
        h = rmsnorm(x, norm_ffn[l])
        x = x + moe(h, w_router[l], b_router[l], w_gate_up[l], b_gate_up[l], w_down[l], b_down[l])
    return rmsnorm(x, norm_final)
```

```python
import functools

import jax
import jax.numpy as jnp
from jax import lax
from jax.experimental import pallas as pl
from jax.experimental.pallas import tpu as pltpu

F32 = jnp.float32
BF16 = jnp.bfloat16

RMS_EPS = 1e-5
A_HEADS = 4
A_QK = 128
A_V = 256
CONV_W = 4
B_HEADS = 8
B_HEAD_DIM = 128
N_EXPERTS = 32
TOP_K = 4
SWIGLU_LIMIT = 7.0
SWIGLU_ALPHA = 1.702

LANE = 128
SUBLANE = 8
VMEM_LIMIT = 56 * 1024 * 1024

MLSTM_CHUNK = 128
SB_BQ = 256
SB_BK = 256
SB_SKIP = 110.0
MOE_TM = 256


def _cparams(sem):
    return pltpu.CompilerParams(dimension_semantics=sem, vmem_limit_bytes=VMEM_LIMIT)


def _log_sigmoid(z):
    return jnp.minimum(z, 0.0) - jnp.log1p(jnp.exp(-jnp.abs(z)))


def _sigmoid(z):
    return 1.0 / (1.0 + jnp.exp(-z))


def _split_bf16(x):
    hi = x.astype(BF16)
    lo = (x - hi.astype(F32)).astype(BF16)
    return hi, lo


def _dot(a, b):
    return jnp.dot(a, b, preferred_element_type=F32)


def _dot_nt(a, b):
    return lax.dot_general(a, b, (((1,), (1,)), ((), ())), preferred_element_type=F32)


def _dot_tn(a, b):
    return lax.dot_general(a, b, (((0,), (0,)), ((), ())), preferred_element_type=F32)


def _norm_proj_kernel(x_ref, g_ref, w_ref, b_ref, o_ref, h_scr, *, sigmoid):
    @pl.when(pl.program_id(1) == 0)
    def _():
        x = x_ref[...]
        ms = jnp.mean(x * x, axis=-1, keepdims=True)
        h_scr[...] = (x * lax.rsqrt(ms + RMS_EPS) * g_ref[...]).astype(BF16)

    acc = _dot(h_scr[...], w_ref[...]) + b_ref[...]
    if sigmoid:
        acc = _sigmoid(acc)
    o_ref[...] = acc.astype(o_ref.dtype)


def _norm_proj(x, g, w, b, *, out_dtype, sigmoid, tm, tn):
    t, d = x.shape
    n = w.shape[1]
    return pl.pallas_call(
        functools.partial(_norm_proj_kernel, sigmoid=sigmoid),
        out_shape=jax.ShapeDtypeStruct((t, n), out_dtype),
        grid=(t // tm, n // tn),
        in_specs=[
            pl.BlockSpec((tm, d), lambda i, j: (i, 0)),
            pl.BlockSpec((1, d), lambda i, j: (0, 0)),
            pl.BlockSpec((d, tn), lambda i, j: (0, j)),
            pl.BlockSpec((1, tn), lambda i, j: (0, j)),
        ],
        out_specs=pl.BlockSpec((tm, tn), lambda i, j: (i, j)),
        scratch_shapes=[pltpu.VMEM((tm, d), BF16)],
        compiler_params=_cparams(("parallel", "arbitrary")),
        name="norm_proj",
    )(x, g, w, b)


def _mlstm_kernel(qk_ref, v_ref, o_ref, ifc_ref, ifr_ref, cw_ref, cb_ref, nh_ref,
                  y_ref, ext_scr, c_scr, n_scr, m_scr):
    L = MLSTM_CHUNK
    step = pl.program_id(0)

    @pl.when(step == 0)
    def _():
        ext_scr[0:SUBLANE, :] = jnp.zeros((SUBLANE, ext_scr.shape[1]), F32)
        c_scr[...] = jnp.zeros_like(c_scr)
        n_scr[...] = jnp.zeros_like(n_scr)
        m_scr[...] = jnp.zeros_like(m_scr)

    raw = qk_ref[...]
    ext_scr[SUBLANE:SUBLANE + L, :] = raw
    conv = cb_ref[...]
    for j in range(CONV_W):
        off = SUBLANE - (CONV_W - 1) + j
        conv = conv + ext_scr[off:off + L, :] * cw_ref[j:j + 1, :]
    ext_scr[0:SUBLANE, :] = raw[L - SUBLANE:L, :]
    qk = conv * _sigmoid(conv)

    row = lax.broadcasted_iota(jnp.int32, (L, L), 0)
    col = lax.broadcasted_iota(jnp.int32, (L, L), 1)
    causal = row >= col
    tri_incl = jnp.where(causal, 1.0, 0.0).astype(BF16)
    tri_incl_t = jnp.where(col >= row, 1.0, 0.0).astype(BF16)

    for h in range(A_HEADS):
        q = qk[:, h * A_QK:(h + 1) * A_QK]
        k = qk[:, A_HEADS * A_QK + h * A_QK:A_HEADS * A_QK + (h + 1) * A_QK] * (A_QK ** -0.5)
        v = v_ref[:, h * A_V:(h + 1) * A_V]
        q_bf = q.astype(BF16)
        k_bf = k.astype(BF16)

        i_col = ifc_ref[:, h:h + 1]
        f_col = ifc_ref[:, A_HEADS + h:A_HEADS + h + 1]
        i_row = ifr_ref[h:h + 1, :]
        f_row = ifr_ref[A_HEADS + h:A_HEADS + h + 1, :]
        lf_col = jnp.broadcast_to(_log_sigmoid(f_col), (L, L))
        lf_row = jnp.broadcast_to(_log_sigmoid(f_row), (L, L))
        hi, lo = _split_bf16(lf_col)
        a_colb = _dot(tri_incl, hi) + _dot(tri_incl, lo)
        hi, lo = _split_bf16(lf_row)
        a_rowb = _dot(hi, tri_incl_t) + _dot(lo, tri_incl_t)
        a_col = a_colb[:, 0:1]
        g_tot = a_colb[L - 1:L, 0:1]

        m_prev = m_scr[h]
        n_prev = n_scr[h]
        ct_prev = c_scr[h]

        m_inter = a_col + m_prev
        d_log = jnp.where(causal, a_colb - a_rowb + i_row, -jnp.inf)
        m_t = jnp.maximum(m_inter, jnp.max(d_log, axis=-1, keepdims=True))
        p = jnp.exp(d_log - m_t) * _dot_nt(q_bf, k_bf)
        s_inter = jnp.exp(m_inter - m_t)
        num = s_inter * _dot(q_bf, ct_prev.astype(BF16)) + _dot(p.astype(BF16), v)
        den = (s_inter * jnp.sum(q * n_prev, axis=-1, keepdims=True)
               + jnp.sum(p, axis=-1, keepdims=True))
        hh = num / jnp.maximum(jnp.abs(den), jnp.exp(-m_t))

        hh = hh * lax.rsqrt(jnp.mean(hh * hh, axis=-1, keepdims=True) + RMS_EPS)
        gate = _sigmoid(o_ref[:, h * A_V:(h + 1) * A_V].astype(F32))
        y_ref[:, h * A_V:(h + 1) * A_V] = (
            hh * nh_ref[:, h * A_V:(h + 1) * A_V] * gate).astype(y_ref.dtype)

        w_col = g_tot - a_col + i_col
        m_loc = jnp.max(w_col, axis=0, keepdims=True)
        ke = k * jnp.exp(w_col - m_loc)
        ct_loc = _dot_tn(ke.astype(BF16), v)
        n_loc = jnp.sum(ke, axis=0, keepdims=True)
        m_new = jnp.maximum(g_tot + m_prev, m_loc)
        s_old = jnp.exp(g_tot + m_prev - m_new)
        s_new = jnp.exp(m_loc - m_new)
        c_scr[h] = s_old * ct_prev + s_new * ct_loc
        n_scr[h] = s_old * n_prev + s_new * n_loc
        m_scr[h] = m_new


def _mlstm(qk_raw, v, o, if_col, if_row, conv_w, conv_b, norm_head):
    t = qk_raw.shape[0]
    L = MLSTM_CHUNK
    wq = qk_raw.shape[1]
    wv = v.shape[1]
    return pl.pallas_call(
        _mlstm_kernel,
        out_shape=jax.ShapeDtypeStruct((t, wv), BF16),
        grid=(t // L,),
        in_specs=[
            pl.BlockSpec((L, wq), lambda i: (i, 0)),
            pl.BlockSpec((L, wv), lambda i: (i, 0)),
            pl.BlockSpec((L, wv), lambda i: (i, 0)),
            pl.BlockSpec((L, LANE), lambda i: (i, 0)),
            pl.BlockSpec((SUBLANE, L), lambda i: (0, i)),
            pl.BlockSpec((CONV_W, wq), lambda i: (0, 0)),
            pl.BlockSpec((1, wq), lambda i: (0, 0)),
            pl.BlockSpec((1, wv), lambda i: (0, 0)),
        ],
        out_specs=pl.BlockSpec((L, wv), lambda i: (i, 0)),
        scratch_shapes=[
            pltpu.VMEM((L + SUBLANE, wq), F32),
            pltpu.VMEM((A_HEADS, A_QK, A_V), F32),
            pltpu.VMEM((A_HEADS, 1, A_QK), F32),
            pltpu.VMEM((A_HEADS, 1, 1), F32),
        ],
        compiler_params=_cparams(("arbitrary",)),
        name="mlstm",
    )(qk_raw, v, o, if_col, if_row, conv_w, conv_b, norm_head)


def _sb_kernel(q_ref, k_ref, v_ref, o_ref, acc_scr, carry_scr):
    bq, bk = SB_BQ, SB_BK
    q0 = pl.program_id(1) * bq
    q = q_ref[...]
    scale = B_HEAD_DIM ** -0.5

    acc_scr[...] = jnp.zeros_like(acc_scr)
    carry_scr[...] = jnp.zeros_like(carry_scr)

    jj = lax.broadcasted_iota(jnp.int32, (bk, bk), 0)
    ss = lax.broadcasted_iota(jnp.int32, (bk, bk), 1)
    later = jnp.where(jj > ss, 1.0, 0.0).astype(BF16)
    tpos = q0 + lax.broadcasted_iota(jnp.int32, (bq, bk), 0)
    srel = lax.broadcasted_iota(jnp.int32, (bq, bk), 1)

    def cond(state):
        kb, mx = state
        return jnp.logical_and(kb >= 0, mx > -SB_SKIP)

    def body(state):
        kb, _ = state
        k0 = pl.multiple_of(kb * bk, bk)
        k = k_ref[pl.ds(k0, bk), :]
        v = v_ref[pl.ds(k0, bk), :]
        z = _dot_nt(q, k) * scale
        ls = _log_sigmoid(z)
        past = (srel + k0) < tpos
        lk = jnp.where(past, ls - z, 0.0)
        hi, lo = _split_bf16(lk)
        carry = carry_scr[...]
        r = _dot(hi, later) + _dot(lo, later) + carry
        att = jnp.where(past, jnp.exp(ls + r), 0.0)
        acc_scr[...] += _dot(att.astype(BF16), v)
        carry = carry + jnp.sum(lk, axis=-1, keepdims=True)
        carry_scr[...] = carry
        return kb - 1, jnp.max(carry)

    kb_start = (q0 + bq - 1) // bk
    lax.while_loop(cond, body, (kb_start, jnp.float32(0.0)))
    o_ref[...] = acc_scr[...].astype(o_ref.dtype)


def _stick_breaking(q, k, v):
    t = q.shape[0]
    d = B_HEAD_DIM
    return pl.pallas_call(
        _sb_kernel,
        out_shape=jax.ShapeDtypeStruct((t, B_HEADS * d), BF16),
        grid=(B_HEADS, t // SB_BQ),
        in_specs=[
            pl.BlockSpec((SB_BQ, d), lambda h, i: (i, h)),
            pl.BlockSpec((t, d), lambda h, i: (0, h)),
            pl.BlockSpec((t, d), lambda h, i: (0, h)),
        ],
        out_specs=pl.BlockSpec((SB_BQ, d), lambda h, i: (i, h)),
        scratch_shapes=[pltpu.VMEM((SB_BQ, d), F32), pltpu.VMEM((SB_BQ, 1), F32)],
        compiler_params=_cparams(("parallel", "arbitrary")),
        name="stick_breaking",
    )(q, k, v)


def _mix_kernel(g1_ref, g2_ref, ya_ref, wa_ref, yb_ref, wb_ref, o_ref):
    pa = _dot(ya_ref[...], wa_ref[...])
    pb = _dot(yb_ref[...], wb_ref[...])
    o_ref[...] = (g1_ref[...].astype(F32) * pa + g2_ref[...].astype(F32) * pb).astype(o_ref.dtype)


def _mix(gates, y_a, w_a, y_b, w_b, *, tm, tn):
    t = y_a.shape[0]
    d = w_a.shape[1]
    nt = d // tn
    return pl.pallas_call(
        _mix_kernel,
        out_shape=jax.ShapeDtypeStruct((t, d), BF16),
        grid=(t // tm, nt),
        in_specs=[
            pl.BlockSpec((tm, tn), lambda i, j: (i, j)),
            pl.BlockSpec((tm, tn), lambda i, j: (i, j + nt)),
            pl.BlockSpec((tm, y_a.shape[1]), lambda i, j: (i, 0)),
            pl.BlockSpec((w_a.shape[0], tn), lambda i, j: (0, j)),
            pl.BlockSpec((tm, y_b.shape[1]), lambda i, j: (i, 0)),
            pl.BlockSpec((w_b.shape[0], tn), lambda i, j: (0, j)),
        ],
        out_specs=pl.BlockSpec((tm, tn), lambda i, j: (i, j)),
        compiler_params=_cparams(("parallel", "arbitrary")),
        name="mix",
    )(gates, gates, y_a, w_a, y_b, w_b)


def _out_kernel(x_ref, mix_ref, wo_ref, g_ref, wr_ref, br_ref, x1_ref, h2_ref, lg_ref):
    x1 = x_ref[...] + _dot(mix_ref[...], wo_ref[...])
    x1_ref[...] = x1
    ms = jnp.mean(x1 * x1, axis=-1, keepdims=True)
    h2 = x1 * lax.rsqrt(ms + RMS_EPS) * g_ref[...]
    h2_ref[...] = h2.astype(h2_ref.dtype)
    lg_ref[...] = jnp.dot(h2, wr_ref[...], preferred_element_type=F32,
                          precision=lax.Precision.HIGHEST) + br_ref[...]


def _out_proj(x, mixed, w_o, g, w_r, b_r, *, tm):
    t, d = x.shape
    return pl.pallas_call(
        _out_kernel,
        out_shape=(jax.ShapeDtypeStruct((t, d), F32),
                   jax.ShapeDtypeStruct((t, d), BF16),
                   jax.ShapeDtypeStruct((t, LANE), F32)),
        grid=(t // tm,),
        in_specs=[
            pl.BlockSpec((tm, d), lambda i: (i, 0)),
            pl.BlockSpec((tm, d), lambda i: (i, 0)),
            pl.BlockSpec((d, d), lambda i: (0, 0)),
            pl.BlockSpec((1, d), lambda i: (0, 0)),
            pl.BlockSpec((d, LANE), lambda i: (0, 0)),
            pl.BlockSpec((1, LANE), lambda i: (0, 0)),
        ],
        out_specs=(pl.BlockSpec((tm, d), lambda i: (i, 0)),
                   pl.BlockSpec((tm, d), lambda i: (i, 0)),
                   pl.BlockSpec((tm, LANE), lambda i: (i, 0))),
        compiler_params=_cparams(("parallel",)),
        name="out_proj",
    )(x, mixed, w_o, g, w_r, b_r)


def _moe_up_kernel(be_ref, nu_ref, x_ref, wg_ref, wu_ref, bg_ref, bu_ref, o_ref):
    @pl.when(pl.program_id(1) < nu_ref[0])
    def _():
        x = x_ref[...]
        gate = jnp.minimum(_dot(x, wg_ref[0]) + bg_ref[0], SWIGLU_LIMIT)
        up = jnp.clip(_dot(x, wu_ref[0]) + bu_ref[0], -SWIGLU_LIMIT, SWIGLU_LIMIT)
        o_ref[...] = ((up + 1.0) * gate * _sigmoid(SWIGLU_ALPHA * gate)).astype(o_ref.dtype)


def _moe_up(block_expert, n_used, x_pad, w_gu, b_gu, *, tf):
    rows, d = x_pad.shape
    nb = rows // MOE_TM
    d_ff = w_gu.shape[2] // 2
    nf = d_ff // tf

    def blk(m, nu):
        return jnp.minimum(m, nu[0] - 1)

    return pl.pallas_call(
        _moe_up_kernel,
        out_shape=jax.ShapeDtypeStruct((rows, d_ff), BF16),
        grid_spec=pltpu.PrefetchScalarGridSpec(
            num_scalar_prefetch=2,
            grid=(nf, nb),
            in_specs=[
                pl.BlockSpec((MOE_TM, d), lambda n, m, be, nu: (blk(m, nu), 0)),
                pl.BlockSpec((1, d, tf), lambda n, m, be, nu: (be[blk(m, nu)], 0, n)),
                pl.BlockSpec((1, d, tf), lambda n, m, be, nu: (be[blk(m, nu)], 0, n + nf)),
                pl.BlockSpec((1, 1, tf), lambda n, m, be, nu: (be[blk(m, nu)], 0, n)),
                pl.BlockSpec((1, 1, tf), lambda n, m, be, nu: (be[blk(m, nu)], 0, n + nf)),
            ],
            out_specs=pl.BlockSpec((MOE_TM, tf), lambda n, m, be, nu: (m, n)),
        ),
        compiler_params=_cparams(("parallel", "arbitrary")),
        name="moe_up",
    )(block_expert, n_used, x_pad, w_gu, w_gu, b_gu, b_gu)


def _moe_down_kernel(be_ref, nu_ref, a_ref, wd_ref, bd_ref, gt_ref, o_ref):
    @pl.when(pl.program_id(1) < nu_ref[0])
    def _():
        y = _dot(a_ref[...], wd_ref[0]) + bd_ref[0]
        o_ref[...] = (gt_ref[...] * y).astype(o_ref.dtype)


def _moe_down(block_expert, n_used, act, w_d, b_d, gate_pad, *, tn):
    rows, d_ff = act.shape
    nb = rows // MOE_TM
    d = w_d.shape[2]

    def blk(m, nu):
        return jnp.minimum(m, nu[0] - 1)

    return pl.pallas_call(
        _moe_down_kernel,
        out_shape=jax.ShapeDtypeStruct((rows, d), F32),
        grid_spec=pltpu.PrefetchScalarGridSpec(
            num_scalar_prefetch=2,
            grid=(d // tn, nb),
            in_specs=[
                pl.BlockSpec((MOE_TM, d_ff), lambda n, m, be, nu: (blk(m, nu), 0)),
                pl.BlockSpec((1, d_ff, tn), lambda n, m, be, nu: (be[blk(m, nu)], 0, n)),
                pl.BlockSpec((1, 1, tn), lambda n, m, be, nu: (be[blk(m, nu)], 0, n)),
                pl.BlockSpec((MOE_TM, 1), lambda n, m, be, nu: (blk(m, nu), 0)),
            ],
            out_specs=pl.BlockSpec((MOE_TM, tn), lambda n, m, be, nu: (m, n)),
        ),
        compiler_params=_cparams(("parallel", "arbitrary")),
        name="moe_down",
    )(block_expert, n_used, act, w_d, b_d, gate_pad)


def _final_kernel(x_ref, y_ref, g_ref, o_ref):
    d = x_ref.shape[1]
    x = x_ref[...]
    for k in range(TOP_K):
        x = x + y_ref[:, k * d:(k + 1) * d]
    ms = jnp.mean(x * x, axis=-1, keepdims=True)
    o_ref[...] = x * lax.rsqrt(ms + RMS_EPS) * g_ref[...]


def _final(x1, y4, g, *, tm):
    t, d = x1.shape
    return pl.pallas_call(
        _final_kernel,
        out_shape=jax.ShapeDtypeStruct((t, d), F32),
        grid=(t // tm,),
        in_specs=[
            pl.BlockSpec((tm, d), lambda i: (i, 0)),
            pl.BlockSpec((tm, TOP_K * d), lambda i: (i, 0)),
            pl.BlockSpec((1, d), lambda i: (0, 0)),
        ],
        out_specs=pl.BlockSpec((tm, d), lambda i: (i, 0)),
        compiler_params=_cparams(("parallel",)),
        name="final_norm",
    )(x1, y4, g)


def _route(logits):
    t = logits.shape[0]
    p_n = t * TOP_K
    top_val, top_idx = lax.top_k(logits, TOP_K)
    gates = jax.nn.softmax(top_val, axis=-1)
    e_flat = top_idx.reshape(p_n)
    onehot = (e_flat[:, None] == jnp.arange(N_EXPERTS, dtype=e_flat.dtype)[None, :]).astype(jnp.int32)
    csum = jnp.cumsum(onehot, axis=0)
    rank = jnp.sum(onehot * csum, axis=1) - 1
    counts = csum[-1]
    padded = (counts + MOE_TM - 1) // MOE_TM * MOE_TM
    pad_end = jnp.cumsum(padded)
    pad_start = pad_end - padded
    dest = pad_start[e_flat] + rank
    nb = -(-p_n // MOE_TM) + N_EXPERTS
    src_tok = jnp.zeros((nb * MOE_TM,), jnp.int32).at[dest].set(
        jnp.arange(p_n, dtype=jnp.int32) // TOP_K)
    gate_pad = jnp.zeros((nb * MOE_TM,), F32).at[dest].set(gates.reshape(p_n))
    block_expert = jnp.minimum(
        jnp.searchsorted(pad_end, jnp.arange(nb, dtype=jnp.int32) * MOE_TM, side='right'),
        N_EXPERTS - 1).astype(jnp.int32)
    n_used = (pad_end[-1] // MOE_TM).astype(jnp.int32).reshape(1)
    return dest, src_tok, gate_pad, block_expert, n_used


def kernel(x, norm_mix, w_in, conv_w, conv_b, b_gates_if, norm_head, w_proj_a, w_proj_b,
           w_merge_gate, b_merge_gate, w_out, norm_ffn, w_router, b_router, w_gate_up,
           b_gate_up, w_down, b_down, norm_final):
    bn, s, d = x.shape
    assert bn == 1
    depth = norm_mix.shape[0]
    xt = x.reshape(s, d)
    tm = min(1024, s)
    aqk = A_HEADS * A_QK
    av = A_HEADS * A_V
    bw = B_HEADS * B_HEAD_DIM

    for l in range(depth):
        wl = w_in[l]
        c0 = 2 * aqk
        c1 = c0 + 2 * av
        c2 = c1 + 2 * A_HEADS
        w_f32 = jnp.concatenate(
            [wl[:, :c0], wl[:, c1:c2], jnp.zeros((d, LANE - 2 * A_HEADS), F32)], axis=1).astype(BF16)
        b_f32 = jnp.concatenate(
            [jnp.zeros((c0,), F32), b_gates_if[l], jnp.zeros((LANE - 2 * A_HEADS,), F32)])[None, :]
        w_bf = jnp.concatenate([wl[:, c0:c1], wl[:, c2:]], axis=1).astype(BF16)
        g_mix = norm_mix[l][None, :]

        p_f32 = _norm_proj(xt, g_mix, w_f32, b_f32, out_dtype=F32, sigmoid=False,
                           tm=tm, tn=w_f32.shape[1])
        p_bf = _norm_proj(xt, g_mix, w_bf, jnp.zeros((1, w_bf.shape[1]), F32),
                          out_dtype=BF16, sigmoid=False, tm=tm, tn=512)
        gates = _norm_proj(xt, g_mix, w_merge_gate[l].astype(BF16), b_merge_gate[l][None, :],
                           out_dtype=BF16, sigmoid=True, tm=tm, tn=512)

        qk_raw = p_f32[:, :c0]
        if_col = p_f32[:, c0:]
        if_row = if_col[:, :SUBLANE].T
        a_v = p_bf[:, :av]
        a_o = p_bf[:, av:2 * av]
        b_q = p_bf[:, 2 * av:2 * av + bw]
        b_k = p_bf[:, 2 * av + bw:2 * av + 2 * bw]
        b_v = p_bf[:, 2 * av + 2 * bw:]

        y_a = _mlstm(qk_raw, a_v, a_o, if_col, if_row, conv_w[l], conv_b[l][None, :],
                     norm_head[l][None, :])
        y_b = _stick_breaking(b_q, b_k, b_v)

        mixed = _mix(gates, y_a, w_proj_a[l].astype(BF16), y_b, w_proj_b[l].astype(BF16),
                     tm=tm, tn=512)
        w_r = jnp.concatenate([w_router[l], jnp.zeros((d, LANE - N_EXPERTS), F32)], axis=1)
        b_r = jnp.concatenate([b_router[l], jnp.zeros((LANE - N_EXPERTS,), F32)])[None, :]
        x1, h2, logits = _out_proj(xt, mixed, w_out[l].astype(BF16), norm_ffn[l][None, :],
                                   w_r, b_r, tm=min(512, s))

        dest, src_tok, gate_pad, block_expert, n_used = _route(logits[:, :N_EXPERTS])
        x_pad = jnp.take(h2, src_tok, axis=0)
        act = _moe_up(block_expert, n_used, x_pad, w_gate_up[l].astype(BF16),
                      b_gate_up[l][:, None, :], tf=512)
        y_pad = _moe_down(block_expert, n_used, act, w_down[l].astype(BF16),
                          b_down[l][:, None, :], gate_pad[:, None], tn=512)
        y4 = jnp.take(y_pad, dest, axis=0).reshape(s, TOP_K * d)
        if l + 1 < depth:
            xt = x1 + jnp.sum(y4.reshape(s, TOP_K, d), axis=1)
        else:
            xt = _final(x1, y4, norm_final[None, :], tm=min(256, s))
    return xt.reshape(bn, s, d)
```

```python
import functools

import jax
import jax.numpy as jnp
from jax import lax
from jax.experimental import pallas as pl
from jax.experimental.pallas import tpu as pltpu

F32 = jnp.float32
BF16 = jnp.bfloat16

RMS_EPS = 1e-5
A_HEADS = 4
A_QK = 128
A_V = 256
CONV_W = 4
B_HEADS = 8
B_HEAD_DIM = 128
N_EXPERTS = 32
TOP_K = 4
SWIGLU_LIMIT = 7.0
SWIGLU_ALPHA = 1.702

LANE = 128
SUBLANE = 8
VMEM_LIMIT = 56 * 1024 * 1024

MLSTM_CHUNK = 128
SB_BQ = 256
SB_BK = 256
SB_SKIP = 110.0
MOE_TM = 512


def _cparams(sem):
    return pltpu.CompilerParams(dimension_semantics=sem, vmem_limit_bytes=VMEM_LIMIT)


def _log_sigmoid(z):
    return jnp.minimum(z, 0.0) - jnp.log1p(jnp.exp(-jnp.abs(z)))


def _sigmoid(z):
    return 1.0 / (1.0 + jnp.exp(-z))


def _split_bf16(x):
    hi = x.astype(BF16)
    lo = (x - hi.astype(F32)).astype(BF16)
    return hi, lo


def _dot(a, b):
    return jnp.dot(a, b, preferred_element_type=F32)


def _dot_nt(a, b):
    return lax.dot_general(a, b, (((1,), (1,)), ((), ())), preferred_element_type=F32)


def _dot_tn(a, b):
    return lax.dot_general(a, b, (((0,), (0,)), ((), ())), preferred_element_type=F32)


def _norm_proj_kernel(x_ref, g_ref, w_ref, b_ref, o_ref, h_scr, *, sigmoid):
    @pl.when(pl.program_id(1) == 0)
    def _():
        x = x_ref[...]
        ms = jnp.mean(x * x, axis=-1, keepdims=True)
        h_scr[...] = (x * lax.rsqrt(ms + RMS_EPS) * g_ref[...]).astype(BF16)

    acc = _dot(h_scr[...], w_ref[...]) + b_ref[...]
    if sigmoid:
        acc = _sigmoid(acc)
    o_ref[...] = acc.astype(o_ref.dtype)


def _norm_proj(x, g, w, b, *, out_dtype, sigmoid, tm, tn):
    t, d = x.shape
    n = w.shape[1]
    return pl.pallas_call(
        functools.partial(_norm_proj_kernel, sigmoid=sigmoid),
        out_shape=jax.ShapeDtypeStruct((t, n), out_dtype),
        grid=(t // tm, n // tn),
        in_specs=[
            pl.BlockSpec((tm, d), lambda i, j: (i, 0)),
            pl.BlockSpec((1, d), lambda i, j: (0, 0)),
            pl.BlockSpec((d, tn), lambda i, j: (0, j)),
            pl.BlockSpec((1, tn), lambda i, j: (0, j)),
        ],
        out_specs=pl.BlockSpec((tm, tn), lambda i, j: (i, j)),
        scratch_shapes=[pltpu.VMEM((tm, d), BF16)],
        compiler_params=_cparams(("parallel", "arbitrary")),
        name="norm_proj",
    )(x, g, w, b)


def _mlstm_kernel(qk_ref, v_ref, o_ref, ifc_ref, ifr_ref, cw_ref, cb_ref, nh_ref,
                  y_ref, ext_scr, c_scr, n_scr, m_scr):
    L = MLSTM_CHUNK
    step = pl.program_id(0)

    @pl.when(step == 0)
    def _():
        ext_scr[0:SUBLANE, :] = jnp.zeros((SUBLANE, ext_scr.shape[1]), F32)
        c_scr[...] = jnp.zeros_like(c_scr)
        n_scr[...] = jnp.zeros_like(n_scr)
        m_scr[...] = jnp.zeros_like(m_scr)

    raw = qk_ref[...]
    ext_scr[SUBLANE:SUBLANE + L, :] = raw
    conv = cb_ref[...]
    for j in range(CONV_W):
        off = SUBLANE - (CONV_W - 1) + j
        conv = conv + ext_scr[off:off + L, :] * cw_ref[j:j + 1, :]
    ext_scr[0:SUBLANE, :] = raw[L - SUBLANE:L, :]
    qk = conv * _sigmoid(conv)

    row = lax.broadcasted_iota(jnp.int32, (L, L), 0)
    col = lax.broadcasted_iota(jnp.int32, (L, L), 1)
    causal = row >= col
    tri_incl = jnp.where(causal, 1.0, 0.0).astype(BF16)
    tri_incl_t = jnp.where(col >= row, 1.0, 0.0).astype(BF16)

    for h in range(A_HEADS):
        q = qk[:, h * A_QK:(h + 1) * A_QK]
        k = qk[:, A_HEADS * A_QK + h * A_QK:A_HEADS * A_QK + (h + 1) * A_QK] * (A_QK ** -0.5)
        v = v_ref[:, h * A_V:(h + 1) * A_V]
        q_bf = q.astype(BF16)
        k_bf = k.astype(BF16)

        i_col = ifc_ref[:, h:h + 1]
        f_col = ifc_ref[:, A_HEADS + h:A_HEADS + h + 1]
        i_row = ifr_ref[h:h + 1, :]
        f_row = ifr_ref[A_HEADS + h:A_HEADS + h + 1, :]
        lf_col = jnp.broadcast_to(_log_sigmoid(f_col), (L, L))
        lf_row = jnp.broadcast_to(_log_sigmoid(f_row), (L, L))
        hi, lo = _split_bf16(lf_col)
        a_colb = _dot(tri_incl, hi) + _dot(tri_incl, lo)
        hi, lo = _split_bf16(lf_row)
        a_rowb = _dot(hi, tri_incl_t) + _dot(lo, tri_incl_t)
        a_col = a_colb[:, 0:1]
        g_tot = a_colb[L - 1:L, 0:1]

        m_prev = m_scr[h]
        n_prev = n_scr[h]
        ct_prev = c_scr[h]

        m_inter = a_col + m_prev
        d_log = jnp.where(causal, a_colb - a_rowb + i_row, -jnp.inf)
        m_t = jnp.maximum(m_inter, jnp.max(d_log, axis=-1, keepdims=True))
        p = jnp.exp(d_log - m_t) * _dot_nt(q_bf, k_bf)
        s_inter = jnp.exp(m_inter - m_t)
        num = s_inter * _dot(q_bf, ct_prev.astype(BF16)) + _dot(p.astype(BF16), v)
        den = (s_inter * jnp.sum(q * n_prev, axis=-1, keepdims=True)
               + jnp.sum(p, axis=-1, keepdims=True))
        hh = num / jnp.maximum(jnp.abs(den), jnp.exp(-m_t))

        hh = hh * lax.rsqrt(jnp.mean(hh * hh, axis=-1, keepdims=True) + RMS_EPS)
        gate = _sigmoid(o_ref[:, h * A_V:(h + 1) * A_V].astype(F32))
        y_ref[:, h * A_V:(h + 1) * A_V] = (
            hh * nh_ref[:, h * A_V:(h + 1) * A_V] * gate).astype(y_ref.dtype)

        w_col = g_tot - a_col + i_col
        m_loc = jnp.max(w_col, axis=0, keepdims=True)
        ke = k * jnp.exp(w_col - m_loc)
        ct_loc = _dot_tn(ke.astype(BF16), v)
        n_loc = jnp.sum(ke, axis=0, keepdims=True)
        m_new = jnp.maximum(g_tot + m_prev, m_loc)
        s_old = jnp.exp(g_tot + m_prev - m_new)
        s_new = jnp.exp(m_loc - m_new)
        c_scr[h] = s_old * ct_prev + s_new * ct_loc
        n_scr[h] = s_old * n_prev + s_new * n_loc
        m_scr[h] = m_new


def _mlstm(qk_raw, v, o, if_col, if_row, conv_w, conv_b, norm_head):
    t = qk_raw.shape[0]
    L = MLSTM_CHUNK
    wq = qk_raw.shape[1]
    wv = v.shape[1]
    return pl.pallas_call(
        _mlstm_kernel,
        out_shape=jax.ShapeDtypeStruct((t, wv), BF16),
        grid=(t // L,),
        in_specs=[
            pl.BlockSpec((L, wq), lambda i: (i, 0)),
            pl.BlockSpec((L, wv), lambda i: (i, 0)),
            pl.BlockSpec((L, wv), lambda i: (i, 0)),
            pl.BlockSpec((L, LANE), lambda i: (i, 0)),
            pl.BlockSpec((SUBLANE, L), lambda i: (0, i)),
            pl.BlockSpec((CONV_W, wq), lambda i: (0, 0)),
            pl.BlockSpec((1, wq), lambda i: (0, 0)),
            pl.BlockSpec((1, wv), lambda i: (0, 0)),
        ],
        out_specs=pl.BlockSpec((L, wv), lambda i: (i, 0)),
        scratch_shapes=[
            pltpu.VMEM((L + SUBLANE, wq), F32),
            pltpu.VMEM((A_HEADS, A_QK, A_V), F32),
            pltpu.VMEM((A_HEADS, 1, A_QK), F32),
            pltpu.VMEM((A_HEADS, 1, 1), F32),
        ],
        compiler_params=_cparams(("arbitrary",)),
        name="mlstm",
    )(qk_raw, v, o, if_col, if_row, conv_w, conv_b, norm_head)


def _sb_kernel(q_ref, k_ref, v_ref, o_ref, acc_scr, carry_scr):
    bq, bk = SB_BQ, SB_BK
    q0 = pl.program_id(1) * bq
    q = q_ref[...]
    scale = B_HEAD_DIM ** -0.5

    acc_scr[...] = jnp.zeros_like(acc_scr)
    carry_scr[...] = jnp.zeros_like(carry_scr)

    jj = lax.broadcasted_iota(jnp.int32, (bk, bk), 0)
    ss = lax.broadcasted_iota(jnp.int32, (bk, bk), 1)
    later = jnp.where(jj > ss, 1.0, 0.0).astype(BF16)
    tpos = q0 + lax.broadcasted_iota(jnp.int32, (bq, bk), 0)
    srel = lax.broadcasted_iota(jnp.int32, (bq, bk), 1)

    def cond(state):
        kb, mx = state
        return jnp.logical_and(kb >= 0, mx > -SB_SKIP)

    def body(state):
        kb, _ = state
        k0 = pl.multiple_of(kb * bk, bk)
        k = k_ref[pl.ds(k0, bk), :]
        v = v_ref[pl.ds(k0, bk), :]
        z = _dot_nt(q, k) * scale
        ls = _log_sigmoid(z)
        past = (srel + k0) < tpos
        lk = jnp.where(past, ls - z, 0.0)
        hi, lo = _split_bf16(lk)
        carry = carry_scr[...]
        r = _dot(hi, later) + _dot(lo, later) + carry
        att = jnp.where(past, jnp.exp(ls + r), 0.0)
        acc_scr[...] += _dot(att.astype(BF16), v)
        carry = carry + jnp.sum(lk, axis=-1, keepdims=True)
        carry_scr[...] = carry
        return kb - 1, jnp.max(carry)

    kb_start = (q0 + bq - 1) // bk
    lax.while_loop(cond, body, (kb_start, jnp.float32(0.0)))
    o_ref[...] = acc_scr[...].astype(o_ref.dtype)


def _stick_breaking(q, k, v):
    t = q.shape[0]
    d = B_HEAD_DIM
    return pl.pallas_call(
        _sb_kernel,
        out_shape=jax.ShapeDtypeStruct((t, B_HEADS * d), BF16),
        grid=(B_HEADS, t // SB_BQ),
        in_specs=[
            pl.BlockSpec((SB_BQ, d), lambda h, i: (i, h)),
            pl.BlockSpec((t, d), lambda h, i: (0, h)),
            pl.BlockSpec((t, d), lambda h, i: (0, h)),
        ],
        out_specs=pl.BlockSpec((SB_BQ, d), lambda h, i: (i, h)),
        scratch_shapes=[pltpu.VMEM((SB_BQ, d), F32), pltpu.VMEM((SB_BQ, 1), F32)],
        compiler_params=_cparams(("parallel", "arbitrary")),
        name="stick_breaking",
    )(q, k, v)


def _mix_kernel(g1_ref, g2_ref, ya_ref, wa_ref, yb_ref, wb_ref, o_ref):
    pa = _dot(ya_ref[...], wa_ref[...])
    pb = _dot(yb_ref[...], wb_ref[...])
    o_ref[...] = (g1_ref[...].astype(F32) * pa + g2_ref[...].astype(F32) * pb).astype(o_ref.dtype)


def _mix(gates, y_a, w_a, y_b, w_b, *, tm, tn):
    t = y_a.shape[0]
    d = w_a.shape[1]
    nt = d // tn
    return pl.pallas_call(
        _mix_kernel,
        out_shape=jax.ShapeDtypeStruct((t, d), BF16),
        grid=(t // tm, nt),
        in_specs=[
            pl.BlockSpec((tm, tn), lambda i, j: (i, j)),
            pl.BlockSpec((tm, tn), lambda i, j: (i, j + nt)),
            pl.BlockSpec((tm, y_a.shape[1]), lambda i, j: (i, 0)),
            pl.BlockSpec((w_a.shape[0], tn), lambda i, j: (0, j)),
            pl.BlockSpec((tm, y_b.shape[1]), lambda i, j: (i, 0)),
            pl.BlockSpec((w_b.shape[0], tn), lambda i, j: (0, j)),
        ],
        out_specs=pl.BlockSpec((tm, tn), lambda i, j: (i, j)),
        compiler_params=_cparams(("parallel", "arbitrary")),
        name="mix",
    )(gates, gates, y_a, w_a, y_b, w_b)


def _out_kernel(x_ref, mix_ref, wo_ref, g_ref, wr_ref, br_ref,
                x1_ref, h2_ref, ri_ref, rg_ref, cnt_ref):
    tm = x_ref.shape[0]
    x1 = x_ref[...] + _dot(mix_ref[...], wo_ref[...])
    x1_ref[...] = x1
    ms = jnp.mean(x1 * x1, axis=-1, keepdims=True)
    h2 = x1 * lax.rsqrt(ms + RMS_EPS) * g_ref[...]
    h2_ref[...] = h2.astype(h2_ref.dtype)
    logits = jnp.dot(h2, wr_ref[...], preferred_element_type=F32,
                     precision=lax.Precision.HIGHEST) + br_ref[...]

    @pl.when(pl.program_id(0) == 0)
    def _():
        cnt_ref[...] = jnp.zeros_like(cnt_ref)

    lane = lax.broadcasted_iota(jnp.int32, (tm, LANE), 1)
    lg = jnp.where(lane < N_EXPERTS, logits, -jnp.inf)
    vals, idxs, hots = [], [], []
    for _ in range(TOP_K):
        mx = jnp.max(lg, axis=-1, keepdims=True)
        idx = jnp.min(jnp.where(lg == mx, lane, LANE), axis=-1, keepdims=True)
        hot = lane == idx
        vals.append(mx)
        idxs.append(idx)
        hots.append(hot)
        lg = jnp.where(hot, -jnp.inf, lg)

    exps = [jnp.exp(v - vals[0]) for v in vals]
    denom = exps[0]
    for e in exps[1:]:
        denom = denom + e
    rg = jnp.zeros((tm, LANE), F32)
    for k in range(TOP_K):
        rg = jnp.where(lane == k, exps[k] / denom, rg)
    rg_ref[...] = rg

    chosen = hots[0]
    for hot in hots[1:]:
        chosen = jnp.logical_or(chosen, hot)
    chosen_f = jnp.where(chosen, 1.0, 0.0)
    r = lax.broadcasted_iota(jnp.int32, (tm, tm), 0)
    c = lax.broadcasted_iota(jnp.int32, (tm, tm), 1)
    earlier = jnp.where(c < r, 1.0, 0.0).astype(BF16)
    before = _dot(earlier, chosen_f.astype(BF16)) + cnt_ref[...]
    cnt_ref[...] += jnp.sum(chosen_f, axis=0, keepdims=True)

    ri = jnp.zeros((tm, LANE), jnp.int32)
    for k in range(TOP_K):
        rank = jnp.sum(jnp.where(hots[k], before, 0.0), axis=-1, keepdims=True).astype(jnp.int32)
        ri = jnp.where(lane == k, idxs[k], ri)
        ri = jnp.where(lane == TOP_K + k, rank, ri)
    ri_ref[...] = ri


def _out_proj(x, mixed, w_o, g, w_r, b_r, *, tm):
    t, d = x.shape
    return pl.pallas_call(
        _out_kernel,
        out_shape=(jax.ShapeDtypeStruct((t, d), F32),
                   jax.ShapeDtypeStruct((t, d), BF16),
                   jax.ShapeDtypeStruct((t, LANE), jnp.int32),
                   jax.ShapeDtypeStruct((t, LANE), F32),
                   jax.ShapeDtypeStruct((1, LANE), F32)),
        grid=(t // tm,),
        in_specs=[
            pl.BlockSpec((tm, d), lambda i: (i, 0)),
            pl.BlockSpec((tm, d), lambda i: (i, 0)),
            pl.BlockSpec((d, d), lambda i: (0, 0)),
            pl.BlockSpec((1, d), lambda i: (0, 0)),
            pl.BlockSpec((d, LANE), lambda i: (0, 0)),
            pl.BlockSpec((1, LANE), lambda i: (0, 0)),
        ],
        out_specs=(pl.BlockSpec((tm, d), lambda i: (i, 0)),
                   pl.BlockSpec((tm, d), lambda i: (i, 0)),
                   pl.BlockSpec((tm, LANE), lambda i: (i, 0)),
                   pl.BlockSpec((tm, LANE), lambda i: (i, 0)),
                   pl.BlockSpec((1, LANE), lambda i: (0, 0))),
        compiler_params=_cparams(("arbitrary",)),
        name="out_proj",
    )(x, mixed, w_o, g, w_r, b_r)


def _expert_changed(be_ref, m):
    return jnp.logical_or(m == 0, be_ref[m] != be_ref[jnp.maximum(m - 1, 0)])


def _moe_up_kernel(be_ref, nu_ref, x_ref, wg_ref, wu_ref, bg_ref, bu_ref, o_ref, wg_scr, wu_scr):
    m = pl.program_id(1)

    @pl.when(m < nu_ref[0])
    def _():
        @pl.when(_expert_changed(be_ref, m))
        def _():
            wg_scr[...] = wg_ref[0].astype(BF16)
            wu_scr[...] = wu_ref[0].astype(BF16)

        x = x_ref[...]
        gate = jnp.minimum(_dot(x, wg_scr[...]) + bg_ref[0], SWIGLU_LIMIT)
        up = jnp.clip(_dot(x, wu_scr[...]) + bu_ref[0], -SWIGLU_LIMIT, SWIGLU_LIMIT)
        o_ref[...] = ((up + 1.0) * gate * _sigmoid(SWIGLU_ALPHA * gate)).astype(o_ref.dtype)


def _moe_up(block_expert, n_used, x_pad, w_gu, b_gu, *, tf):
    rows, d = x_pad.shape
    nb = rows // MOE_TM
    d_ff = w_gu.shape[2] // 2
    nf = d_ff // tf

    def blk(m, nu):
        return jnp.minimum(m, nu[0] - 1)

    return pl.pallas_call(
        _moe_up_kernel,
        out_shape=jax.ShapeDtypeStruct((rows, d_ff), BF16),
        grid_spec=pltpu.PrefetchScalarGridSpec(
            num_scalar_prefetch=2,
            grid=(nf, nb),
            in_specs=[
                pl.BlockSpec((MOE_TM, d), lambda n, m, be, nu: (blk(m, nu), 0)),
                pl.BlockSpec((1, d, tf), lambda n, m, be, nu: (be[blk(m, nu)], 0, n)),
                pl.BlockSpec((1, d, tf), lambda n, m, be, nu: (be[blk(m, nu)], 0, n + nf)),
                pl.BlockSpec((1, 1, tf), lambda n, m, be, nu: (be[blk(m, nu)], 0, n)),
                pl.BlockSpec((1, 1, tf), lambda n, m, be, nu: (be[blk(m, nu)], 0, n + nf)),
            ],
            out_specs=pl.BlockSpec((MOE_TM, tf), lambda n, m, be, nu: (m, n)),
            scratch_shapes=[pltpu.VMEM((d, tf), BF16), pltpu.VMEM((d, tf), BF16)],
        ),
        compiler_params=_cparams(("arbitrary", "arbitrary")),
        name="moe_up",
    )(block_expert, n_used, x_pad, w_gu, w_gu, b_gu, b_gu)


def _moe_down_kernel(be_ref, nu_ref, a_ref, wd_ref, bd_ref, o_ref, wd_scr):
    m = pl.program_id(1)

    @pl.when(m < nu_ref[0])
    def _():
        @pl.when(_expert_changed(be_ref, m))
        def _():
            wd_scr[...] = wd_ref[0].astype(BF16)

        o_ref[...] = (_dot(a_ref[...], wd_scr[...]) + bd_ref[0]).astype(o_ref.dtype)


def _moe_down(block_expert, n_used, act, w_d, b_d, *, tn):
    rows, d_ff = act.shape
    nb = rows // MOE_TM
    d = w_d.shape[2]

    def blk(m, nu):
        return jnp.minimum(m, nu[0] - 1)

    return pl.pallas_call(
        _moe_down_kernel,
        out_shape=jax.ShapeDtypeStruct((rows, d), BF16),
        grid_spec=pltpu.PrefetchScalarGridSpec(
            num_scalar_prefetch=2,
            grid=(d // tn, nb),
            in_specs=[
                pl.BlockSpec((MOE_TM, d_ff), lambda n, m, be, nu: (blk(m, nu), 0)),
                pl.BlockSpec((1, d_ff, tn), lambda n, m, be, nu: (be[blk(m, nu)], 0, n)),
                pl.BlockSpec((1, 1, tn), lambda n, m, be, nu: (be[blk(m, nu)], 0, n)),
            ],
            out_specs=pl.BlockSpec((MOE_TM, tn), lambda n, m, be, nu: (m, n)),
            scratch_shapes=[pltpu.VMEM((d_ff, tn), BF16)],
        ),
        compiler_params=_cparams(("arbitrary", "arbitrary")),
        name="moe_down",
    )(block_expert, n_used, act, w_d, b_d)


def _combine_kernel(x_ref, y0_ref, y1_ref, y2_ref, y3_ref, rg_ref, g_ref, o_ref, *, normalize):
    x = x_ref[...]
    rg = rg_ref[...]
    for k, y_ref in enumerate((y0_ref, y1_ref, y2_ref, y3_ref)):
        x = x + rg[:, k:k + 1] * y_ref[...].astype(F32)
    if normalize:
        ms = jnp.mean(x * x, axis=-1, keepdims=True)
        x = x * lax.rsqrt(ms + RMS_EPS) * g_ref[...]
    o_ref[...] = x


def _combine(x1, ys, rg, g, *, normalize, tm):
    t, d = x1.shape
    row_spec = pl.BlockSpec((tm, d), lambda i: (i, 0))
    return pl.pallas_call(
        functools.partial(_combine_kernel, normalize=normalize),
        out_shape=jax.ShapeDtypeStruct((t, d), F32),
        grid=(t // tm,),
        in_specs=[row_spec] * (1 + TOP_K) + [
            pl.BlockSpec((tm, LANE), lambda i: (i, 0)),
            pl.BlockSpec((1, d), lambda i: (0, 0)),
        ],
        out_specs=row_spec,
        compiler_params=_cparams(("parallel",)),
        name="combine_norm",
    )(x1, *ys, rg, g)


def _route(ri, counts, t):
    p_n = t * TOP_K
    experts = ri[:, :TOP_K]
    rank = ri[:, TOP_K:2 * TOP_K]
    counts = counts[0, :N_EXPERTS].astype(jnp.int32)
    padded = (counts + MOE_TM - 1) // MOE_TM * MOE_TM
    pad_end = jnp.cumsum(padded)
    pad_start = pad_end - padded
    dest = pad_start[experts] + rank
    nb = -(-p_n // MOE_TM) + N_EXPERTS
    src_tok = jnp.zeros((nb * MOE_TM,), jnp.int32).at[dest.reshape(p_n)].set(
        jnp.arange(p_n, dtype=jnp.int32) // TOP_K)
    block_start = jnp.arange(nb, dtype=jnp.int32) * MOE_TM
    block_expert = jnp.minimum(
        jnp.sum((block_start[:, None] >= pad_end[None, :]).astype(jnp.int32), axis=1),
        N_EXPERTS - 1)
    n_used = (pad_end[-1] // MOE_TM).astype(jnp.int32).reshape(1)
    return dest, src_tok, block_expert, n_used


def kernel(x, norm_mix, w_in, conv_w, conv_b, b_gates_if, norm_head, w_proj_a, w_proj_b,
           w_merge_gate, b_merge_gate, w_out, norm_ffn, w_router, b_router, w_gate_up,
           b_gate_up, w_down, b_down, norm_final):
    bn, s, d = x.shape
    assert bn == 1
    depth = norm_mix.shape[0]
    xt = x.reshape(s, d)
    tm = min(1024, s)
    aqk = A_HEADS * A_QK
    av = A_HEADS * A_V
    bw = B_HEADS * B_HEAD_DIM

    for l in range(depth):
        wl = w_in[l]
        c0 = 2 * aqk
        c1 = c0 + 2 * av
        c2 = c1 + 2 * A_HEADS
        w_f32 = jnp.concatenate(
            [wl[:, :c0], wl[:, c1:c2], jnp.zeros((d, LANE - 2 * A_HEADS), F32)], axis=1).astype(BF16)
        b_f32 = jnp.concatenate(
            [jnp.zeros((c0,), F32), b_gates_if[l], jnp.zeros((LANE - 2 * A_HEADS,), F32)])[None, :]
        w_bf = jnp.concatenate([wl[:, c0:c1], wl[:, c2:]], axis=1).astype(BF16)
        g_mix = norm_mix[l][None, :]

        p_f32 = _norm_proj(xt, g_mix, w_f32, b_f32, out_dtype=F32, sigmoid=False,
                           tm=tm, tn=w_f32.shape[1])
        p_bf = _norm_proj(xt, g_mix, w_bf, jnp.zeros((1, w_bf.shape[1]), F32),
                          out_dtype=BF16, sigmoid=False, tm=tm, tn=512)
        gates = _norm_proj(xt, g_mix, w_merge_gate[l].astype(BF16), b_merge_gate[l][None, :],
                           out_dtype=BF16, sigmoid=True, tm=tm, tn=512)

        qk_raw = p_f32[:, :c0]
        if_col = p_f32[:, c0:]
        if_row = if_col[:, :SUBLANE].T
        a_v = p_bf[:, :av]
        a_o = p_bf[:, av:2 * av]
        b_q = p_bf[:, 2 * av:2 * av + bw]
        b_k = p_bf[:, 2 * av + bw:2 * av + 2 * bw]
        b_v = p_bf[:, 2 * av + 2 * bw:]

        y_a = _mlstm(qk_raw, a_v, a_o, if_col, if_row, conv_w[l], conv_b[l][None, :],
                     norm_head[l][None, :])
        y_b = _stick_breaking(b_q, b_k, b_v)

        mixed = _mix(gates, y_a, w_proj_a[l].astype(BF16), y_b, w_proj_b[l].astype(BF16),
                     tm=tm, tn=512)
        w_r = jnp.concatenate([w_router[l], jnp.zeros((d, LANE - N_EXPERTS), F32)], axis=1)
        b_r = jnp.concatenate([b_router[l], jnp.zeros((LANE - N_EXPERTS,), F32)])[None, :]
        x1, h2, ri, rg, counts = _out_proj(xt, mixed, w_out[l].astype(BF16), norm_ffn[l][None, :],
                                           w_r, b_r, tm=min(512, s))

        dest, src_tok, block_expert, n_used = _route(ri, counts, s)
        x_pad = jnp.take(h2, src_tok, axis=0)
        act = _moe_up(block_expert, n_used, x_pad, w_gate_up[l], b_gate_up[l][:, None, :], tf=512)
        y_pad = _moe_down(block_expert, n_used, act, w_down[l], b_down[l][:, None, :], tn=1024)
        ys = [jnp.take(y_pad, dest[:, k], axis=0) for k in range(TOP_K)]
        xt = _combine(x1, ys, rg, norm_final[None, :], normalize=(l + 1 == depth), tm=min(512, s))
    return xt.reshape(bn, s, d)
```

```python
import functools

import jax
import jax.numpy as jnp
from jax import lax
from jax.experimental import pallas as pl
from jax.experimental.pallas import tpu as pltpu

F32 = jnp.float32
BF16 = jnp.bfloat16

RMS_EPS = 1e-5
A_HEADS = 4
A_QK = 128
A_V = 256
CONV_W = 4
B_HEADS = 8
B_HEAD_DIM = 128
N_EXPERTS = 32
TOP_K = 4
SWIGLU_LIMIT = 7.0
SWIGLU_ALPHA = 1.702

LANE = 128
SUBLANE = 8
VMEM_LIMIT = 60 * 1024 * 1024

MLSTM_CHUNK = 128
SB_BQ = 256
SB_BK = 256
SB_SKIP = 110.0
MOE_TM = 512


def _cparams(sem):
    return pltpu.CompilerParams(dimension_semantics=sem, vmem_limit_bytes=VMEM_LIMIT)


def _log_sigmoid(z):
    return jnp.minimum(z, 0.0) - jnp.log1p(jnp.exp(-jnp.abs(z)))


def _sigmoid(z):
    return 1.0 / (1.0 + jnp.exp(-z))


def _split_bf16(x):
    hi = x.astype(BF16)
    lo = (x - hi.astype(F32)).astype(BF16)
    return hi, lo


def _dot(a, b):
    return jnp.dot(a, b, preferred_element_type=F32)


def _dot_nt(a, b):
    return lax.dot_general(a, b, (((1,), (1,)), ((), ())), preferred_element_type=F32)


def _dot_tn(a, b):
    return lax.dot_general(a, b, (((0,), (0,)), ((), ())), preferred_element_type=F32)


def _norm_proj_kernel(x_ref, g_ref, w_ref, b_ref, o_ref, h_scr, *, sigmoid):
    @pl.when(pl.program_id(1) == 0)
    def _():
        x = x_ref[...]
        ms = jnp.mean(x * x, axis=-1, keepdims=True)
        h_scr[...] = (x * lax.rsqrt(ms + RMS_EPS) * g_ref[...]).astype(BF16)

    acc = _dot(h_scr[...], w_ref[...]) + b_ref[...]
    if sigmoid:
        acc = _sigmoid(acc)
    o_ref[...] = acc.astype(o_ref.dtype)


def _norm_proj(x, g, w, b, *, out_dtype, sigmoid, tm, tn):
    t, d = x.shape
    n = w.shape[1]
    return pl.pallas_call(
        functools.partial(_norm_proj_kernel, sigmoid=sigmoid),
        out_shape=jax.ShapeDtypeStruct((t, n), out_dtype),
        grid=(t // tm, n // tn),
        in_specs=[
            pl.BlockSpec((tm, d), lambda i, j: (i, 0)),
            pl.BlockSpec((1, d), lambda i, j: (0, 0)),
            pl.BlockSpec((d, tn), lambda i, j: (0, j)),
            pl.BlockSpec((1, tn), lambda i, j: (0, j)),
        ],
        out_specs=pl.BlockSpec((tm, tn), lambda i, j: (i, j)),
        scratch_shapes=[pltpu.VMEM((tm, d), BF16)],
        compiler_params=_cparams(("parallel", "arbitrary")),
        name="norm_proj",
    )(x, g, w, b)


def _mlstm_kernel(qk_ref, v_ref, o_ref, ifc_ref, ifr_ref, cw_ref, cb_ref, nh_ref,
                  y_ref, ext_scr, c_scr, n_scr, m_scr):
    L = MLSTM_CHUNK
    step = pl.program_id(0)

    @pl.when(step == 0)
    def _():
        ext_scr[0:SUBLANE, :] = jnp.zeros((SUBLANE, ext_scr.shape[1]), F32)
        c_scr[...] = jnp.zeros_like(c_scr)
        n_scr[...] = jnp.zeros_like(n_scr)
        m_scr[...] = jnp.zeros_like(m_scr)

    raw = qk_ref[...]
    ext_scr[SUBLANE:SUBLANE + L, :] = raw
    conv = cb_ref[...]
    for j in range(CONV_W):
        off = SUBLANE - (CONV_W - 1) + j
        conv = conv + ext_scr[off:off + L, :] * cw_ref[j:j + 1, :]
    ext_scr[0:SUBLANE, :] = raw[L - SUBLANE:L, :]
    qk = conv * _sigmoid(conv)

    row = lax.broadcasted_iota(jnp.int32, (L, L), 0)
    col = lax.broadcasted_iota(jnp.int32, (L, L), 1)
    causal = row >= col
    tri_incl = jnp.where(causal, 1.0, 0.0).astype(BF16)
    tri_incl_t = jnp.where(col >= row, 1.0, 0.0).astype(BF16)

    for h in range(A_HEADS):
        q = qk[:, h * A_QK:(h + 1) * A_QK]
        k = qk[:, A_HEADS * A_QK + h * A_QK:A_HEADS * A_QK + (h + 1) * A_QK] * (A_QK ** -0.5)
        v = v_ref[:, h * A_V:(h + 1) * A_V]
        q_bf = q.astype(BF16)
        k_bf = k.astype(BF16)

        i_col = ifc_ref[:, h:h + 1]
        f_col = ifc_ref[:, A_HEADS + h:A_HEADS + h + 1]
        i_row = ifr_ref[h:h + 1, :]
        f_row = ifr_ref[A_HEADS + h:A_HEADS + h + 1, :]
        lf_col = jnp.broadcast_to(_log_sigmoid(f_col), (L, L))
        lf_row = jnp.broadcast_to(_log_sigmoid(f_row), (L, L))
        hi, lo = _split_bf16(lf_col)
        a_colb = _dot(tri_incl, hi) + _dot(tri_incl, lo)
        hi, lo = _split_bf16(lf_row)
        a_rowb = _dot(hi, tri_incl_t) + _dot(lo, tri_incl_t)
        a_col = a_colb[:, 0:1]
        g_tot = a_colb[L - 1:L, 0:1]

        m_prev = m_scr[h]
        n_prev = n_scr[h]
        ct_prev = c_scr[h]

        m_inter = a_col + m_prev
        d_log = jnp.where(causal, a_colb - a_rowb + i_row, -jnp.inf)
        m_t = jnp.maximum(m_inter, jnp.max(d_log, axis=-1, keepdims=True))
        p = jnp.exp(d_log - m_t) * _dot_nt(q_bf, k_bf)
        s_inter = jnp.exp(m_inter - m_t)
        num = s_inter * _dot(q_bf, ct_prev.astype(BF16)) + _dot(p.astype(BF16), v)
        den = (s_inter * jnp.sum(q * n_prev, axis=-1, keepdims=True)
               + jnp.sum(p, axis=-1, keepdims=True))
        hh = num / jnp.maximum(jnp.abs(den), jnp.exp(-m_t))

        hh = hh * lax.rsqrt(jnp.mean(hh * hh, axis=-1, keepdims=True) + RMS_EPS)
        gate = _sigmoid(o_ref[:, h * A_V:(h + 1) * A_V].astype(F32))
        y_ref[:, h * A_V:(h + 1) * A_V] = (
            hh * nh_ref[:, h * A_V:(h + 1) * A_V] * gate).astype(y_ref.dtype)

        w_col = g_tot - a_col + i_col
        m_loc = jnp.max(w_col, axis=0, keepdims=True)
        ke = k * jnp.exp(w_col - m_loc)
        ct_loc = _dot_tn(ke.astype(BF16), v)
        n_loc = jnp.sum(ke, axis=0, keepdims=True)
        m_new = jnp.maximum(g_tot + m_prev, m_loc)
        s_old = jnp.exp(g_tot + m_prev - m_new)
        s_new = jnp.exp(m_loc - m_new)
        c_scr[h] = s_old * ct_prev + s_new * ct_loc
        n_scr[h] = s_old * n_prev + s_new * n_loc
        m_scr[h] = m_new


def _mlstm(qk_raw, v, o, if_col, if_row, conv_w, conv_b, norm_head):
    t = qk_raw.shape[0]
    L = MLSTM_CHUNK
    wq = qk_raw.shape[1]
    wv = v.shape[1]
    return pl.pallas_call(
        _mlstm_kernel,
        out_shape=jax.ShapeDtypeStruct((t, wv), BF16),
        grid=(t // L,),
        in_specs=[
            pl.BlockSpec((L, wq), lambda i: (i, 0)),
            pl.BlockSpec((L, wv), lambda i: (i, 0)),
            pl.BlockSpec((L, wv), lambda i: (i, 0)),
            pl.BlockSpec((L, LANE), lambda i: (i, 0)),
            pl.BlockSpec((SUBLANE, L), lambda i: (0, i)),
            pl.BlockSpec((CONV_W, wq), lambda i: (0, 0)),
            pl.BlockSpec((1, wq), lambda i: (0, 0)),
            pl.BlockSpec((1, wv), lambda i: (0, 0)),
        ],
        out_specs=pl.BlockSpec((L, wv), lambda i: (i, 0)),
        scratch_shapes=[
            pltpu.VMEM((L + SUBLANE, wq), F32),
            pltpu.VMEM((A_HEADS, A_QK, A_V), F32),
            pltpu.VMEM((A_HEADS, 1, A_QK), F32),
            pltpu.VMEM((A_HEADS, 1, 1), F32),
        ],
        compiler_params=_cparams(("arbitrary",)),
        name="mlstm",
    )(qk_raw, v, o, if_col, if_row, conv_w, conv_b, norm_head)


def _sb_kernel(q_ref, k_ref, v_ref, o_ref, acc_scr, carry_scr):
    bq, bk = SB_BQ, SB_BK
    q0 = pl.program_id(1) * bq
    q = q_ref[...]
    scale = B_HEAD_DIM ** -0.5

    acc_scr[...] = jnp.zeros_like(acc_scr)
    carry_scr[...] = jnp.zeros_like(carry_scr)

    jj = lax.broadcasted_iota(jnp.int32, (bk, bk), 0)
    ss = lax.broadcasted_iota(jnp.int32, (bk, bk), 1)
    later = jnp.where(jj > ss, 1.0, 0.0).astype(BF16)
    tpos = q0 + lax.broadcasted_iota(jnp.int32, (bq, bk), 0)
    srel = lax.broadcasted_iota(jnp.int32, (bq, bk), 1)

    def cond(state):
        kb, mx = state
        return jnp.logical_and(kb >= 0, mx > -SB_SKIP)

    def body(state):
        kb, _ = state
        k0 = pl.multiple_of(kb * bk, bk)
        k = k_ref[pl.ds(k0, bk), :]
        v = v_ref[pl.ds(k0, bk), :]
        z = _dot_nt(q, k) * scale
        ls = _log_sigmoid(z)
        past = (srel + k0) < tpos
        lk = jnp.where(past, ls - z, 0.0)
        hi, lo = _split_bf16(lk)
        carry = carry_scr[...]
        r = _dot(hi, later) + _dot(lo, later) + carry
        att = jnp.where(past, jnp.exp(ls + r), 0.0)
        acc_scr[...] += _dot(att.astype(BF16), v)
        carry = carry + jnp.sum(lk, axis=-1, keepdims=True)
        carry_scr[...] = carry
        return kb - 1, jnp.max(carry)

    kb_start = (q0 + bq - 1) // bk
    lax.while_loop(cond, body, (kb_start, jnp.float32(0.0)))
    o_ref[...] = acc_scr[...].astype(o_ref.dtype)


def _stick_breaking(q, k, v):
    t = q.shape[0]
    d = B_HEAD_DIM
    return pl.pallas_call(
        _sb_kernel,
        out_shape=jax.ShapeDtypeStruct((t, B_HEADS * d), BF16),
        grid=(B_HEADS, t // SB_BQ),
        in_specs=[
            pl.BlockSpec((SB_BQ, d), lambda h, i: (i, h)),
            pl.BlockSpec((t, d), lambda h, i: (0, h)),
            pl.BlockSpec((t, d), lambda h, i: (0, h)),
        ],
        out_specs=pl.BlockSpec((SB_BQ, d), lambda h, i: (i, h)),
        scratch_shapes=[pltpu.VMEM((SB_BQ, d), F32), pltpu.VMEM((SB_BQ, 1), F32)],
        compiler_params=_cparams(("parallel", "arbitrary")),
        name="stick_breaking",
    )(q, k, v)


def _mix_kernel(g1_ref, g2_ref, ya_ref, wa_ref, yb_ref, wb_ref, o_ref):
    pa = _dot(ya_ref[...], wa_ref[...])
    pb = _dot(yb_ref[...], wb_ref[...])
    o_ref[...] = (g1_ref[...].astype(F32) * pa + g2_ref[...].astype(F32) * pb).astype(o_ref.dtype)


def _mix(gates, y_a, w_a, y_b, w_b, *, tm, tn):
    t = y_a.shape[0]
    d = w_a.shape[1]
    nt = d // tn
    return pl.pallas_call(
        _mix_kernel,
        out_shape=jax.ShapeDtypeStruct((t, d), BF16),
        grid=(t // tm, nt),
        in_specs=[
            pl.BlockSpec((tm, tn), lambda i, j: (i, j)),
            pl.BlockSpec((tm, tn), lambda i, j: (i, j + nt)),
            pl.BlockSpec((tm, y_a.shape[1]), lambda i, j: (i, 0)),
            pl.BlockSpec((w_a.shape[0], tn), lambda i, j: (0, j)),
            pl.BlockSpec((tm, y_b.shape[1]), lambda i, j: (i, 0)),
            pl.BlockSpec((w_b.shape[0], tn), lambda i, j: (0, j)),
        ],
        out_specs=pl.BlockSpec((tm, tn), lambda i, j: (i, j)),
        compiler_params=_cparams(("parallel", "arbitrary")),
        name="mix",
    )(gates, gates, y_a, w_a, y_b, w_b)


def _out_kernel(x_ref, mix_ref, wo_ref, g_ref, wrh_ref, wrl_ref, br_ref,
                x1_ref, h2_ref, ri_ref, rg_ref, cnt_ref):
    tm = x_ref.shape[0]
    x1 = x_ref[...] + _dot(mix_ref[...], wo_ref[...])
    x1_ref[...] = x1
    ms = jnp.mean(x1 * x1, axis=-1, keepdims=True)
    h2 = x1 * lax.rsqrt(ms + RMS_EPS) * g_ref[...]
    h_hi, h_lo = _split_bf16(h2)
    h2_ref[...] = h_hi
    w_hi = wrh_ref[...]
    lg = (_dot_nt(w_hi, h_hi) + _dot_nt(w_hi, h_lo) + _dot_nt(wrl_ref[...], h_hi)
          + br_ref[:, 0:1])

    @pl.when(pl.program_id(0) == 0)
    def _():
        cnt_ref[...] = jnp.zeros_like(cnt_ref)

    row = lax.broadcasted_iota(jnp.int32, (N_EXPERTS, tm), 0)
    vals, idxs, hots = [], [], []
    for _ in range(TOP_K):
        mx = jnp.max(lg, axis=0, keepdims=True)
        idx = jnp.min(jnp.where(lg == mx, row, N_EXPERTS), axis=0, keepdims=True)
        hot = row == idx
        vals.append(mx)
        idxs.append(idx)
        hots.append(hot)
        lg = jnp.where(hot, -jnp.inf, lg)

    exps = [jnp.exp(v - vals[0]) for v in vals]
    denom = exps[0]
    for e in exps[1:]:
        denom = denom + e

    chosen = hots[0]
    for hot in hots[1:]:
        chosen = jnp.logical_or(chosen, hot)
    chosen_f = jnp.where(chosen, 1.0, 0.0)
    r = lax.broadcasted_iota(jnp.int32, (tm, tm), 0)
    c = lax.broadcasted_iota(jnp.int32, (tm, tm), 1)
    earlier = jnp.where(r < c, 1.0, 0.0).astype(BF16)
    before = _dot(chosen_f.astype(BF16), earlier) + cnt_ref[:, 0:1]
    cnt_ref[...] += jnp.sum(chosen_f, axis=1, keepdims=True)

    row8 = lax.broadcasted_iota(jnp.int32, (2 * TOP_K, tm), 0)
    ri = jnp.zeros((2 * TOP_K, tm), jnp.int32)
    rg = jnp.zeros((2 * TOP_K, tm), F32)
    for k in range(TOP_K):
        rank = jnp.sum(jnp.where(hots[k], before, 0.0), axis=0, keepdims=True).astype(jnp.int32)
        ri = jnp.where(row8 == k, idxs[k], ri)
        ri = jnp.where(row8 == TOP_K + k, rank, ri)
        rg = jnp.where(row8 == k, exps[k] / denom, rg)
    ri_ref[...] = ri
    rg_ref[...] = rg


def _out_proj(x, mixed, w_o, g, wr_hi, wr_lo, b_r, *, tm):
    t, d = x.shape
    ne = wr_hi.shape[0]
    return pl.pallas_call(
        _out_kernel,
        out_shape=(jax.ShapeDtypeStruct((t, d), F32),
                   jax.ShapeDtypeStruct((t, d), BF16),
                   jax.ShapeDtypeStruct((2 * TOP_K, t), jnp.int32),
                   jax.ShapeDtypeStruct((2 * TOP_K, t), F32),
                   jax.ShapeDtypeStruct((ne, LANE), F32)),
        grid=(t // tm,),
        in_specs=[
            pl.BlockSpec((tm, d), lambda i: (i, 0)),
            pl.BlockSpec((tm, d), lambda i: (i, 0)),
            pl.BlockSpec((d, d), lambda i: (0, 0)),
            pl.BlockSpec((1, d), lambda i: (0, 0)),
            pl.BlockSpec((ne, d), lambda i: (0, 0)),
            pl.BlockSpec((ne, d), lambda i: (0, 0)),
            pl.BlockSpec((ne, LANE), lambda i: (0, 0)),
        ],
        out_specs=(pl.BlockSpec((tm, d), lambda i: (i, 0)),
                   pl.BlockSpec((tm, d), lambda i: (i, 0)),
                   pl.BlockSpec((2 * TOP_K, tm), lambda i: (0, i)),
                   pl.BlockSpec((2 * TOP_K, tm), lambda i: (0, i)),
                   pl.BlockSpec((ne, LANE), lambda i: (0, 0))),
        compiler_params=_cparams(("arbitrary",)),
        name="out_proj",
    )(x, mixed, w_o, g, wr_hi, wr_lo, b_r)


def _expert_changed(be_ref, m):
    return jnp.logical_or(m == 0, be_ref[m] != be_ref[jnp.maximum(m - 1, 0)])


def _moe_up_kernel(be_ref, nu_ref, x_ref, wg_ref, wu_ref, bg_ref, bu_ref, o_ref, wg_scr, wu_scr):
    m = pl.program_id(1)

    @pl.when(m < nu_ref[0])
    def _():
        @pl.when(_expert_changed(be_ref, m))
        def _():
            wg_scr[...] = wg_ref[0].astype(BF16)
            wu_scr[...] = wu_ref[0].astype(BF16)

        x = x_ref[...]
        gate = jnp.minimum(_dot(x, wg_scr[...]) + bg_ref[0], SWIGLU_LIMIT)
        up = jnp.clip(_dot(x, wu_scr[...]) + bu_ref[0], -SWIGLU_LIMIT, SWIGLU_LIMIT)
        o_ref[...] = ((up + 1.0) * gate * _sigmoid(SWIGLU_ALPHA * gate)).astype(o_ref.dtype)


def _moe_up(block_expert, n_used, x_pad, w_gu, b_gu, *, tf):
    rows, d = x_pad.shape
    nb = rows // MOE_TM
    d_ff = w_gu.shape[2] // 2
    nf = d_ff // tf

    def blk(m, nu):
        return jnp.maximum(jnp.minimum(m, nu[0] - 1), 0)

    return pl.pallas_call(
        _moe_up_kernel,
        out_shape=jax.ShapeDtypeStruct((rows, d_ff), BF16),
        grid_spec=pltpu.PrefetchScalarGridSpec(
            num_scalar_prefetch=2,
            grid=(nf, nb),
            in_specs=[
                pl.BlockSpec((MOE_TM, d), lambda n, m, be, nu: (blk(m, nu), 0)),
                pl.BlockSpec((1, d, tf), lambda n, m, be, nu: (be[blk(m, nu)], 0, n)),
                pl.BlockSpec((1, d, tf), lambda n, m, be, nu: (be[blk(m, nu)], 0, n + nf)),
                pl.BlockSpec((1, 1, tf), lambda n, m, be, nu: (be[blk(m, nu)], 0, n)),
                pl.BlockSpec((1, 1, tf), lambda n, m, be, nu: (be[blk(m, nu)], 0, n + nf)),
            ],
            out_specs=pl.BlockSpec((MOE_TM, tf), lambda n, m, be, nu: (m, n)),
            scratch_shapes=[pltpu.VMEM((d, tf), BF16), pltpu.VMEM((d, tf), BF16)],
        ),
        compiler_params=_cparams(("arbitrary", "arbitrary")),
        name="moe_up",
    )(block_expert, n_used, x_pad, w_gu, w_gu, b_gu, b_gu)


def _moe_down_kernel(be_ref, nu_ref, a_ref, wd_ref, bd_ref, o_ref, wd_scr):
    m = pl.program_id(1)

    @pl.when(m < nu_ref[0])
    def _():
        @pl.when(_expert_changed(be_ref, m))
        def _():
            wd_scr[...] = wd_ref[0].astype(BF16)

        o_ref[...] = (_dot(a_ref[...], wd_scr[...]) + bd_ref[0]).astype(o_ref.dtype)


def _moe_down(block_expert, n_used, act, w_d, b_d, *, tn):
    rows, d_ff = act.shape
    nb = rows // MOE_TM
    d = w_d.shape[2]

    def blk(m, nu):
        return jnp.maximum(jnp.minimum(m, nu[0] - 1), 0)

    return pl.pallas_call(
        _moe_down_kernel,
        out_shape=jax.ShapeDtypeStruct((rows, d), BF16),
        grid_spec=pltpu.PrefetchScalarGridSpec(
            num_scalar_prefetch=2,
            grid=(d // tn, nb),
            in_specs=[
                pl.BlockSpec((MOE_TM, d_ff), lambda n, m, be, nu: (blk(m, nu), 0)),
                pl.BlockSpec((1, d_ff, tn), lambda n, m, be, nu: (be[blk(m, nu)], 0, n)),
                pl.BlockSpec((1, 1, tn), lambda n, m, be, nu: (be[blk(m, nu)], 0, n)),
            ],
            out_specs=pl.BlockSpec((MOE_TM, tn), lambda n, m, be, nu: (m, n)),
            scratch_shapes=[pltpu.VMEM((d_ff, tn), BF16)],
        ),
        compiler_params=_cparams(("arbitrary", "arbitrary")),
        name="moe_down",
    )(block_expert, n_used, act, w_d, b_d)


def _combine_kernel(x_ref, y0_ref, y1_ref, y2_ref, y3_ref, rg_ref, g_ref, o_ref, *, normalize):
    x = x_ref[...]
    rg = rg_ref[...]
    for k, y_ref in enumerate((y0_ref, y1_ref, y2_ref, y3_ref)):
        x = x + rg[:, k:k + 1] * y_ref[...].astype(F32)
    if normalize:
        ms = jnp.mean(x * x, axis=-1, keepdims=True)
        x = x * lax.rsqrt(ms + RMS_EPS) * g_ref[...]
    o_ref[...] = x


def _combine(x1, ys, rg, g, *, normalize, tm):
    t, d = x1.shape
    row_spec = pl.BlockSpec((tm, d), lambda i: (i, 0))
    return pl.pallas_call(
        functools.partial(_combine_kernel, normalize=normalize),
        out_shape=jax.ShapeDtypeStruct((t, d), F32),
        grid=(t // tm,),
        in_specs=[row_spec] * (1 + TOP_K) + [
            pl.BlockSpec((tm, rg.shape[1]), lambda i: (i, 0)),
            pl.BlockSpec((1, d), lambda i: (0, 0)),
        ],
        out_specs=row_spec,
        compiler_params=_cparams(("parallel",)),
        name="combine_norm",
    )(x1, *ys, rg, g)


def _route(ri, counts, t):
    p_n = t * TOP_K
    experts = ri[:TOP_K]
    rank = ri[TOP_K:]
    counts = counts[:, 0].astype(jnp.int32)
    padded = (counts + MOE_TM - 1) // MOE_TM * MOE_TM
    pad_end = jnp.cumsum(padded)
    pad_start = pad_end - padded
    dest = pad_start[experts] + rank
    nb = -(-p_n // MOE_TM) + N_EXPERTS
    src_tok = (jnp.arange(nb * MOE_TM, dtype=jnp.int32) % t).at[dest.reshape(p_n)].set(
        jnp.tile(jnp.arange(t, dtype=jnp.int32), TOP_K))
    block_start = jnp.arange(nb, dtype=jnp.int32) * MOE_TM
    block_expert = jnp.minimum(
        jnp.sum((block_start[:, None] >= pad_end[None, :]).astype(jnp.int32), axis=1),
        N_EXPERTS - 1)
    n_used = (pad_end[-1] // MOE_TM).astype(jnp.int32).reshape(1)
    return dest, src_tok, block_expert, n_used


def kernel(x, norm_mix, w_in, conv_w, conv_b, b_gates_if, norm_head, w_proj_a, w_proj_b,
           w_merge_gate, b_merge_gate, w_out, norm_ffn, w_router, b_router, w_gate_up,
           b_gate_up, w_down, b_down, norm_final):
    bn, s, d = x.shape
    assert bn == 1
    depth = norm_mix.shape[0]
    xt = x.reshape(s, d)
    tm = min(1024, s)
    aqk = A_HEADS * A_QK
    av = A_HEADS * A_V
    bw = B_HEADS * B_HEAD_DIM

    for l in range(depth):
        wl = w_in[l]
        c0 = 2 * aqk
        c1 = c0 + 2 * av
        c2 = c1 + 2 * A_HEADS
        w_f32 = jnp.concatenate(
            [wl[:, :c0], wl[:, c1:c2], jnp.zeros((d, LANE - 2 * A_HEADS), F32)], axis=1).astype(BF16)
        b_f32 = jnp.concatenate(
            [jnp.zeros((c0,), F32), b_gates_if[l], jnp.zeros((LANE - 2 * A_HEADS,), F32)])[None, :]
        w_bf = jnp.concatenate([wl[:, c0:c1], wl[:, c2:]], axis=1).astype(BF16)
        g_mix = norm_mix[l][None, :]

        p_f32 = _norm_proj(xt, g_mix, w_f32, b_f32, out_dtype=F32, sigmoid=False,
                           tm=tm, tn=w_f32.shape[1])
        p_bf = _norm_proj(xt, g_mix, w_bf, jnp.zeros((1, w_bf.shape[1]), F32),
                          out_dtype=BF16, sigmoid=False, tm=tm, tn=1024)
        gates = _norm_proj(xt, g_mix, w_merge_gate[l].astype(BF16), b_merge_gate[l][None, :],
                           out_dtype=BF16, sigmoid=True, tm=tm, tn=1024)

        qk_raw = p_f32[:, :c0]
        if_col = p_f32[:, c0:]
        if_row = if_col[:, :SUBLANE].T
        a_v = p_bf[:, :av]
        a_o = p_bf[:, av:2 * av]
        b_q = p_bf[:, 2 * av:2 * av + bw]
        b_k = p_bf[:, 2 * av + bw:2 * av + 2 * bw]
        b_v = p_bf[:, 2 * av + 2 * bw:]

        y_a = _mlstm(qk_raw, a_v, a_o, if_col, if_row, conv_w[l], conv_b[l][None, :],
                     norm_head[l][None, :])
        y_b = _stick_breaking(b_q, b_k, b_v)

        mixed = _mix(gates, y_a, w_proj_a[l].astype(BF16), y_b, w_proj_b[l].astype(BF16),
                     tm=tm, tn=512)
        wr_hi, wr_lo = _split_bf16(w_router[l].T)
        b_r = jnp.broadcast_to(b_router[l][:, None], (N_EXPERTS, LANE))
        x1, h2, ri, rg, counts = _out_proj(xt, mixed, w_out[l].astype(BF16), norm_ffn[l][None, :],
                                           wr_hi, wr_lo, b_r, tm=min(512, s))

        dest, src_tok, block_expert, n_used = _route(ri, counts, s)
        x_pad = jnp.take(h2, src_tok, axis=0)
        act = _moe_up(block_expert, n_used, x_pad, w_gate_up[l], b_gate_up[l][:, None, :], tf=1024)
        y_pad = _moe_down(block_expert, n_used, act, w_down[l], b_down[l][:, None, :], tn=2048)
        ys = [jnp.take(y_pad, dest[k], axis=0) for k in range(TOP_K)]
        xt = _combine(x1, ys, rg.T, norm_final[None, :], normalize=(l + 1 == depth), tm=min(512, s))
    return xt.reshape(bn, s, d)
```

```python
import functools

import jax
import jax.numpy as jnp
from jax import lax
from jax.experimental import pallas as pl
from jax.experimental.pallas import tpu as pltpu

F32 = jnp.float32
BF16 = jnp.bfloat16

RMS_EPS = 1e-5
A_HEADS = 4
A_QK = 128
A_V = 256
CONV_W = 4
B_HEADS = 8
B_HEAD_DIM = 128
N_EXPERTS = 32
TOP_K = 4
SWIGLU_LIMIT = 7.0
SWIGLU_ALPHA = 1.702

LANE = 128
SUBLANE = 8
VMEM_LIMIT = 60 * 1024 * 1024

MLSTM_CHUNK = 128
SB_BQ = 512
SB_SUB = 128
SB_BAND = 256
SB_SKIP = 88.0
MOE_TM = 512


def _cparams(sem):
    return pltpu.CompilerParams(dimension_semantics=sem, vmem_limit_bytes=VMEM_LIMIT)


def _log_sigmoid(z):
    return jnp.minimum(z, 0.0) - jnp.log1p(jnp.exp(-jnp.abs(z)))


def _sigmoid(z):
    return 1.0 / (1.0 + jnp.exp(-z))


def _split_bf16(x):
    hi = x.astype(BF16)
    lo = (x - hi.astype(F32)).astype(BF16)
    return hi, lo


def _dot(a, b):
    return jnp.dot(a, b, preferred_element_type=F32)


def _dot_nt(a, b):
    return lax.dot_general(a, b, (((1,), (1,)), ((), ())), preferred_element_type=F32)


def _dot_tn(a, b):
    return lax.dot_general(a, b, (((0,), (0,)), ((), ())), preferred_element_type=F32)


def _norm_proj_kernel(x_ref, g_ref, w_ref, b_ref, o_ref, h_scr, *, sigmoid):
    @pl.when(pl.program_id(1) == 0)
    def _():
        x = x_ref[...]
        ms = jnp.mean(x * x, axis=-1, keepdims=True)
        h_scr[...] = (x * lax.rsqrt(ms + RMS_EPS) * g_ref[...]).astype(BF16)

    acc = _dot(h_scr[...], w_ref[...]) + b_ref[...]
    if sigmoid:
        acc = _sigmoid(acc)
    o_ref[...] = acc.astype(o_ref.dtype)


def _norm_proj(x, g, w, b, *, out_dtype, sigmoid, tm, tn):
    t, d = x.shape
    n = w.shape[1]
    return pl.pallas_call(
        functools.partial(_norm_proj_kernel, sigmoid=sigmoid),
        out_shape=jax.ShapeDtypeStruct((t, n), out_dtype),
        grid=(t // tm, n // tn),
        in_specs=[
            pl.BlockSpec((tm, d), lambda i, j: (i, 0)),
            pl.BlockSpec((1, d), lambda i, j: (0, 0)),
            pl.BlockSpec((d, tn), lambda i, j: (0, j)),
            pl.BlockSpec((1, tn), lambda i, j: (0, j)),
        ],
        out_specs=pl.BlockSpec((tm, tn), lambda i, j: (i, j)),
        scratch_shapes=[pltpu.VMEM((tm, d), BF16)],
        compiler_params=_cparams(("parallel", "arbitrary")),
        name="norm_proj",
    )(x, g, w, b)


def _mlstm_kernel(qk_ref, v_ref, o_ref, ifc_ref, ifr_ref, cw_ref, cb_ref, nh_ref,
                  y_ref, ext_scr, c_scr, n_scr, m_scr):
    L = MLSTM_CHUNK
    step = pl.program_id(0)

    @pl.when(step == 0)
    def _():
        ext_scr[0:SUBLANE, :] = jnp.zeros((SUBLANE, ext_scr.shape[1]), F32)
        c_scr[...] = jnp.zeros_like(c_scr)
        n_scr[...] = jnp.zeros_like(n_scr)
        m_scr[...] = jnp.zeros_like(m_scr)

    raw = qk_ref[...]
    ext_scr[SUBLANE:SUBLANE + L, :] = raw
    conv = cb_ref[...]
    for j in range(CONV_W):
        off = SUBLANE - (CONV_W - 1) + j
        conv = conv + ext_scr[off:off + L, :] * cw_ref[j:j + 1, :]
    ext_scr[0:SUBLANE, :] = raw[L - SUBLANE:L, :]
    qk = conv * _sigmoid(conv)

    row = lax.broadcasted_iota(jnp.int32, (L, L), 0)
    col = lax.broadcasted_iota(jnp.int32, (L, L), 1)
    causal = row >= col
    tri_incl = jnp.where(causal, 1.0, 0.0).astype(BF16)
    tri_incl_t = jnp.where(col >= row, 1.0, 0.0).astype(BF16)

    for h in range(A_HEADS):
        q = qk[:, h * A_QK:(h + 1) * A_QK]
        k = qk[:, A_HEADS * A_QK + h * A_QK:A_HEADS * A_QK + (h + 1) * A_QK] * (A_QK ** -0.5)
        v = v_ref[:, h * A_V:(h + 1) * A_V]
        q_bf = q.astype(BF16)
        k_bf = k.astype(BF16)

        i_col = ifc_ref[:, h:h + 1]
        f_col = ifc_ref[:, A_HEADS + h:A_HEADS + h + 1]
        i_row = ifr_ref[h:h + 1, :]
        f_row = ifr_ref[A_HEADS + h:A_HEADS + h + 1, :]
        lf_col = jnp.broadcast_to(_log_sigmoid(f_col), (L, L))
        lf_row = jnp.broadcast_to(_log_sigmoid(f_row), (L, L))
        hi, lo = _split_bf16(lf_col)
        a_colb = _dot(tri_incl, hi) + _dot(tri_incl, lo)
        hi, lo = _split_bf16(lf_row)
        a_rowb = _dot(hi, tri_incl_t) + _dot(lo, tri_incl_t)
        a_col = a_colb[:, 0:1]
        g_tot = a_colb[L - 1:L, 0:1]

        m_prev = m_scr[h]
        n_prev = n_scr[h]
        ct_prev = c_scr[h]

        m_inter = a_col + m_prev
        d_log = jnp.where(causal, a_colb - a_rowb + i_row, -jnp.inf)
        m_t = jnp.maximum(m_inter, jnp.max(d_log, axis=-1, keepdims=True))
        p = jnp.exp(d_log - m_t) * _dot_nt(q_bf, k_bf)
        s_inter = jnp.exp(m_inter - m_t)
        num = s_inter * _dot(q_bf, ct_prev.astype(BF16)) + _dot(p.astype(BF16), v)
        den = (s_inter * jnp.sum(q * n_prev, axis=-1, keepdims=True)
               + jnp.sum(p, axis=-1, keepdims=True))
        hh = num / jnp.maximum(jnp.abs(den), jnp.exp(-m_t))

        hh = hh * lax.rsqrt(jnp.mean(hh * hh, axis=-1, keepdims=True) + RMS_EPS)
        gate = _sigmoid(o_ref[:, h * A_V:(h + 1) * A_V].astype(F32))
        y_ref[:, h * A_V:(h + 1) * A_V] = (
            hh * nh_ref[:, h * A_V:(h + 1) * A_V] * gate).astype(y_ref.dtype)

        w_col = g_tot - a_col + i_col
        m_loc = jnp.max(w_col, axis=0, keepdims=True)
        ke = k * jnp.exp(w_col - m_loc)
        ct_loc = _dot_tn(ke.astype(BF16), v)
        n_loc = jnp.sum(ke, axis=0, keepdims=True)
        m_new = jnp.maximum(g_tot + m_prev, m_loc)
        s_old = jnp.exp(g_tot + m_prev - m_new)
        s_new = jnp.exp(m_loc - m_new)
        c_scr[h] = s_old * ct_prev + s_new * ct_loc
        n_scr[h] = s_old * n_prev + s_new * n_loc
        m_scr[h] = m_new


def _mlstm(p_f32, p_bf, if_row, conv_w, conv_b, norm_head):
    t = p_f32.shape[0]
    L = MLSTM_CHUNK
    wq = conv_w.shape[1]
    wv = norm_head.shape[1]
    return pl.pallas_call(
        _mlstm_kernel,
        out_shape=jax.ShapeDtypeStruct((t, wv), BF16),
        grid=(t // L,),
        in_specs=[
            pl.BlockSpec((L, wq), lambda i: (i, 0)),
            pl.BlockSpec((L, wv), lambda i: (i, 0)),
            pl.BlockSpec((L, wv), lambda i: (i, 1)),
            pl.BlockSpec((L, LANE), lambda i: (i, wq // LANE)),
            pl.BlockSpec((SUBLANE, L), lambda i: (0, i)),
            pl.BlockSpec((CONV_W, wq), lambda i: (0, 0)),
            pl.BlockSpec((1, wq), lambda i: (0, 0)),
            pl.BlockSpec((1, wv), lambda i: (0, 0)),
        ],
        out_specs=pl.BlockSpec((L, wv), lambda i: (i, 0)),
        scratch_shapes=[
            pltpu.VMEM((L + SUBLANE, wq), F32),
            pltpu.VMEM((A_HEADS, A_QK, A_V), F32),
            pltpu.VMEM((A_HEADS, 1, A_QK), F32),
            pltpu.VMEM((A_HEADS, 1, 1), F32),
        ],
        compiler_params=_cparams(("arbitrary",)),
        name="mlstm",
    )(p_f32, p_bf, p_bf, p_f32, if_row, conv_w, conv_b, norm_head)


def _later(n):
    j = lax.broadcasted_iota(jnp.int32, (n, n), 0)
    s = lax.broadcasted_iota(jnp.int32, (n, n), 1)
    return jnp.where(j > s, 1.0, 0.0).astype(BF16)


def _sb_tile(q, k, v, carry, past, later):
    z = _dot_nt(q, k)
    ls = jnp.minimum(z, 0.0) - jnp.log(1.0 + jnp.exp(-jnp.abs(z)))
    lk = ls - z
    if past is not None:
        lk = jnp.where(past, lk, 0.0)
    hi, lo = _split_bf16(lk)
    r_in = _dot(hi, later) + _dot(lo, later)
    w = jnp.exp(ls + r_in + carry)
    if past is not None:
        w = jnp.where(past, w, 0.0)
    return _dot(w.astype(BF16), v), carry + r_in[:, 0:1] + lk[:, 0:1]


def _sb_kernel(q_ref, k_ref, v_ref, o_ref, acc_scr, carry_scr):
    sub, band = SB_SUB, SB_BAND
    nsub = q_ref.shape[0] // sub
    q0 = pl.program_id(1) * q_ref.shape[0]
    later_band = _later(band)
    row = lax.broadcasted_iota(jnp.int32, (sub, band), 0)
    col = lax.broadcasted_iota(jnp.int32, (sub, band), 1)

    band_start, band_max = [], []
    for j in range(nsub):
        t0 = q0 + j * sub
        k0 = pl.multiple_of(jnp.maximum(t0 + sub - band, 0), sub)
        past = (col + k0) < (row + t0)
        acc, carry = _sb_tile(q_ref[j * sub:(j + 1) * sub, :], k_ref[pl.ds(k0, band), :],
                              v_ref[pl.ds(k0, band), :], 0.0, past, later_band)
        acc_scr[j] = acc
        carry_scr[j] = carry
        band_start.append(k0)
        band_max.append(jnp.max(carry))

    later_blk = _later(sub)
    for j in range(nsub):
        def cond(state):
            kb, mx = state
            return jnp.logical_and(kb >= 0, mx > -SB_SKIP)

        def body(state, j=j):
            kb, _ = state
            kk0 = pl.multiple_of(kb * sub, sub)
            acc, carry = _sb_tile(q_ref[j * sub:(j + 1) * sub, :], k_ref[pl.ds(kk0, sub), :],
                                  v_ref[pl.ds(kk0, sub), :], carry_scr[j], None, later_blk)
            acc_scr[j] += acc
            carry_scr[j] = carry
            return kb - 1, jnp.max(carry)

        lax.while_loop(cond, body, (band_start[j] // sub - 1, band_max[j]))
        o_ref[j * sub:(j + 1) * sub, :] = acc_scr[j].astype(o_ref.dtype)


def _stick_breaking(qkv, q_col, k_col, v_col):
    t = qkv.shape[0]
    d = B_HEAD_DIM
    bq = min(SB_BQ, t)
    assert t >= SB_BAND and t % bq == 0
    return pl.pallas_call(
        _sb_kernel,
        out_shape=jax.ShapeDtypeStruct((t, B_HEADS * d), BF16),
        grid=(B_HEADS, t // bq),
        in_specs=[
            pl.BlockSpec((bq, d), lambda h, i: (i, q_col + h)),
            pl.BlockSpec((t, d), lambda h, i: (0, k_col + h)),
            pl.BlockSpec((t, d), lambda h, i: (0, v_col + h)),
        ],
        out_specs=pl.BlockSpec((bq, d), lambda h, i: (i, h)),
        scratch_shapes=[pltpu.VMEM((bq // SB_SUB, SB_SUB, d), F32),
                        pltpu.VMEM((bq // SB_SUB, SB_SUB, 1), F32)],
        compiler_params=_cparams(("parallel", "arbitrary")),
        name="stick_breaking",
    )(qkv, qkv, qkv)


def _mix_kernel(g1_ref, g2_ref, ya_ref, wa_ref, yb_ref, wb_ref, o_ref):
    pa = _dot(ya_ref[...], wa_ref[...])
    pb = _dot(yb_ref[...], wb_ref[...])
    o_ref[...] = (g1_ref[...].astype(F32) * pa + g2_ref[...].astype(F32) * pb).astype(o_ref.dtype)


def _mix(gates, y_a, w_a, y_b, w_b, *, tm, tn):
    t = y_a.shape[0]
    d = w_a.shape[1]
    nt = d // tn
    return pl.pallas_call(
        _mix_kernel,
        out_shape=jax.ShapeDtypeStruct((t, d), BF16),
        grid=(t // tm, nt),
        in_specs=[
            pl.BlockSpec((tm, tn), lambda i, j: (i, j)),
            pl.BlockSpec((tm, tn), lambda i, j: (i, j + nt)),
            pl.BlockSpec((tm, y_a.shape[1]), lambda i, j: (i, 0)),
            pl.BlockSpec((w_a.shape[0], tn), lambda i, j: (0, j)),
            pl.BlockSpec((tm, y_b.shape[1]), lambda i, j: (i, 0)),
            pl.BlockSpec((w_b.shape[0], tn), lambda i, j: (0, j)),
        ],
        out_specs=pl.BlockSpec((tm, tn), lambda i, j: (i, j)),
        compiler_params=_cparams(("parallel", "arbitrary")),
        name="mix",
    )(gates, gates, y_a, w_a, y_b, w_b)


def _out_kernel(x_ref, mix_ref, wo_ref, g_ref, wrh_ref, wrl_ref, br_ref,
                x1_ref, h2_ref, ri_ref, rg_ref, cnt_ref):
    tm = x_ref.shape[0]
    x1 = x_ref[...] + _dot(mix_ref[...], wo_ref[...])
    x1_ref[...] = x1
    ms = jnp.mean(x1 * x1, axis=-1, keepdims=True)
    h2 = x1 * lax.rsqrt(ms + RMS_EPS) * g_ref[...]
    h_hi, h_lo = _split_bf16(h2)
    h2_ref[...] = h_hi
    w_hi = wrh_ref[...]
    lg = (_dot_nt(w_hi, h_hi) + _dot_nt(w_hi, h_lo) + _dot_nt(wrl_ref[...], h_hi)
          + br_ref[:, 0:1])

    @pl.when(pl.program_id(0) == 0)
    def _():
        cnt_ref[...] = jnp.zeros_like(cnt_ref)

    row = lax.broadcasted_iota(jnp.int32, (N_EXPERTS, tm), 0)
    vals, idxs, hots = [], [], []
    for _ in range(TOP_K):
        mx = jnp.max(lg, axis=0, keepdims=True)
        idx = jnp.min(jnp.where(lg == mx, row, N_EXPERTS), axis=0, keepdims=True)
        hot = row == idx
        vals.append(mx)
        idxs.append(idx)
        hots.append(hot)
        lg = jnp.where(hot, -jnp.inf, lg)

    exps = [jnp.exp(v - vals[0]) for v in vals]
    denom = exps[0]
    for e in exps[1:]:
        denom = denom + e

    chosen = hots[0]
    for hot in hots[1:]:
        chosen = jnp.logical_or(chosen, hot)
    chosen_f = jnp.where(chosen, 1.0, 0.0)
    r = lax.broadcasted_iota(jnp.int32, (tm, tm), 0)
    c = lax.broadcasted_iota(jnp.int32, (tm, tm), 1)
    earlier = jnp.where(r < c, 1.0, 0.0).astype(BF16)
    before = _dot(chosen_f.astype(BF16), earlier) + cnt_ref[:, 0:1]
    cnt_ref[...] += jnp.sum(chosen_f, axis=1, keepdims=True)

    row8 = lax.broadcasted_iota(jnp.int32, (2 * TOP_K, tm), 0)
    ri = jnp.zeros((2 * TOP_K, tm), jnp.int32)
    rg = jnp.zeros((2 * TOP_K, tm), F32)
    for k in range(TOP_K):
        rank = jnp.sum(jnp.where(hots[k], before, 0.0), axis=0, keepdims=True).astype(jnp.int32)
        ri = jnp.where(row8 == k, idxs[k], ri)
        ri = jnp.where(row8 == TOP_K + k, rank, ri)
        rg = jnp.where(row8 == k, exps[k] / denom, rg)
    ri_ref[...] = ri
    rg_ref[...] = rg


def _out_proj(x, mixed, w_o, g, wr_hi, wr_lo, b_r, *, tm):
    t, d = x.shape
    ne = wr_hi.shape[0]
    return pl.pallas_call(
        _out_kernel,
        out_shape=(jax.ShapeDtypeStruct((t, d), F32),
                   jax.ShapeDtypeStruct((t, d), BF16),
                   jax.ShapeDtypeStruct((2 * TOP_K, t), jnp.int32),
                   jax.ShapeDtypeStruct((2 * TOP_K, t), F32),
                   jax.ShapeDtypeStruct((ne, LANE), F32)),
        grid=(t // tm,),
        in_specs=[
            pl.BlockSpec((tm, d), lambda i: (i, 0)),
            pl.BlockSpec((tm, d), lambda i: (i, 0)),
            pl.BlockSpec((d, d), lambda i: (0, 0)),
            pl.BlockSpec((1, d), lambda i: (0, 0)),
            pl.BlockSpec((ne, d), lambda i: (0, 0)),
            pl.BlockSpec((ne, d), lambda i: (0, 0)),
            pl.BlockSpec((ne, LANE), lambda i: (0, 0)),
        ],
        out_specs=(pl.BlockSpec((tm, d), lambda i: (i, 0)),
                   pl.BlockSpec((tm, d), lambda i: (i, 0)),
                   pl.BlockSpec((2 * TOP_K, tm), lambda i: (0, i)),
                   pl.BlockSpec((2 * TOP_K, tm), lambda i: (0, i)),
                   pl.BlockSpec((ne, LANE), lambda i: (0, 0))),
        compiler_params=_cparams(("arbitrary",)),
        name="out_proj",
    )(x, mixed, w_o, g, wr_hi, wr_lo, b_r)


def _expert_changed(be_ref, m):
    return jnp.logical_or(m == 0, be_ref[m] != be_ref[jnp.maximum(m - 1, 0)])


def _moe_up_kernel(be_ref, nu_ref, x_ref, wg_ref, wu_ref, bg_ref, bu_ref, o_ref, wg_scr, wu_scr):
    m = pl.program_id(1)

    @pl.when(m < nu_ref[0])
    def _():
        @pl.when(_expert_changed(be_ref, m))
        def _():
            wg_scr[...] = wg_ref[0].astype(BF16)
            wu_scr[...] = wu_ref[0].astype(BF16)

        x = x_ref[...]
        gate = jnp.minimum(_dot(x, wg_scr[...]) + bg_ref[0], SWIGLU_LIMIT)
        up = jnp.clip(_dot(x, wu_scr[...]) + bu_ref[0], -SWIGLU_LIMIT, SWIGLU_LIMIT)
        o_ref[...] = ((up + 1.0) * gate * _sigmoid(SWIGLU_ALPHA * gate)).astype(o_ref.dtype)


def _moe_up(block_expert, n_used, x_pad, w_gu, b_gu, *, tf):
    rows, d = x_pad.shape
    nb = rows // MOE_TM
    d_ff = w_gu.shape[2] // 2
    nf = d_ff // tf

    def blk(m, nu):
        return jnp.maximum(jnp.minimum(m, nu[0] - 1), 0)

    return pl.pallas_call(
        _moe_up_kernel,
        out_shape=jax.ShapeDtypeStruct((rows, d_ff), BF16),
        grid_spec=pltpu.PrefetchScalarGridSpec(
            num_scalar_prefetch=2,
            grid=(nf, nb),
            in_specs=[
                pl.BlockSpec((MOE_TM, d), lambda n, m, be, nu: (blk(m, nu), 0)),
                pl.BlockSpec((1, d, tf), lambda n, m, be, nu: (be[blk(m, nu)], 0, n)),
                pl.BlockSpec((1, d, tf), lambda n, m, be, nu: (be[blk(m, nu)], 0, n + nf)),
                pl.BlockSpec((1, 1, tf), lambda n, m, be, nu: (be[blk(m, nu)], 0, n)),
                pl.BlockSpec((1, 1, tf), lambda n, m, be, nu: (be[blk(m, nu)], 0, n + nf)),
            ],
            out_specs=pl.BlockSpec((MOE_TM, tf), lambda n, m, be, nu: (m, n)),
            scratch_shapes=[pltpu.VMEM((d, tf), BF16), pltpu.VMEM((d, tf), BF16)],
        ),
        compiler_params=_cparams(("arbitrary", "arbitrary")),
        name="moe_up",
    )(block_expert, n_used, x_pad, w_gu, w_gu, b_gu, b_gu)


def _moe_down_kernel(be_ref, nu_ref, a_ref, wd_ref, bd_ref, o_ref, wd_scr):
    m = pl.program_id(1)

    @pl.when(m < nu_ref[0])
    def _():
        @pl.when(_expert_changed(be_ref, m))
        def _():
            wd_scr[...] = wd_ref[0].astype(BF16)

        o_ref[...] = (_dot(a_ref[...], wd_scr[...]) + bd_ref[0]).astype(o_ref.dtype)


def _moe_down(block_expert, n_used, act, w_d, b_d, *, tn):
    rows, d_ff = act.shape
    nb = rows // MOE_TM
    d = w_d.shape[2]

    def blk(m, nu):
        return jnp.maximum(jnp.minimum(m, nu[0] - 1), 0)

    return pl.pallas_call(
        _moe_down_kernel,
        out_shape=jax.ShapeDtypeStruct((rows, d), BF16),
        grid_spec=pltpu.PrefetchScalarGridSpec(
            num_scalar_prefetch=2,
            grid=(d // tn, nb),
            in_specs=[
                pl.BlockSpec((MOE_TM, d_ff), lambda n, m, be, nu: (blk(m, nu), 0)),
                pl.BlockSpec((1, d_ff, tn), lambda n, m, be, nu: (be[blk(m, nu)], 0, n)),
                pl.BlockSpec((1, 1, tn), lambda n, m, be, nu: (be[blk(m, nu)], 0, n)),
            ],
            out_specs=pl.BlockSpec((MOE_TM, tn), lambda n, m, be, nu: (m, n)),
            scratch_shapes=[pltpu.VMEM((d_ff, tn), BF16)],
        ),
        compiler_params=_cparams(("arbitrary", "arbitrary")),
        name="moe_down",
    )(block_expert, n_used, act, w_d, b_d)


def _combine_kernel(x_ref, y0_ref, y1_ref, y2_ref, y3_ref, rg_ref, g_ref, o_ref, *, normalize):
    x = x_ref[...]
    rg = rg_ref[...]
    for k, y_ref in enumerate((y0_ref, y1_ref, y2_ref, y3_ref)):
        x = x + rg[:, k:k + 1] * y_ref[...].astype(F32)
    if normalize:
        ms = jnp.mean(x * x, axis=-1, keepdims=True)
        x = x * lax.rsqrt(ms + RMS_EPS) * g_ref[...]
    o_ref[...] = x


def _combine(x1, ys, rg, g, *, normalize, tm):
    t, d = x1.shape
    row_spec = pl.BlockSpec((tm, d), lambda i: (i, 0))
    return pl.pallas_call(
        functools.partial(_combine_kernel, normalize=normalize),
        out_shape=jax.ShapeDtypeStruct((t, d), F32),
        grid=(t // tm,),
        in_specs=[row_spec] * (1 + TOP_K) + [
            pl.BlockSpec((tm, rg.shape[1]), lambda i: (i, 0)),
            pl.BlockSpec((1, d), lambda i: (0, 0)),
        ],
        out_specs=row_spec,
        compiler_params=_cparams(("parallel",)),
        name="combine_norm",
    )(x1, *ys, rg, g)


def _route(ri, counts, t):
    p_n = t * TOP_K
    experts = ri[:TOP_K]
    rank = ri[TOP_K:]
    counts = counts[:, 0].astype(jnp.int32)
    padded = (counts + MOE_TM - 1) // MOE_TM * MOE_TM
    pad_end = jnp.cumsum(padded)
    pad_start = pad_end - padded
    e_ids = jnp.arange(N_EXPERTS, dtype=jnp.int32)[:, None, None]
    dest = rank + jnp.sum(jnp.where(experts[None] == e_ids, pad_start[:, None, None], 0), axis=0)
    nb = -(-p_n // MOE_TM) + N_EXPERTS
    src_tok = (jnp.arange(nb * MOE_TM, dtype=jnp.int32) % t).at[dest.reshape(p_n)].set(
        jnp.tile(jnp.arange(t, dtype=jnp.int32), TOP_K), mode="promise_in_bounds",
        unique_indices=True)
    block_start = jnp.arange(nb, dtype=jnp.int32) * MOE_TM
    block_expert = jnp.minimum(
        jnp.sum((block_start[:, None] >= pad_end[None, :]).astype(jnp.int32), axis=1),
        N_EXPERTS - 1)
    n_used = (pad_end[-1] // MOE_TM).astype(jnp.int32).reshape(1)
    return dest, src_tok, block_expert, n_used


def kernel(x, norm_mix, w_in, conv_w, conv_b, b_gates_if, norm_head, w_proj_a, w_proj_b,
           w_merge_gate, b_merge_gate, w_out, norm_ffn, w_router, b_router, w_gate_up,
           b_gate_up, w_down, b_down, norm_final):
    bn, s, d = x.shape
    assert bn == 1
    depth = norm_mix.shape[0]
    xt = x.reshape(s, d)
    tm = min(1024, s)
    aqk = A_HEADS * A_QK
    av = A_HEADS * A_V
    bw = B_HEADS * B_HEAD_DIM

    for l in range(depth):
        wl = w_in[l]
        c0 = 2 * aqk
        c1 = c0 + 2 * av
        c2 = c1 + 2 * A_HEADS
        w_f32 = jnp.concatenate(
            [wl[:, :c0], wl[:, c1:c2], jnp.zeros((d, LANE - 2 * A_HEADS), F32)], axis=1).astype(BF16)
        b_f32 = jnp.concatenate(
            [jnp.zeros((c0,), F32), b_gates_if[l], jnp.zeros((LANE - 2 * A_HEADS,), F32)])[None, :]
        w_bf = jnp.concatenate([wl[:, c0:c1], wl[:, c2:c2 + bw] * (B_HEAD_DIM ** -0.5), wl[:, c2 + bw:]],
                               axis=1).astype(BF16)
        g_mix = norm_mix[l][None, :]

        p_f32 = _norm_proj(xt, g_mix, w_f32, b_f32, out_dtype=F32, sigmoid=False,
                           tm=tm, tn=w_f32.shape[1])
        p_bf = _norm_proj(xt, g_mix, w_bf, jnp.zeros((1, w_bf.shape[1]), F32),
                          out_dtype=BF16, sigmoid=False, tm=tm, tn=1024)
        gates = _norm_proj(xt, g_mix, w_merge_gate[l].astype(BF16), b_merge_gate[l][None, :],
                           out_dtype=BF16, sigmoid=True, tm=tm, tn=1024)

        if_row = p_f32[:, c0:c0 + SUBLANE].T
        y_a = _mlstm(p_f32, p_bf, if_row, conv_w[l], conv_b[l][None, :], norm_head[l][None, :])
        qb = 2 * av // B_HEAD_DIM
        y_b = _stick_breaking(p_bf, qb, qb + B_HEADS, qb + 2 * B_HEADS)

        mixed = _mix(gates, y_a, w_proj_a[l].astype(BF16), y_b, w_proj_b[l].astype(BF16),
                     tm=tm, tn=512)
        wr_hi, wr_lo = _split_bf16(w_router[l].T)
        b_r = jnp.broadcast_to(b_router[l][:, None], (N_EXPERTS, LANE))
        x1, h2, ri, rg, counts = _out_proj(xt, mixed, w_out[l].astype(BF16), norm_ffn[l][None, :],
                                           wr_hi, wr_lo, b_r, tm=min(512, s))

        dest, src_tok, block_expert, n_used = _route(ri, counts, s)
        x_pad = h2.at[src_tok].get(mode="promise_in_bounds")
        act = _moe_up(block_expert, n_used, x_pad, w_gate_up[l], b_gate_up[l][:, None, :], tf=1024)
        y_pad = _moe_down(block_expert, n_used, act, w_down[l], b_down[l][:, None, :], tn=2048)
        ys = [y_pad.at[dest[k]].get(mode="promise_in_bounds") for k in range(TOP_K)]
        xt = _combine(x1, ys, rg.T, norm_final[None, :], normalize=(l + 1 == depth), tm=min(512, s))
    return xt.reshape(bn, s, d)
```

```python
import functools

import jax
import jax.numpy as jnp
from jax import lax
from jax.experimental import pallas as pl
from jax.experimental.pallas import tpu as pltpu

F32 = jnp.float32
BF16 = jnp.bfloat16

RMS_EPS = 1e-5
A_HEADS = 4
A_QK = 128
A_V = 256
CONV_W = 4
B_HEADS = 8
B_HEAD_DIM = 128
N_EXPERTS = 32
TOP_K = 4
SWIGLU_LIMIT = 7.0
SWIGLU_ALPHA = 1.702

LANE = 128
SUBLANE = 8
VMEM_LIMIT = 60 * 1024 * 1024

MLSTM_CHUNK = 128
SB_BQ = 1024
SB_SUB = 128
SB_SKIP = 88.0
MOE_TM = 512
MOE_CHUNKS = 2


def _cparams(sem):
    return pltpu.CompilerParams(dimension_semantics=sem, vmem_limit_bytes=VMEM_LIMIT)


def _log_sigmoid(z):
    return jnp.minimum(z, 0.0) - jnp.log1p(jnp.exp(-jnp.abs(z)))


def _sigmoid(z):
    return 1.0 / (1.0 + jnp.exp(-z))


def _split_bf16(x):
    hi = x.astype(BF16)
    lo = (x - hi.astype(F32)).astype(BF16)
    return hi, lo


def _dot(a, b):
    return jnp.dot(a, b, preferred_element_type=F32)


def _dot_nt(a, b):
    return lax.dot_general(a, b, (((1,), (1,)), ((), ())), preferred_element_type=F32)


def _dot_tn(a, b):
    return lax.dot_general(a, b, (((0,), (0,)), ((), ())), preferred_element_type=F32)


def _norm_proj_kernel(x_ref, g_ref, w_ref, b_ref, o_ref, h_scr, *, sigmoid):
    @pl.when(pl.program_id(1) == 0)
    def _():
        x = x_ref[...]
        ms = jnp.mean(x * x, axis=-1, keepdims=True)
        h_scr[...] = (x * lax.rsqrt(ms + RMS_EPS) * g_ref[...]).astype(BF16)

    acc = _dot(h_scr[...], w_ref[...]) + b_ref[...]
    if sigmoid:
        acc = _sigmoid(acc)
    o_ref[...] = acc.astype(o_ref.dtype)


def _norm_proj(x, g, w, b, *, out_dtype, sigmoid, tm, tn):
    t, d = x.shape
    n = w.shape[1]
    return pl.pallas_call(
        functools.partial(_norm_proj_kernel, sigmoid=sigmoid),
        out_shape=jax.ShapeDtypeStruct((t, n), out_dtype),
        grid=(t // tm, n // tn),
        in_specs=[
            pl.BlockSpec((tm, d), lambda i, j: (i, 0)),
            pl.BlockSpec((1, d), lambda i, j: (0, 0)),
            pl.BlockSpec((d, tn), lambda i, j: (0, j)),
            pl.BlockSpec((1, tn), lambda i, j: (0, j)),
        ],
        out_specs=pl.BlockSpec((tm, tn), lambda i, j: (i, j)),
        scratch_shapes=[pltpu.VMEM((tm, d), BF16)],
        compiler_params=_cparams(("parallel", "arbitrary")),
        name="norm_proj",
    )(x, g, w, b)


def _mlstm_kernel(qk_ref, v_ref, o_ref, ifc_ref, ifr_ref, cw_ref, cb_ref, nh_ref,
                  y_ref, ext_scr, c_scr, n_scr, m_scr):
    L = MLSTM_CHUNK
    step = pl.program_id(0)

    @pl.when(step == 0)
    def _():
        ext_scr[0:SUBLANE, :] = jnp.zeros((SUBLANE, ext_scr.shape[1]), F32)
        c_scr[...] = jnp.zeros_like(c_scr)
        n_scr[...] = jnp.zeros_like(n_scr)
        m_scr[...] = jnp.zeros_like(m_scr)

    raw = qk_ref[...]
    ext_scr[SUBLANE:SUBLANE + L, :] = raw
    conv = cb_ref[...]
    for j in range(CONV_W):
        off = SUBLANE - (CONV_W - 1) + j
        conv = conv + ext_scr[off:off + L, :] * cw_ref[j:j + 1, :]
    ext_scr[0:SUBLANE, :] = raw[L - SUBLANE:L, :]
    qk = conv * _sigmoid(conv)

    row = lax.broadcasted_iota(jnp.int32, (L, L), 0)
    col = lax.broadcasted_iota(jnp.int32, (L, L), 1)
    causal = row >= col
    tri_incl = jnp.where(causal, 1.0, 0.0).astype(BF16)
    tri_incl_t = jnp.where(col >= row, 1.0, 0.0).astype(BF16)

    for h in range(A_HEADS):
        q = qk[:, h * A_QK:(h + 1) * A_QK]
        k = qk[:, A_HEADS * A_QK + h * A_QK:A_HEADS * A_QK + (h + 1) * A_QK] * (A_QK ** -0.5)
        v = v_ref[:, h * A_V:(h + 1) * A_V]
        q_bf = q.astype(BF16)
        k_bf = k.astype(BF16)

        i_col = ifc_ref[:, h:h + 1]
        f_col = ifc_ref[:, A_HEADS + h:A_HEADS + h + 1]
        i_row = ifr_ref[h:h + 1, :]
        f_row = ifr_ref[A_HEADS + h:A_HEADS + h + 1, :]
        lf_col = jnp.broadcast_to(_log_sigmoid(f_col), (L, L))
        lf_row = jnp.broadcast_to(_log_sigmoid(f_row), (L, L))
        hi, lo = _split_bf16(lf_col)
        a_colb = _dot(tri_incl, hi) + _dot(tri_incl, lo)
        hi, lo = _split_bf16(lf_row)
        a_rowb = _dot(hi, tri_incl_t) + _dot(lo, tri_incl_t)
        a_col = a_colb[:, 0:1]
        g_tot = a_colb[L - 1:L, 0:1]

        m_prev = m_scr[h]
        n_prev = n_scr[h]
        ct_prev = c_scr[h]

        m_inter = a_col + m_prev
        d_log = jnp.where(causal, a_colb - a_rowb + i_row, -jnp.inf)
        m_t = jnp.maximum(m_inter, jnp.max(d_log, axis=-1, keepdims=True))
        p = jnp.exp(d_log - m_t) * _dot_nt(q_bf, k_bf)
        s_inter = jnp.exp(m_inter - m_t)
        num = s_inter * _dot(q_bf, ct_prev.astype(BF16)) + _dot(p.astype(BF16), v)
        den = (s_inter * jnp.sum(q * n_prev, axis=-1, keepdims=True)
               + jnp.sum(p, axis=-1, keepdims=True))
        hh = num / jnp.maximum(jnp.abs(den), jnp.exp(-m_t))

        hh = hh * lax.rsqrt(jnp.mean(hh * hh, axis=-1, keepdims=True) + RMS_EPS)
        gate = _sigmoid(o_ref[:, h * A_V:(h + 1) * A_V].astype(F32))
        y_ref[:, h * A_V:(h + 1) * A_V] = (
            hh * nh_ref[:, h * A_V:(h + 1) * A_V] * gate).astype(y_ref.dtype)

        w_col = g_tot - a_col + i_col
        m_loc = jnp.max(w_col, axis=0, keepdims=True)
        ke = k * jnp.exp(w_col - m_loc)
        ct_loc = _dot_tn(ke.astype(BF16), v)
        n_loc = jnp.sum(ke, axis=0, keepdims=True)
        m_new = jnp.maximum(g_tot + m_prev, m_loc)
        s_old = jnp.exp(g_tot + m_prev - m_new)
        s_new = jnp.exp(m_loc - m_new)
        c_scr[h] = s_old * ct_prev + s_new * ct_loc
        n_scr[h] = s_old * n_prev + s_new * n_loc
        m_scr[h] = m_new


def _mlstm(p_f32, p_bf, if_row, conv_w, conv_b, norm_head):
    t = p_f32.shape[0]
    L = MLSTM_CHUNK
    wq = conv_w.shape[1]
    wv = norm_head.shape[1]
    return pl.pallas_call(
        _mlstm_kernel,
        out_shape=jax.ShapeDtypeStruct((t, wv), BF16),
        grid=(t // L,),
        in_specs=[
            pl.BlockSpec((L, wq), lambda i: (i, 0)),
            pl.BlockSpec((L, wv), lambda i: (i, 0)),
            pl.BlockSpec((L, wv), lambda i: (i, 1)),
            pl.BlockSpec((L, LANE), lambda i: (i, wq // LANE)),
            pl.BlockSpec((SUBLANE, L), lambda i: (0, i)),
            pl.BlockSpec((CONV_W, wq), lambda i: (0, 0)),
            pl.BlockSpec((1, wq), lambda i: (0, 0)),
            pl.BlockSpec((1, wv), lambda i: (0, 0)),
        ],
        out_specs=pl.BlockSpec((L, wv), lambda i: (i, 0)),
        scratch_shapes=[
            pltpu.VMEM((L + SUBLANE, wq), F32),
            pltpu.VMEM((A_HEADS, A_QK, A_V), F32),
            pltpu.VMEM((A_HEADS, 1, A_QK), F32),
            pltpu.VMEM((A_HEADS, 1, 1), F32),
        ],
        compiler_params=_cparams(("arbitrary",)),
        name="mlstm",
    )(p_f32, p_bf, p_bf, p_f32, if_row, conv_w, conv_b, norm_head)


def _later(n):
    j = lax.broadcasted_iota(jnp.int32, (n, n), 0)
    s = lax.broadcasted_iota(jnp.int32, (n, n), 1)
    return jnp.where(j > s, 1.0, 0.0).astype(BF16)


def _sb_logs(z):
    ls = jnp.minimum(z, 0.0) - jnp.log(1.0 + jnp.exp(-jnp.abs(z)))
    return ls, ls - z


def _sb_later_sums(lk, later2):
    hi, lo = _split_bf16(lk)
    return _dot(jnp.concatenate([hi, lo], axis=1), later2)


def _sb_tile(q, k, v, carry, later2):
    ls, lk = _sb_logs(_dot_nt(q, k))
    r_in = _sb_later_sums(lk, later2)
    w = jnp.exp(ls + r_in + carry)
    return _dot(w.astype(BF16), v), carry + r_in[:, 0:1] + lk[:, 0:1]


def _sb_kernel(q_ref, kc_ref, vc_ref, kp_ref, vp_ref, k_ref, v_ref, o_ref, acc_scr, carry_scr):
    sub = SB_SUB
    nsub = q_ref.shape[0] // sub
    step = pl.program_id(1)
    later = _later(sub)
    later2 = jnp.concatenate([later, later], axis=0)
    row = lax.broadcasted_iota(jnp.int32, (sub, sub), 0)
    col = lax.broadcasted_iota(jnp.int32, (sub, sub), 1)
    diag_past = col < row
    has_prev = step > 0

    blk = [slice(j * sub, (j + 1) * sub) for j in range(nsub)]
    qs = [q_ref[b, :] for b in blk]
    k_d = [kc_ref[b, :] for b in blk]
    v_d = [vc_ref[b, :] for b in blk]
    k_p = [kp_ref[...]] + k_d[:-1]
    v_p = [vp_ref[...]] + v_d[:-1]
    ls_d, lk_d, ls_p, lk_p = [], [], [], []
    for j in range(nsub):
        ls, lk = _sb_logs(_dot_nt(qs[j], k_d[j]))
        ls_d.append(ls)
        lk_d.append(jnp.where(diag_past, lk, 0.0))
        ls, lk = _sb_logs(_dot_nt(qs[j], k_p[j]))
        ls_p.append(ls)
        lk_p.append(jnp.where(has_prev, lk, 0.0) if j == 0 else lk)
    r_in = _sb_later_sums(jnp.concatenate(lk_d + lk_p, axis=0), later2)
    first_max = []
    for j in range(nsub):
        r_d = r_in[j * sub:(j + 1) * sub]
        r_p = r_in[(nsub + j) * sub:(nsub + j + 1) * sub]
        carry_d = r_d[:, 0:1] + lk_d[j][:, 0:1]
        w_d = jnp.where(diag_past, jnp.exp(ls_d[j] + r_d), 0.0)
        w_p = jnp.exp(ls_p[j] + r_p + carry_d)
        if j == 0:
            w_p = jnp.where(has_prev, w_p, 0.0)
        carry = carry_d + r_p[:, 0:1] + lk_p[j][:, 0:1]
        acc_scr[j] = _dot(w_d.astype(BF16), v_d[j]) + _dot(w_p.astype(BF16), v_p[j])
        carry_scr[j] = carry
        first_max.append(jnp.max(carry))

    for j in range(nsub):
        def cond(state):
            kb, mx = state
            return jnp.logical_and(kb >= 0, mx > -SB_SKIP)

        def body(state, j=j):
            kb, _ = state
            k0 = pl.multiple_of(kb * sub, sub)
            acc, carry = _sb_tile(q_ref[j * sub:(j + 1) * sub, :], k_ref[pl.ds(k0, sub), :],
                                  v_ref[pl.ds(k0, sub), :], carry_scr[j], later2)
            acc_scr[j] += acc
            carry_scr[j] = carry
            return kb - 1, jnp.max(carry)

        lax.while_loop(cond, body, (step * nsub + j - 2, first_max[j]))
        o_ref[j * sub:(j + 1) * sub, :] = acc_scr[j].astype(o_ref.dtype)


def _stick_breaking(qkv, q_col, k_col, v_col):
    t = qkv.shape[0]
    d = B_HEAD_DIM
    bq = min(SB_BQ, t)
    nsub = bq // SB_SUB
    assert t % bq == 0 and bq % SB_SUB == 0

    def cur(col):
        return pl.BlockSpec((bq, d), lambda h, i: (i, col + h))

    def prev(col):
        return pl.BlockSpec((SB_SUB, d), lambda h, i: (jnp.maximum(i * nsub - 1, 0), col + h))

    def full(col):
        return pl.BlockSpec((t, d), lambda h, i: (0, col + h))

    return pl.pallas_call(
        _sb_kernel,
        out_shape=jax.ShapeDtypeStruct((t, B_HEADS * d), BF16),
        grid=(B_HEADS, t // bq),
        in_specs=[cur(q_col), cur(k_col), cur(v_col), prev(k_col), prev(v_col),
                  full(k_col), full(v_col)],
        out_specs=pl.BlockSpec((bq, d), lambda h, i: (i, h)),
        scratch_shapes=[pltpu.VMEM((nsub, SB_SUB, d), F32), pltpu.VMEM((nsub, SB_SUB, 1), F32)],
        compiler_params=_cparams(("parallel", "arbitrary")),
        name="stick_breaking",
    )(qkv, qkv, qkv, qkv, qkv, qkv, qkv)


def _mix_kernel(g1_ref, g2_ref, ya_ref, wa_ref, yb_ref, wb_ref, o_ref):
    pa = _dot(ya_ref[...], wa_ref[...])
    pb = _dot(yb_ref[...], wb_ref[...])
    o_ref[...] = (g1_ref[...].astype(F32) * pa + g2_ref[...].astype(F32) * pb).astype(o_ref.dtype)


def _mix(gates, y_a, w_a, y_b, w_b, *, tm, tn):
    t = y_a.shape[0]
    d = w_a.shape[1]
    nt = d // tn
    return pl.pallas_call(
        _mix_kernel,
        out_shape=jax.ShapeDtypeStruct((t, d), BF16),
        grid=(t // tm, nt),
        in_specs=[
            pl.BlockSpec((tm, tn), lambda i, j: (i, j)),
            pl.BlockSpec((tm, tn), lambda i, j: (i, j + nt)),
            pl.BlockSpec((tm, y_a.shape[1]), lambda i, j: (i, 0)),
            pl.BlockSpec((w_a.shape[0], tn), lambda i, j: (0, j)),
            pl.BlockSpec((tm, y_b.shape[1]), lambda i, j: (i, 0)),
            pl.BlockSpec((w_b.shape[0], tn), lambda i, j: (0, j)),
        ],
        out_specs=pl.BlockSpec((tm, tn), lambda i, j: (i, j)),
        compiler_params=_cparams(("parallel", "arbitrary")),
        name="mix",
    )(gates, gates, y_a, w_a, y_b, w_b)


def _out_kernel(x_ref, mix_ref, wo_ref, g_ref, wrh_ref, wrl_ref, br_ref,
                x1_ref, h2_ref, ri_ref, rg_ref, cnt_ref):
    tm = x_ref.shape[0]
    x1 = x_ref[...] + _dot(mix_ref[...], wo_ref[...])
    x1_ref[...] = x1
    ms = jnp.mean(x1 * x1, axis=-1, keepdims=True)
    h2 = x1 * lax.rsqrt(ms + RMS_EPS) * g_ref[...]
    h_hi, h_lo = _split_bf16(h2)
    h2_ref[...] = h_hi
    w_hi = wrh_ref[...]
    lg = (_dot_nt(w_hi, h_hi) + _dot_nt(w_hi, h_lo) + _dot_nt(wrl_ref[...], h_hi)
          + br_ref[:, 0:1])

    @pl.when(pl.program_id(0) == 0)
    def _():
        cnt_ref[...] = jnp.zeros_like(cnt_ref)

    row = lax.broadcasted_iota(jnp.int32, (N_EXPERTS, tm), 0)
    vals, idxs, hots = [], [], []
    for _ in range(TOP_K):
        mx = jnp.max(lg, axis=0, keepdims=True)
        idx = jnp.min(jnp.where(lg == mx, row, N_EXPERTS), axis=0, keepdims=True)
        hot = row == idx
        vals.append(mx)
        idxs.append(idx)
        hots.append(hot)
        lg = jnp.where(hot, -jnp.inf, lg)

    exps = [jnp.exp(v - vals[0]) for v in vals]
    denom = exps[0]
    for e in exps[1:]:
        denom = denom + e

    chosen = hots[0]
    for hot in hots[1:]:
        chosen = jnp.logical_or(chosen, hot)
    chosen_f = jnp.where(chosen, 1.0, 0.0)
    r = lax.broadcasted_iota(jnp.int32, (tm, tm), 0)
    c = lax.broadcasted_iota(jnp.int32, (tm, tm), 1)
    earlier = jnp.where(r < c, 1.0, 0.0).astype(BF16)
    before = _dot(chosen_f.astype(BF16), earlier) + cnt_ref[:, 0:1]
    cnt_ref[...] += jnp.sum(chosen_f, axis=1, keepdims=True)

    row8 = lax.broadcasted_iota(jnp.int32, (2 * TOP_K, tm), 0)
    ri = jnp.zeros((2 * TOP_K, tm), jnp.int32)
    rg = jnp.zeros((2 * TOP_K, tm), F32)
    for k in range(TOP_K):
        rank = jnp.sum(jnp.where(hots[k], before, 0.0), axis=0, keepdims=True).astype(jnp.int32)
        ri = jnp.where(row8 == k, idxs[k], ri)
        ri = jnp.where(row8 == TOP_K + k, rank, ri)
        rg = jnp.where(row8 == k, exps[k] / denom, rg)
    ri_ref[...] = ri
    rg_ref[...] = rg


def _out_proj(x, mixed, w_o, g, wr_hi, wr_lo, b_r, *, tm):
    t, d = x.shape
    ne = wr_hi.shape[0]
    return pl.pallas_call(
        _out_kernel,
        out_shape=(jax.ShapeDtypeStruct((t, d), F32),
                   jax.ShapeDtypeStruct((t, d), BF16),
                   jax.ShapeDtypeStruct((2 * TOP_K, t), jnp.int32),
                   jax.ShapeDtypeStruct((2 * TOP_K, t), F32),
                   jax.ShapeDtypeStruct((ne, LANE), F32)),
        grid=(t // tm,),
        in_specs=[
            pl.BlockSpec((tm, d), lambda i: (i, 0)),
            pl.BlockSpec((tm, d), lambda i: (i, 0)),
            pl.BlockSpec((d, d), lambda i: (0, 0)),
            pl.BlockSpec((1, d), lambda i: (0, 0)),
            pl.BlockSpec((ne, d), lambda i: (0, 0)),
            pl.BlockSpec((ne, d), lambda i: (0, 0)),
            pl.BlockSpec((ne, LANE), lambda i: (0, 0)),
        ],
        out_specs=(pl.BlockSpec((tm, d), lambda i: (i, 0)),
                   pl.BlockSpec((tm, d), lambda i: (i, 0)),
                   pl.BlockSpec((2 * TOP_K, tm), lambda i: (0, i)),
                   pl.BlockSpec((2 * TOP_K, tm), lambda i: (0, i)),
                   pl.BlockSpec((ne, LANE), lambda i: (0, 0))),
        compiler_params=_cparams(("arbitrary",)),
        name="out_proj",
    )(x, mixed, w_o, g, wr_hi, wr_lo, b_r)


def _expert_changed(be_ref, m):
    return jnp.logical_or(m == 0, be_ref[m] != be_ref[jnp.maximum(m - 1, 0)])


def _moe_up_kernel(be_ref, nu_ref, x_ref, wg_ref, wu_ref, bg_ref, bu_ref, o_ref, wg_scr, wu_scr):
    m = pl.program_id(1)

    @pl.when(m < nu_ref[0])
    def _():
        @pl.when(_expert_changed(be_ref, m))
        def _():
            wg_scr[...] = wg_ref[0].astype(BF16)
            wu_scr[...] = wu_ref[0].astype(BF16)

        x = x_ref[...]
        gate = jnp.minimum(_dot(x, wg_scr[...]) + bg_ref[0], SWIGLU_LIMIT)
        up = jnp.clip(_dot(x, wu_scr[...]) + bu_ref[0], -SWIGLU_LIMIT, SWIGLU_LIMIT)
        o_ref[...] = ((up + 1.0) * gate * _sigmoid(SWIGLU_ALPHA * gate)).astype(o_ref.dtype)

    @pl.when(m >= nu_ref[0])
    def _():
        o_ref[...] = jnp.zeros_like(o_ref)


def _moe_up(block_expert, n_used, x_pad, w_gu, b_gu, *, tf):
    rows, d = x_pad.shape
    nb = rows // MOE_TM
    d_ff = w_gu.shape[2] // 2
    nf = d_ff // tf

    def blk(m, nu):
        return jnp.maximum(jnp.minimum(m, nu[0] - 1), 0)

    return pl.pallas_call(
        _moe_up_kernel,
        out_shape=jax.ShapeDtypeStruct((rows, d_ff), BF16),
        grid_spec=pltpu.PrefetchScalarGridSpec(
            num_scalar_prefetch=2,
            grid=(nf, nb),
            in_specs=[
                pl.BlockSpec((MOE_TM, d), lambda n, m, be, nu: (blk(m, nu), 0)),
                pl.BlockSpec((1, d, tf), lambda n, m, be, nu: (be[blk(m, nu)], 0, n)),
                pl.BlockSpec((1, d, tf), lambda n, m, be, nu: (be[blk(m, nu)], 0, n + nf)),
                pl.BlockSpec((1, 1, tf), lambda n, m, be, nu: (be[blk(m, nu)], 0, n)),
                pl.BlockSpec((1, 1, tf), lambda n, m, be, nu: (be[blk(m, nu)], 0, n + nf)),
            ],
            out_specs=pl.BlockSpec((MOE_TM, tf), lambda n, m, be, nu: (m, n)),
            scratch_shapes=[pltpu.VMEM((d, tf), BF16), pltpu.VMEM((d, tf), BF16)],
        ),
        compiler_params=_cparams(("arbitrary", "arbitrary")),
        name="moe_up",
    )(block_expert, n_used, x_pad, w_gu, w_gu, b_gu, b_gu)


def _moe_down_kernel(be_ref, nu_ref, *refs):
    a_refs, (wd_ref, bd_ref, o_ref, wd_scr) = refs[:-4], refs[-4:]
    blocks_per_chunk = pl.num_programs(1) // len(a_refs)
    m = pl.program_id(1)

    @pl.when(m < nu_ref[0])
    def _():
        @pl.when(_expert_changed(be_ref, m))
        def _():
            wd_scr[...] = wd_ref[0].astype(BF16)

        for c, a_ref in enumerate(a_refs):
            @pl.when(m // blocks_per_chunk == c)
            def _(a_ref=a_ref):
                o_ref[...] = (_dot(a_ref[...], wd_scr[...]) + bd_ref[0]).astype(o_ref.dtype)

    @pl.when(m >= nu_ref[0])
    def _():
        o_ref[...] = jnp.zeros_like(o_ref)


def _moe_down(block_expert, n_used, acts, w_d, b_d, *, tn):
    rows, d_ff = acts[0].shape
    nbc = rows // MOE_TM
    nb = nbc * len(acts)
    d = w_d.shape[2]

    def blk(m, nu):
        return jnp.maximum(jnp.minimum(m, nu[0] - 1), 0)

    def act_spec(c):
        return pl.BlockSpec((MOE_TM, d_ff),
                            lambda n, m, be, nu: (jnp.clip(blk(m, nu) - c * nbc, 0, nbc - 1), 0))

    return pl.pallas_call(
        _moe_down_kernel,
        out_shape=jax.ShapeDtypeStruct((nb * MOE_TM, d), BF16),
        grid_spec=pltpu.PrefetchScalarGridSpec(
            num_scalar_prefetch=2,
            grid=(d // tn, nb),
            in_specs=[act_spec(c) for c in range(len(acts))] + [
                pl.BlockSpec((1, d_ff, tn), lambda n, m, be, nu: (be[blk(m, nu)], 0, n)),
                pl.BlockSpec((1, 1, tn), lambda n, m, be, nu: (be[blk(m, nu)], 0, n)),
            ],
            out_specs=pl.BlockSpec((MOE_TM, tn), lambda n, m, be, nu: (m, n)),
            scratch_shapes=[pltpu.VMEM((d_ff, tn), BF16)],
        ),
        compiler_params=_cparams(("arbitrary", "arbitrary")),
        name="moe_down",
    )(block_expert, n_used, *acts, w_d, b_d)


def _combine_kernel(x_ref, y0_ref, y1_ref, y2_ref, y3_ref, rg_ref, g_ref, o_ref, *, normalize):
    x = x_ref[...]
    rg = rg_ref[...]
    for k, y_ref in enumerate((y0_ref, y1_ref, y2_ref, y3_ref)):
        x = x + rg[:, k:k + 1] * y_ref[...].astype(F32)
    if normalize:
        ms = jnp.mean(x * x, axis=-1, keepdims=True)
        x = x * lax.rsqrt(ms + RMS_EPS) * g_ref[...]
    o_ref[...] = x


def _combine(x1, ys, rg, g, *, normalize, tm):
    t, d = x1.shape
    row_spec = pl.BlockSpec((tm, d), lambda i: (i, 0))
    return pl.pallas_call(
        functools.partial(_combine_kernel, normalize=normalize),
        out_shape=jax.ShapeDtypeStruct((t, d), F32),
        grid=(t // tm,),
        in_specs=[row_spec] * (1 + TOP_K) + [
            pl.BlockSpec((tm, rg.shape[1]), lambda i: (i, 0)),
            pl.BlockSpec((1, d), lambda i: (0, 0)),
        ],
        out_specs=row_spec,
        compiler_params=_cparams(("parallel",)),
        name="combine_norm",
    )(x1, *ys, rg, g)


def _route(ri, counts, t):
    p_n = t * TOP_K
    experts = ri[:TOP_K]
    rank = ri[TOP_K:]
    counts = counts[:, 0].astype(jnp.int32)
    padded = (counts + MOE_TM - 1) // MOE_TM * MOE_TM
    pad_end = jnp.cumsum(padded)
    pad_start = pad_end - padded
    e_ids = jnp.arange(N_EXPERTS, dtype=jnp.int32)[:, None, None]
    dest = rank + jnp.sum(jnp.where(experts[None] == e_ids, pad_start[:, None, None], 0), axis=0)
    nb = -(-p_n // MOE_TM) + N_EXPERTS
    src_tok = (jnp.arange(nb * MOE_TM, dtype=jnp.int32) % t).at[dest.reshape(p_n)].set(
        jnp.tile(jnp.arange(t, dtype=jnp.int32), TOP_K), mode="promise_in_bounds",
        unique_indices=True)
    block_start = jnp.arange(nb, dtype=jnp.int32) * MOE_TM
    block_expert = jnp.minimum(
        jnp.sum((block_start[:, None] >= pad_end[None, :]).astype(jnp.int32), axis=1),
        N_EXPERTS - 1)
    n_used = (pad_end[-1] // MOE_TM).astype(jnp.int32).reshape(1)
    return dest, src_tok, block_expert, n_used


def kernel(x, norm_mix, w_in, conv_w, conv_b, b_gates_if, norm_head, w_proj_a, w_proj_b,
           w_merge_gate, b_merge_gate, w_out, norm_ffn, w_router, b_router, w_gate_up,
           b_gate_up, w_down, b_down, norm_final):
    bn, s, d = x.shape
    assert bn == 1
    depth = norm_mix.shape[0]
    xt = x.reshape(s, d)
    tm = min(1024, s)
    aqk = A_HEADS * A_QK
    av = A_HEADS * A_V
    bw = B_HEADS * B_HEAD_DIM

    for l in range(depth):
        wl = w_in[l]
        c0 = 2 * aqk
        c1 = c0 + 2 * av
        c2 = c1 + 2 * A_HEADS
        w_f32 = jnp.concatenate(
            [wl[:, :c0], wl[:, c1:c2], jnp.zeros((d, LANE - 2 * A_HEADS), F32)], axis=1).astype(BF16)
        b_f32 = jnp.concatenate(
            [jnp.zeros((c0,), F32), b_gates_if[l], jnp.zeros((LANE - 2 * A_HEADS,), F32)])[None, :]
        w_bf = jnp.concatenate([wl[:, c0:c1], wl[:, c2:c2 + bw] * (B_HEAD_DIM ** -0.5), wl[:, c2 + bw:]],
                               axis=1).astype(BF16)
        g_mix = norm_mix[l][None, :]

        p_f32 = _norm_proj(xt, g_mix, w_f32, b_f32, out_dtype=F32, sigmoid=False,
                           tm=tm, tn=w_f32.shape[1])
        p_bf = _norm_proj(xt, g_mix, w_bf, jnp.zeros((1, w_bf.shape[1]), F32),
                          out_dtype=BF16, sigmoid=False, tm=tm, tn=1024)
        gates = _norm_proj(xt, g_mix, w_merge_gate[l].astype(BF16), b_merge_gate[l][None, :],
                           out_dtype=BF16, sigmoid=True, tm=tm, tn=1024)

        if_row = p_f32[:, c0:c0 + SUBLANE].T
        y_a = _mlstm(p_f32, p_bf, if_row, conv_w[l], conv_b[l][None, :], norm_head[l][None, :])
        qb = 2 * av // B_HEAD_DIM
        y_b = _stick_breaking(p_bf, qb, qb + B_HEADS, qb + 2 * B_HEADS)

        mixed = _mix(gates, y_a, w_proj_a[l].astype(BF16), y_b, w_proj_b[l].astype(BF16),
                     tm=tm, tn=512)
        wr_hi, wr_lo = _split_bf16(w_router[l].T)
        b_r = jnp.broadcast_to(b_router[l][:, None], (N_EXPERTS, LANE))
        x1, h2, ri, rg, counts = _out_proj(xt, mixed, w_out[l].astype(BF16), norm_ffn[l][None, :],
                                           wr_hi, wr_lo, b_r, tm=min(512, s))

        dest, src_tok, block_expert, n_used = _route(ri, counts, s)
        nbc = block_expert.shape[0] // MOE_CHUNKS
        acts = []
        for c in range(MOE_CHUNKS):
            x_pad = h2.at[src_tok[c * nbc * MOE_TM:(c + 1) * nbc * MOE_TM]].get(mode="promise_in_bounds")
            acts.append(_moe_up(block_expert[c * nbc:(c + 1) * nbc], jnp.clip(n_used - c * nbc, 0, nbc),
                                x_pad, w_gate_up[l], b_gate_up[l][:, None, :], tf=1024))
        y_pad = _moe_down(block_expert, n_used, acts, w_down[l], b_down[l][:, None, :], tn=2048)
        ys = [y_pad.at[dest[k]].get(mode="promise_in_bounds") for k in range(TOP_K)]
        xt = _combine(x1, ys, rg.T, norm_final[None, :], normalize=(l + 1 == depth), tm=min(512, s))
    return xt.reshape(bn, s, d)
```

```python
import functools

import jax
import jax.numpy as jnp
from jax import lax
from jax.experimental import pallas as pl
from jax.experimental.pallas import tpu as pltpu

F32 = jnp.float32
BF16 = jnp.bfloat16

RMS_EPS = 1e-5
A_HEADS = 4
A_QK = 128
A_V = 256
CONV_W = 4
B_HEADS = 8
B_HEAD_DIM = 128
N_EXPERTS = 32
TOP_K = 4
SWIGLU_LIMIT = 7.0
SWIGLU_ALPHA = 1.702

LANE = 128
SUBLANE = 8
VMEM_LIMIT = 60 * 1024 * 1024

MLSTM_CHUNK = 128
SB_BQ = 1024
SB_SUB = 128
SB_SKIP = 88.0
MOE_TM = 512
MOE_FIRST_CHUNK_DIV = 5
COMBINE_CHUNKS = 4


def _cparams(sem):
    return pltpu.CompilerParams(dimension_semantics=sem, vmem_limit_bytes=VMEM_LIMIT)


def _log_sigmoid(z):
    return jnp.minimum(z, 0.0) - jnp.log1p(jnp.exp(-jnp.abs(z)))


def _sigmoid(z):
    return 1.0 / (1.0 + jnp.exp(-z))


def _split_bf16(x):
    hi = x.astype(BF16)
    lo = (x - hi.astype(F32)).astype(BF16)
    return hi, lo


def _dot(a, b):
    return jnp.dot(a, b, preferred_element_type=F32)


def _dot_nt(a, b):
    return lax.dot_general(a, b, (((1,), (1,)), ((), ())), preferred_element_type=F32)


def _dot_tn(a, b):
    return lax.dot_general(a, b, (((0,), (0,)), ((), ())), preferred_element_type=F32)


def _norm_proj_kernel(x_ref, g_ref, w_ref, b_ref, o_ref, h_scr, *, sigmoid):
    @pl.when(pl.program_id(1) == 0)
    def _():
        x = x_ref[...]
        ms = jnp.mean(x * x, axis=-1, keepdims=True)
        h_scr[...] = (x * lax.rsqrt(ms + RMS_EPS) * g_ref[...]).astype(BF16)

    acc = _dot(h_scr[...], w_ref[...]) + b_ref[...]
    if sigmoid:
        acc = _sigmoid(acc)
    o_ref[...] = acc.astype(o_ref.dtype)


def _norm_proj(x, g, w, b, *, out_dtype, sigmoid, tm, tn):
    t, d = x.shape
    n = w.shape[1]
    return pl.pallas_call(
        functools.partial(_norm_proj_kernel, sigmoid=sigmoid),
        out_shape=jax.ShapeDtypeStruct((t, n), out_dtype),
        grid=(t // tm, n // tn),
        in_specs=[
            pl.BlockSpec((tm, d), lambda i, j: (i, 0)),
            pl.BlockSpec((1, d), lambda i, j: (0, 0)),
            pl.BlockSpec((d, tn), lambda i, j: (0, j)),
            pl.BlockSpec((1, tn), lambda i, j: (0, j)),
        ],
        out_specs=pl.BlockSpec((tm, tn), lambda i, j: (i, j)),
        scratch_shapes=[pltpu.VMEM((tm, d), BF16)],
        compiler_params=_cparams(("parallel", "arbitrary")),
        name="norm_proj",
    )(x, g, w, b)


def _mlstm_kernel(qk_ref, v_ref, o_ref, ifc_ref, ifr_ref, cw_ref, cb_ref, nh_ref,
                  y_ref, ext_scr, c_scr, n_scr, m_scr):
    L = MLSTM_CHUNK
    step = pl.program_id(0)

    @pl.when(step == 0)
    def _():
        ext_scr[0:SUBLANE, :] = jnp.zeros((SUBLANE, ext_scr.shape[1]), F32)
        c_scr[...] = jnp.zeros_like(c_scr)
        n_scr[...] = jnp.zeros_like(n_scr)
        m_scr[...] = jnp.zeros_like(m_scr)

    raw = qk_ref[...]
    ext_scr[SUBLANE:SUBLANE + L, :] = raw
    conv = cb_ref[...]
    for j in range(CONV_W):
        off = SUBLANE - (CONV_W - 1) + j
        conv = conv + ext_scr[off:off + L, :] * cw_ref[j:j + 1, :]
    ext_scr[0:SUBLANE, :] = raw[L - SUBLANE:L, :]
    qk = conv * _sigmoid(conv)

    row = lax.broadcasted_iota(jnp.int32, (L, L), 0)
    col = lax.broadcasted_iota(jnp.int32, (L, L), 1)
    causal = row >= col
    tri_incl = jnp.where(causal, 1.0, 0.0).astype(BF16)
    tri_incl_t = jnp.where(col >= row, 1.0, 0.0).astype(BF16)

    for h in range(A_HEADS):
        q = qk[:, h * A_QK:(h + 1) * A_QK]
        k = qk[:, A_HEADS * A_QK + h * A_QK:A_HEADS * A_QK + (h + 1) * A_QK] * (A_QK ** -0.5)
        v = v_ref[:, h * A_V:(h + 1) * A_V]
        q_bf = q.astype(BF16)
        k_bf = k.astype(BF16)

        i_col = ifc_ref[:, h:h + 1]
        f_col = ifc_ref[:, A_HEADS + h:A_HEADS + h + 1]
        i_row = ifr_ref[h:h + 1, :]
        f_row = ifr_ref[A_HEADS + h:A_HEADS + h + 1, :]
        lf_col = jnp.broadcast_to(_log_sigmoid(f_col), (L, L))
        lf_row = jnp.broadcast_to(_log_sigmoid(f_row), (L, L))
        hi, lo = _split_bf16(lf_col)
        a_colb = _dot(tri_incl, hi) + _dot(tri_incl, lo)
        hi, lo = _split_bf16(lf_row)
        a_rowb = _dot(hi, tri_incl_t) + _dot(lo, tri_incl_t)
        a_col = a_colb[:, 0:1]
        g_tot = a_colb[L - 1:L, 0:1]

        m_prev = m_scr[h]
        n_prev = n_scr[h]
        ct_prev = c_scr[h]

        m_inter = a_col + m_prev
        d_log = jnp.where(causal, a_colb - a_rowb + i_row, -jnp.inf)
        m_t = jnp.maximum(m_inter, jnp.max(d_log, axis=-1, keepdims=True))
        p = jnp.exp(d_log - m_t) * _dot_nt(q_bf, k_bf)
        s_inter = jnp.exp(m_inter - m_t)
        num = s_inter * _dot(q_bf, ct_prev.astype(BF16)) + _dot(p.astype(BF16), v)
        den = (s_inter * jnp.sum(q * n_prev, axis=-1, keepdims=True)
               + jnp.sum(p, axis=-1, keepdims=True))
        hh = num / jnp.maximum(jnp.abs(den), jnp.exp(-m_t))

        hh = hh * lax.rsqrt(jnp.mean(hh * hh, axis=-1, keepdims=True) + RMS_EPS)
        gate = _sigmoid(o_ref[:, h * A_V:(h + 1) * A_V].astype(F32))
        y_ref[:, h * A_V:(h + 1) * A_V] = (
            hh * nh_ref[:, h * A_V:(h + 1) * A_V] * gate).astype(y_ref.dtype)

        w_col = g_tot - a_col + i_col
        m_loc = jnp.max(w_col, axis=0, keepdims=True)
        ke = k * jnp.exp(w_col - m_loc)
        ct_loc = _dot_tn(ke.astype(BF16), v)
        n_loc = jnp.sum(ke, axis=0, keepdims=True)
        m_new = jnp.maximum(g_tot + m_prev, m_loc)
        s_old = jnp.exp(g_tot + m_prev - m_new)
        s_new = jnp.exp(m_loc - m_new)
        c_scr[h] = s_old * ct_prev + s_new * ct_loc
        n_scr[h] = s_old * n_prev + s_new * n_loc
        m_scr[h] = m_new


def _mlstm(p_f32, p_bf, if_row, conv_w, conv_b, norm_head):
    t = p_f32.shape[0]
    L = MLSTM_CHUNK
    wq = conv_w.shape[1]
    wv = norm_head.shape[1]
    return pl.pallas_call(
        _mlstm_kernel,
        out_shape=jax.ShapeDtypeStruct((t, wv), BF16),
        grid=(t // L,),
        in_specs=[
            pl.BlockSpec((L, wq), lambda i: (i, 0)),
            pl.BlockSpec((L, wv), lambda i: (i, 0)),
            pl.BlockSpec((L, wv), lambda i: (i, 1)),
            pl.BlockSpec((L, LANE), lambda i: (i, wq // LANE)),
            pl.BlockSpec((SUBLANE, L), lambda i: (0, i)),
            pl.BlockSpec((CONV_W, wq), lambda i: (0, 0)),
            pl.BlockSpec((1, wq), lambda i: (0, 0)),
            pl.BlockSpec((1, wv), lambda i: (0, 0)),
        ],
        out_specs=pl.BlockSpec((L, wv), lambda i: (i, 0)),
        scratch_shapes=[
            pltpu.VMEM((L + SUBLANE, wq), F32),
            pltpu.VMEM((A_HEADS, A_QK, A_V), F32),
            pltpu.VMEM((A_HEADS, 1, A_QK), F32),
            pltpu.VMEM((A_HEADS, 1, 1), F32),
        ],
        compiler_params=_cparams(("arbitrary",)),
        name="mlstm",
    )(p_f32, p_bf, p_bf, p_f32, if_row, conv_w, conv_b, norm_head)


def _later(n):
    j = lax.broadcasted_iota(jnp.int32, (n, n), 0)
    s = lax.broadcasted_iota(jnp.int32, (n, n), 1)
    return jnp.where(j > s, 1.0, 0.0).astype(BF16)


def _sb_logs(z):
    ls = jnp.minimum(z, 0.0) - jnp.log(1.0 + jnp.exp(-jnp.abs(z)))
    return ls, ls - z


def _sb_later_sums(lk, later2):
    hi, lo = _split_bf16(lk)
    return _dot(jnp.concatenate([hi, lo], axis=1), later2)


def _sb_tile(q, k, v, carry, later2):
    ls, lk = _sb_logs(_dot_nt(q, k))
    r_in = _sb_later_sums(lk, later2)
    w = jnp.exp(ls + r_in + carry)
    return _dot(w.astype(BF16), v), carry + r_in[:, 0:1] + lk[:, 0:1]


def _sb_kernel(q_ref, kc_ref, vc_ref, kp_ref, vp_ref, k_ref, v_ref, o_ref, acc_scr, carry_scr):
    sub = SB_SUB
    nsub = q_ref.shape[0] // sub
    step = pl.program_id(1)
    later = _later(sub)
    later2 = jnp.concatenate([later, later], axis=0)
    row = lax.broadcasted_iota(jnp.int32, (sub, sub), 0)
    col = lax.broadcasted_iota(jnp.int32, (sub, sub), 1)
    diag_past = col < row
    has_prev = step > 0

    blk = [slice(j * sub, (j + 1) * sub) for j in range(nsub)]
    qs = [q_ref[b, :] for b in blk]
    k_d = [kc_ref[b, :] for b in blk]
    v_d = [vc_ref[b, :] for b in blk]
    k_p = [kp_ref[...]] + k_d[:-1]
    v_p = [vp_ref[...]] + v_d[:-1]
    ls_d, lk_d, ls_p, lk_p = [], [], [], []
    for j in range(nsub):
        ls, lk = _sb_logs(_dot_nt(qs[j], k_d[j]))
        ls_d.append(ls)
        lk_d.append(jnp.where(diag_past, lk, 0.0))
        ls, lk = _sb_logs(_dot_nt(qs[j], k_p[j]))
        ls_p.append(ls)
        lk_p.append(jnp.where(has_prev, lk, 0.0) if j == 0 else lk)
    r_in = _sb_later_sums(jnp.concatenate(lk_d + lk_p, axis=0), later2)
    first_max = []
    for j in range(nsub):
        r_d = r_in[j * sub:(j + 1) * sub]
        r_p = r_in[(nsub + j) * sub:(nsub + j + 1) * sub]
        carry_d = r_d[:, 0:1] + lk_d[j][:, 0:1]
        w_d = jnp.where(diag_past, jnp.exp(ls_d[j] + r_d), 0.0)
        w_p = jnp.exp(ls_p[j] + r_p + carry_d)
        if j == 0:
            w_p = jnp.where(has_prev, w_p, 0.0)
        carry = carry_d + r_p[:, 0:1] + lk_p[j][:, 0:1]
        acc_scr[j] = _dot(w_d.astype(BF16), v_d[j]) + _dot(w_p.astype(BF16), v_p[j])
        carry_scr[j] = carry
        first_max.append(jnp.max(carry))

    for j in range(nsub):
        def cond(state):
            kb, mx = state
            return jnp.logical_and(kb >= 0, mx > -SB_SKIP)

        def body(state, j=j):
            kb, _ = state
            k0 = pl.multiple_of(kb * sub, sub)
            acc, carry = _sb_tile(q_ref[j * sub:(j + 1) * sub, :], k_ref[pl.ds(k0, sub), :],
                                  v_ref[pl.ds(k0, sub), :], carry_scr[j], later2)
            acc_scr[j] += acc
            carry_scr[j] = carry
            return kb - 1, jnp.max(carry)

        lax.while_loop(cond, body, (step * nsub + j - 2, first_max[j]))
        o_ref[j * sub:(j + 1) * sub, :] = acc_scr[j].astype(o_ref.dtype)


def _stick_breaking(qkv, q_col, k_col, v_col):
    t = qkv.shape[0]
    d = B_HEAD_DIM
    bq = min(SB_BQ, t)
    nsub = bq // SB_SUB
    assert t % bq == 0 and bq % SB_SUB == 0

    def cur(col):
        return pl.BlockSpec((bq, d), lambda h, i: (i, col + h))

    def prev(col):
        return pl.BlockSpec((SB_SUB, d), lambda h, i: (jnp.maximum(i * nsub - 1, 0), col + h))

    def full(col):
        return pl.BlockSpec((t, d), lambda h, i: (0, col + h))

    return pl.pallas_call(
        _sb_kernel,
        out_shape=jax.ShapeDtypeStruct((t, B_HEADS * d), BF16),
        grid=(B_HEADS, t // bq),
        in_specs=[cur(q_col), cur(k_col), cur(v_col), prev(k_col), prev(v_col),
                  full(k_col), full(v_col)],
        out_specs=pl.BlockSpec((bq, d), lambda h, i: (i, h)),
        scratch_shapes=[pltpu.VMEM((nsub, SB_SUB, d), F32), pltpu.VMEM((nsub, SB_SUB, 1), F32)],
        compiler_params=_cparams(("parallel", "arbitrary")),
        name="stick_breaking",
    )(qkv, qkv, qkv, qkv, qkv, qkv, qkv)


def _mix_kernel(g1_ref, g2_ref, ya_ref, wa_ref, yb_ref, wb_ref, o_ref):
    pa = _dot(ya_ref[...], wa_ref[...])
    pb = _dot(yb_ref[...], wb_ref[...])
    o_ref[...] = (g1_ref[...].astype(F32) * pa + g2_ref[...].astype(F32) * pb).astype(o_ref.dtype)


def _mix(gates, y_a, w_a, y_b, w_b, *, tm, tn):
    t = y_a.shape[0]
    d = w_a.shape[1]
    nt = d // tn
    return pl.pallas_call(
        _mix_kernel,
        out_shape=jax.ShapeDtypeStruct((t, d), BF16),
        grid=(t // tm, nt),
        in_specs=[
            pl.BlockSpec((tm, tn), lambda i, j: (i, j)),
            pl.BlockSpec((tm, tn), lambda i, j: (i, j + nt)),
            pl.BlockSpec((tm, y_a.shape[1]), lambda i, j: (i, 0)),
            pl.BlockSpec((w_a.shape[0], tn), lambda i, j: (0, j)),
            pl.BlockSpec((tm, y_b.shape[1]), lambda i, j: (i, 0)),
            pl.BlockSpec((w_b.shape[0], tn), lambda i, j: (0, j)),
        ],
        out_specs=pl.BlockSpec((tm, tn), lambda i, j: (i, j)),
        compiler_params=_cparams(("parallel", "arbitrary")),
        name="mix",
    )(gates, gates, y_a, w_a, y_b, w_b)


def _out_kernel(x_ref, mix_ref, wo_ref, g_ref, wrh_ref, wrl_ref, br_ref,
                x1_ref, h2_ref, ri_ref, rg_ref, cnt_ref):
    tm = x_ref.shape[0]
    x1 = x_ref[...] + _dot(mix_ref[...], wo_ref[...])
    x1_ref[...] = x1
    ms = jnp.mean(x1 * x1, axis=-1, keepdims=True)
    h2 = x1 * lax.rsqrt(ms + RMS_EPS) * g_ref[...]
    h_hi, h_lo = _split_bf16(h2)
    h2_ref[...] = h_hi
    w_hi = wrh_ref[...]
    lg = (_dot_nt(w_hi, h_hi) + _dot_nt(w_hi, h_lo) + _dot_nt(wrl_ref[...], h_hi)
          + br_ref[:, 0:1])

    @pl.when(pl.program_id(0) == 0)
    def _():
        cnt_ref[...] = jnp.zeros_like(cnt_ref)

    row = lax.broadcasted_iota(jnp.int32, (N_EXPERTS, tm), 0)
    vals, idxs, hots = [], [], []
    for _ in range(TOP_K):
        mx = jnp.max(lg, axis=0, keepdims=True)
        idx = jnp.min(jnp.where(lg == mx, row, N_EXPERTS), axis=0, keepdims=True)
        hot = row == idx
        vals.append(mx)
        idxs.append(idx)
        hots.append(hot)
        lg = jnp.where(hot, -jnp.inf, lg)

    exps = [jnp.exp(v - vals[0]) for v in vals]
    denom = exps[0]
    for e in exps[1:]:
        denom = denom + e

    chosen = hots[0]
    for hot in hots[1:]:
        chosen = jnp.logical_or(chosen, hot)
    chosen_f = jnp.where(chosen, 1.0, 0.0)
    r = lax.broadcasted_iota(jnp.int32, (tm, tm), 0)
    c = lax.broadcasted_iota(jnp.int32, (tm, tm), 1)
    earlier = jnp.where(r < c, 1.0, 0.0).astype(BF16)
    before = _dot(chosen_f.astype(BF16), earlier) + cnt_ref[:, 0:1]
    cnt_ref[...] += jnp.sum(chosen_f, axis=1, keepdims=True)

    row8 = lax.broadcasted_iota(jnp.int32, (2 * TOP_K, tm), 0)
    ri = jnp.zeros((2 * TOP_K, tm), jnp.int32)
    rg = jnp.zeros((2 * TOP_K, tm), F32)
    for k in range(TOP_K):
        rank = jnp.sum(jnp.where(hots[k], before, 0.0), axis=0, keepdims=True).astype(jnp.int32)
        ri = jnp.where(row8 == k, idxs[k], ri)
        ri = jnp.where(row8 == TOP_K + k, rank, ri)
        rg = jnp.where(row8 == k, exps[k] / denom, rg)
    ri_ref[...] = ri
    rg_ref[...] = rg


def _out_proj(x, mixed, w_o, g, wr_hi, wr_lo, b_r, *, tm):
    t, d = x.shape
    ne = wr_hi.shape[0]
    return pl.pallas_call(
        _out_kernel,
        out_shape=(jax.ShapeDtypeStruct((t, d), F32),
                   jax.ShapeDtypeStruct((t, d), BF16),
                   jax.ShapeDtypeStruct((2 * TOP_K, t), jnp.int32),
                   jax.ShapeDtypeStruct((2 * TOP_K, t), F32),
                   jax.ShapeDtypeStruct((ne, LANE), F32)),
        grid=(t // tm,),
        in_specs=[
            pl.BlockSpec((tm, d), lambda i: (i, 0)),
            pl.BlockSpec((tm, d), lambda i: (i, 0)),
            pl.BlockSpec((d, d), lambda i: (0, 0)),
            pl.BlockSpec((1, d), lambda i: (0, 0)),
            pl.BlockSpec((ne, d), lambda i: (0, 0)),
            pl.BlockSpec((ne, d), lambda i: (0, 0)),
            pl.BlockSpec((ne, LANE), lambda i: (0, 0)),
        ],
        out_specs=(pl.BlockSpec((tm, d), lambda i: (i, 0)),
                   pl.BlockSpec((tm, d), lambda i: (i, 0)),
                   pl.BlockSpec((2 * TOP_K, tm), lambda i: (0, i)),
                   pl.BlockSpec((2 * TOP_K, tm), lambda i: (0, i)),
                   pl.BlockSpec((ne, LANE), lambda i: (0, 0))),
        compiler_params=_cparams(("arbitrary",)),
        name="out_proj",
    )(x, mixed, w_o, g, wr_hi, wr_lo, b_r)


def _expert_changed(be_ref, m):
    return jnp.logical_or(m == 0, be_ref[m] != be_ref[jnp.maximum(m - 1, 0)])


def _moe_up_kernel(be_ref, nu_ref, x_ref, wg_ref, wu_ref, bg_ref, bu_ref, o_ref, wg_scr, wu_scr):
    m = pl.program_id(1)

    @pl.when(m < nu_ref[0])
    def _():
        @pl.when(_expert_changed(be_ref, m))
        def _():
            wg_scr[...] = wg_ref[0].astype(BF16)
            wu_scr[...] = wu_ref[0].astype(BF16)

        x = x_ref[...]
        gate = jnp.minimum(_dot(x, wg_scr[...]) + bg_ref[0], SWIGLU_LIMIT)
        up = jnp.clip(_dot(x, wu_scr[...]) + bu_ref[0], -SWIGLU_LIMIT, SWIGLU_LIMIT)
        o_ref[...] = ((up + 1.0) * gate * _sigmoid(SWIGLU_ALPHA * gate)).astype(o_ref.dtype)

    @pl.when(m >= nu_ref[0])
    def _():
        o_ref[...] = jnp.zeros_like(o_ref)


def _moe_up(block_expert, n_used, x_pad, w_gu, b_gu, *, tf):
    rows, d = x_pad.shape
    nb = rows // MOE_TM
    d_ff = w_gu.shape[2] // 2
    nf = d_ff // tf

    def blk(m, nu):
        return jnp.maximum(jnp.minimum(m, nu[0] - 1), 0)

    return pl.pallas_call(
        _moe_up_kernel,
        out_shape=jax.ShapeDtypeStruct((rows, d_ff), BF16),
        grid_spec=pltpu.PrefetchScalarGridSpec(
            num_scalar_prefetch=2,
            grid=(nf, nb),
            in_specs=[
                pl.BlockSpec((MOE_TM, d), lambda n, m, be, nu: (blk(m, nu), 0)),
                pl.BlockSpec((1, d, tf), lambda n, m, be, nu: (be[blk(m, nu)], 0, n)),
                pl.BlockSpec((1, d, tf), lambda n, m, be, nu: (be[blk(m, nu)], 0, n + nf)),
                pl.BlockSpec((1, 1, tf), lambda n, m, be, nu: (be[blk(m, nu)], 0, n)),
                pl.BlockSpec((1, 1, tf), lambda n, m, be, nu: (be[blk(m, nu)], 0, n + nf)),
            ],
            out_specs=pl.BlockSpec((MOE_TM, tf), lambda n, m, be, nu: (m, n)),
            scratch_shapes=[pltpu.VMEM((d, tf), BF16), pltpu.VMEM((d, tf), BF16)],
        ),
        compiler_params=_cparams(("arbitrary", "arbitrary")),
        name="moe_up",
    )(block_expert, n_used, x_pad, w_gu, w_gu, b_gu, b_gu)


def _moe_down_kernel(be_ref, nu_ref, *refs, chunk_starts):
    a_refs, (wd_ref, bd_ref, o_ref, wd_scr) = refs[:-4], refs[-4:]
    m = pl.program_id(1)

    @pl.when(m < nu_ref[0])
    def _():
        @pl.when(_expert_changed(be_ref, m))
        def _():
            wd_scr[...] = wd_ref[0].astype(BF16)

        for c, a_ref in enumerate(a_refs):
            @pl.when(jnp.logical_and(m >= chunk_starts[c], m < chunk_starts[c + 1]))
            def _(a_ref=a_ref):
                o_ref[...] = (_dot(a_ref[...], wd_scr[...]) + bd_ref[0]).astype(o_ref.dtype)

    @pl.when(m >= nu_ref[0])
    def _():
        o_ref[...] = jnp.zeros_like(o_ref)


def _moe_down(block_expert, n_used, acts, w_d, b_d, *, tn):
    d_ff = acts[0].shape[1]
    chunk_starts = [0]
    for a in acts:
        chunk_starts.append(chunk_starts[-1] + a.shape[0] // MOE_TM)
    nb = chunk_starts[-1]
    d = w_d.shape[2]

    def blk(m, nu):
        return jnp.maximum(jnp.minimum(m, nu[0] - 1), 0)

    def act_spec(c):
        lo, n_c = chunk_starts[c], chunk_starts[c + 1] - chunk_starts[c]
        return pl.BlockSpec((MOE_TM, d_ff),
                            lambda n, m, be, nu: (jnp.clip(blk(m, nu) - lo, 0, n_c - 1), 0))

    return pl.pallas_call(
        functools.partial(_moe_down_kernel, chunk_starts=tuple(chunk_starts)),
        out_shape=jax.ShapeDtypeStruct((nb * MOE_TM, d), BF16),
        grid_spec=pltpu.PrefetchScalarGridSpec(
            num_scalar_prefetch=2,
            grid=(d // tn, nb),
            in_specs=[act_spec(c) for c in range(len(acts))] + [
                pl.BlockSpec((1, d_ff, tn), lambda n, m, be, nu: (be[blk(m, nu)], 0, n)),
                pl.BlockSpec((1, 1, tn), lambda n, m, be, nu: (be[blk(m, nu)], 0, n)),
            ],
            out_specs=pl.BlockSpec((MOE_TM, tn), lambda n, m, be, nu: (m, n)),
            scratch_shapes=[pltpu.VMEM((d_ff, tn), BF16)],
        ),
        compiler_params=_cparams(("arbitrary", "arbitrary")),
        name="moe_down",
    )(block_expert, n_used, *acts, w_d, b_d)


def _combine_kernel(x_ref, y0_ref, y1_ref, y2_ref, y3_ref, rg_ref, g_ref, o_ref, *, normalize):
    x = x_ref[...]
    rg = rg_ref[...]
    for k, y_ref in enumerate((y0_ref, y1_ref, y2_ref, y3_ref)):
        x = x + rg[:, k:k + 1] * y_ref[...].astype(F32)
    if normalize:
        ms = jnp.mean(x * x, axis=-1, keepdims=True)
        x = x * lax.rsqrt(ms + RMS_EPS) * g_ref[...]
    o_ref[...] = x


def _combine(x_buf, y_rows, rg, g, *, chunk, n_chunk_tokens, normalize, tm):
    t, d = x_buf.shape
    nblk = n_chunk_tokens // tm
    x_spec = pl.BlockSpec((tm, d), lambda i: (chunk * nblk + i, 0))

    def y_spec(k):
        return pl.BlockSpec((tm, d), lambda i: (k * nblk + i, 0))

    return pl.pallas_call(
        functools.partial(_combine_kernel, normalize=normalize),
        out_shape=jax.ShapeDtypeStruct((t, d), F32),
        grid=(nblk,),
        in_specs=[x_spec] + [y_spec(k) for k in range(TOP_K)] + [
            pl.BlockSpec((tm, rg.shape[1]), lambda i: (chunk * nblk + i, 0)),
            pl.BlockSpec((1, d), lambda i: (0, 0)),
        ],
        out_specs=x_spec,
        input_output_aliases={0: 0},
        compiler_params=_cparams(("parallel",)),
        name="combine_norm",
    )(x_buf, y_rows, y_rows, y_rows, y_rows, rg, g)


def _route(ri, counts, t):
    p_n = t * TOP_K
    experts = ri[:TOP_K]
    rank = ri[TOP_K:]
    counts = counts[:, 0].astype(jnp.int32)
    padded = (counts + MOE_TM - 1) // MOE_TM * MOE_TM
    pad_end = jnp.cumsum(padded)
    pad_start = pad_end - padded
    e_ids = jnp.arange(N_EXPERTS, dtype=jnp.int32)[:, None, None]
    dest = rank + jnp.sum(jnp.where(experts[None] == e_ids, pad_start[:, None, None], 0), axis=0)
    nb = -(-p_n // MOE_TM) + N_EXPERTS
    src_tok = (jnp.arange(nb * MOE_TM, dtype=jnp.int32) % t).at[dest.reshape(p_n)].set(
        jnp.tile(jnp.arange(t, dtype=jnp.int32), TOP_K), mode="promise_in_bounds",
        unique_indices=True)
    block_start = jnp.arange(nb, dtype=jnp.int32) * MOE_TM
    block_expert = jnp.minimum(
        jnp.sum((block_start[:, None] >= pad_end[None, :]).astype(jnp.int32), axis=1),
        N_EXPERTS - 1)
    n_used = (pad_end[-1] // MOE_TM).astype(jnp.int32).reshape(1)
    return dest, src_tok, block_expert, n_used


def kernel(x, norm_mix, w_in, conv_w, conv_b, b_gates_if, norm_head, w_proj_a, w_proj_b,
           w_merge_gate, b_merge_gate, w_out, norm_ffn, w_router, b_router, w_gate_up,
           b_gate_up, w_down, b_down, norm_final):
    bn, s, d = x.shape
    assert bn == 1
    depth = norm_mix.shape[0]
    xt = x.reshape(s, d)
    tm = min(1024, s)
    aqk = A_HEADS * A_QK
    av = A_HEADS * A_V
    bw = B_HEADS * B_HEAD_DIM

    for l in range(depth):
        wl = w_in[l]
        c0 = 2 * aqk
        c1 = c0 + 2 * av
        c2 = c1 + 2 * A_HEADS
        w_f32 = jnp.concatenate(
            [wl[:, :c0], wl[:, c1:c2], jnp.zeros((d, LANE - 2 * A_HEADS), F32)], axis=1).astype(BF16)
        b_f32 = jnp.concatenate(
            [jnp.zeros((c0,), F32), b_gates_if[l], jnp.zeros((LANE - 2 * A_HEADS,), F32)])[None, :]
        w_bf = jnp.concatenate([wl[:, c0:c1], wl[:, c2:c2 + bw] * (B_HEAD_DIM ** -0.5), wl[:, c2 + bw:]],
                               axis=1).astype(BF16)
        g_mix = norm_mix[l][None, :]

        p_f32 = _norm_proj(xt, g_mix, w_f32, b_f32, out_dtype=F32, sigmoid=False,
                           tm=tm, tn=w_f32.shape[1])
        p_bf = _norm_proj(xt, g_mix, w_bf, jnp.zeros((1, w_bf.shape[1]), F32),
                          out_dtype=BF16, sigmoid=False, tm=tm, tn=1024)
        gates = _norm_proj(xt, g_mix, w_merge_gate[l].astype(BF16), b_merge_gate[l][None, :],
                           out_dtype=BF16, sigmoid=True, tm=tm, tn=1024)

        if_row = p_f32[:, c0:c0 + SUBLANE].T
        y_a = _mlstm(p_f32, p_bf, if_row, conv_w[l], conv_b[l][None, :], norm_head[l][None, :])
        qb = 2 * av // B_HEAD_DIM
        y_b = _stick_breaking(p_bf, qb, qb + B_HEADS, qb + 2 * B_HEADS)

        mixed = _mix(gates, y_a, w_proj_a[l].astype(BF16), y_b, w_proj_b[l].astype(BF16),
                     tm=tm, tn=512)
        wr_hi, wr_lo = _split_bf16(w_router[l].T)
        b_r = jnp.broadcast_to(b_router[l][:, None], (N_EXPERTS, LANE))
        x1, h2, ri, rg, counts = _out_proj(xt, mixed, w_out[l].astype(BF16), norm_ffn[l][None, :],
                                           wr_hi, wr_lo, b_r, tm=min(512, s))

        dest, src_tok, block_expert, n_used = _route(ri, counts, s)
        nb = block_expert.shape[0]
        bounds = [0, nb // MOE_FIRST_CHUNK_DIV, nb]
        acts = []
        for lo, hi in zip(bounds[:-1], bounds[1:]):
            x_pad = h2.at[src_tok[lo * MOE_TM:hi * MOE_TM]].get(mode="promise_in_bounds")
            acts.append(_moe_up(block_expert[lo:hi], jnp.clip(n_used - lo, 0, hi - lo),
                                x_pad, w_gate_up[l], b_gate_up[l][:, None, :], tf=1024))
        y_pad = _moe_down(block_expert, n_used, acts, w_down[l], b_down[l][:, None, :], tn=2048)
        nct = s // COMBINE_CHUNKS
        rg_col = rg.T
        xt = x1
        for j in range(COMBINE_CHUNKS):
            idx = dest[:, j * nct:(j + 1) * nct].reshape(TOP_K * nct)
            y_rows = y_pad.at[idx].get(mode="promise_in_bounds")
            xt = _combine(xt, y_rows, rg_col, norm_final[None, :], chunk=j, n_chunk_tokens=nct,
                          normalize=(l + 1 == depth), tm=min(512, nct))
    return xt.reshape(bn, s, d)
```

```python
import functools

import jax
import jax.numpy as jnp
from jax import lax
from jax.experimental import pallas as pl
from jax.experimental.pallas import tpu as pltpu

F32 = jnp.float32
BF16 = jnp.bfloat16

RMS_EPS = 1e-5
A_HEADS = 4
A_QK = 128
A_V = 256
CONV_W = 4
B_HEADS = 8
B_HEAD_DIM = 128
N_EXPERTS = 32
TOP_K = 4
SWIGLU_LIMIT = 7.0
SWIGLU_ALPHA = 1.702

LANE = 128
SUBLANE = 8
VMEM_LIMIT = 60 * 1024 * 1024

MLSTM_CHUNK = 128
SB_BQ = 1024
SB_SUB = 128
SB_SKIP = 88.0
MOE_TM = 512
MOE_FIRST_CHUNK_DIV = 5
COMBINE_CHUNKS = 4


def _cparams(sem):
    return pltpu.CompilerParams(dimension_semantics=sem, vmem_limit_bytes=VMEM_LIMIT)


def _log_sigmoid(z):
    return jnp.minimum(z, 0.0) - jnp.log1p(jnp.exp(-jnp.abs(z)))


def _sigmoid(z):
    return 1.0 / (1.0 + jnp.exp(-z))


def _split_bf16(x):
    hi = x.astype(BF16)
    lo = (x - hi.astype(F32)).astype(BF16)
    return hi, lo


def _dot(a, b):
    return jnp.dot(a, b, preferred_element_type=F32)


def _dot_nt(a, b):
    return lax.dot_general(a, b, (((1,), (1,)), ((), ())), preferred_element_type=F32)


def _dot_tn(a, b):
    return lax.dot_general(a, b, (((0,), (0,)), ((), ())), preferred_element_type=F32)


def _norm_proj_kernel(x_ref, g_ref, w_ref, b_ref, o_ref, h_ref, gt_ref):
    x = x_ref[...]
    ms = jnp.mean(x * x, axis=-1, keepdims=True)
    h = (x * lax.rsqrt(ms + RMS_EPS) * g_ref[...]).astype(BF16)
    h_ref[...] = h
    acc = _dot(h, w_ref[...]) + b_ref[...]
    o_ref[...] = acc
    gt_ref[...] = acc[:, acc.shape[1] - LANE:].T[:SUBLANE, :]


def _norm_proj(x, g, w, b, *, tm):
    t, d = x.shape
    n = w.shape[1]
    return pl.pallas_call(
        _norm_proj_kernel,
        out_shape=(jax.ShapeDtypeStruct((t, n), F32),
                   jax.ShapeDtypeStruct((t, d), BF16),
                   jax.ShapeDtypeStruct((SUBLANE, t), F32)),
        grid=(t // tm,),
        in_specs=[
            pl.BlockSpec((tm, d), lambda i: (i, 0)),
            pl.BlockSpec((1, d), lambda i: (0, 0)),
            pl.BlockSpec((d, n), lambda i: (0, 0)),
            pl.BlockSpec((1, n), lambda i: (0, 0)),
        ],
        out_specs=(pl.BlockSpec((tm, n), lambda i: (i, 0)),
                   pl.BlockSpec((tm, d), lambda i: (i, 0)),
                   pl.BlockSpec((SUBLANE, tm), lambda i: (0, i))),
        compiler_params=_cparams(("parallel",)),
        name="norm_proj",
    )(x, g, w, b)


def _proj_kernel(h_ref, w_ref, b_ref, o_ref, *, sigmoid_tiles, n_tiles):
    acc = _dot(h_ref[...], w_ref[...]) + b_ref[...]
    lo, hi = sigmoid_tiles
    if lo == 0 and hi == n_tiles:
        o_ref[...] = _sigmoid(acc).astype(o_ref.dtype)
    elif lo == hi:
        o_ref[...] = acc.astype(o_ref.dtype)
    else:
        j = pl.program_id(1)
        gated = jnp.logical_and(j >= lo, j < hi)

        @pl.when(gated)
        def _():
            o_ref[...] = _sigmoid(acc).astype(o_ref.dtype)

        @pl.when(jnp.logical_not(gated))
        def _():
            o_ref[...] = acc.astype(o_ref.dtype)


def _proj(h, w, b, *, sigmoid_tiles, tm, tn):
    t, d = h.shape
    n = w.shape[1]
    return pl.pallas_call(
        functools.partial(_proj_kernel, sigmoid_tiles=sigmoid_tiles, n_tiles=n // tn),
        out_shape=jax.ShapeDtypeStruct((t, n), BF16),
        grid=(t // tm, n // tn),
        in_specs=[
            pl.BlockSpec((tm, d), lambda i, j: (i, 0)),
            pl.BlockSpec((d, tn), lambda i, j: (0, j)),
            pl.BlockSpec((1, tn), lambda i, j: (0, j)),
        ],
        out_specs=pl.BlockSpec((tm, tn), lambda i, j: (i, j)),
        compiler_params=_cparams(("parallel", "arbitrary")),
        name="proj",
    )(h, w, b)


def _mlstm_kernel(qk_ref, v_ref, o_ref, ifc_ref, ifr_ref, cw_ref, cb_ref, nh_ref,
                  y_ref, ext_scr, c_scr, n_scr, m_scr):
    L = MLSTM_CHUNK
    step = pl.program_id(0)

    @pl.when(step == 0)
    def _():
        ext_scr[0:SUBLANE, :] = jnp.zeros((SUBLANE, ext_scr.shape[1]), F32)
        c_scr[...] = jnp.zeros_like(c_scr)
        n_scr[...] = jnp.zeros_like(n_scr)
        m_scr[...] = jnp.zeros_like(m_scr)

    raw = qk_ref[...]
    ext_scr[SUBLANE:SUBLANE + L, :] = raw
    conv = cb_ref[...]
    for j in range(CONV_W):
        off = SUBLANE - (CONV_W - 1) + j
        conv = conv + ext_scr[off:off + L, :] * cw_ref[j:j + 1, :]
    ext_scr[0:SUBLANE, :] = raw[L - SUBLANE:L, :]
    qk = conv * _sigmoid(conv)

    row = lax.broadcasted_iota(jnp.int32, (L, L), 0)
    col = lax.broadcasted_iota(jnp.int32, (L, L), 1)
    causal = row >= col
    tri_incl = jnp.where(causal, 1.0, 0.0).astype(BF16)
    tri_incl_t = jnp.where(col >= row, 1.0, 0.0).astype(BF16)

    hi, lo = _split_bf16(_log_sigmoid(ifc_ref[...]))
    a_cols = _dot(tri_incl, hi) + _dot(tri_incl, lo)
    hi, lo = _split_bf16(_log_sigmoid(ifr_ref[...]))
    a_rows = _dot(hi, tri_incl_t) + _dot(lo, tri_incl_t)

    for h in range(A_HEADS):
        q = qk[:, h * A_QK:(h + 1) * A_QK]
        k = qk[:, A_HEADS * A_QK + h * A_QK:A_HEADS * A_QK + (h + 1) * A_QK] * (A_QK ** -0.5)
        v = v_ref[:, h * A_V:(h + 1) * A_V]
        q_bf = q.astype(BF16)
        k_bf = k.astype(BF16)

        i_col = ifc_ref[:, h:h + 1]
        i_row = ifr_ref[h:h + 1, :]
        a_col = a_cols[:, A_HEADS + h:A_HEADS + h + 1]
        a_row = a_rows[A_HEADS + h:A_HEADS + h + 1, :]
        g_tot = a_col[L - 1:L, :]

        m_prev = m_scr[h]
        n_prev = n_scr[h]
        ct_prev = c_scr[h]

        m_inter = a_col + m_prev
        d_log = jnp.where(causal, a_col - a_row + i_row, -jnp.inf)
        m_t = jnp.maximum(m_inter, jnp.max(d_log, axis=-1, keepdims=True))
        p = jnp.exp(d_log - m_t) * _dot_nt(q_bf, k_bf)
        s_inter = jnp.exp(m_inter - m_t)
        num = s_inter * _dot(q_bf, ct_prev.astype(BF16)) + _dot(p.astype(BF16), v)
        den = (s_inter * jnp.sum(q * n_prev, axis=-1, keepdims=True)
               + jnp.sum(p, axis=-1, keepdims=True))
        hh = num / jnp.maximum(jnp.abs(den), jnp.exp(-m_t))

        hh = hh * lax.rsqrt(jnp.mean(hh * hh, axis=-1, keepdims=True) + RMS_EPS)
        gate = o_ref[:, h * A_V:(h + 1) * A_V].astype(F32)
        y_ref[:, h * A_V:(h + 1) * A_V] = (
            hh * nh_ref[:, h * A_V:(h + 1) * A_V] * gate).astype(y_ref.dtype)

        w_col = g_tot - a_col + i_col
        m_loc = jnp.max(w_col, axis=0, keepdims=True)
        ke = k * jnp.exp(w_col - m_loc)
        ct_loc = _dot_tn(ke.astype(BF16), v)
        n_loc = jnp.sum(ke, axis=0, keepdims=True)
        m_new = jnp.maximum(g_tot + m_prev, m_loc)
        s_old = jnp.exp(g_tot + m_prev - m_new)
        s_new = jnp.exp(m_loc - m_new)
        c_scr[h] = s_old * ct_prev + s_new * ct_loc
        n_scr[h] = s_old * n_prev + s_new * n_loc
        m_scr[h] = m_new


def _mlstm(p_f32, p_bf, if_row, conv_w, conv_b, norm_head):
    t = p_f32.shape[0]
    L = MLSTM_CHUNK
    wq = conv_w.shape[1]
    wv = norm_head.shape[1]
    return pl.pallas_call(
        _mlstm_kernel,
        out_shape=jax.ShapeDtypeStruct((t, wv), BF16),
        grid=(t // L,),
        in_specs=[
            pl.BlockSpec((L, wq), lambda i: (i, 0)),
            pl.BlockSpec((L, wv), lambda i: (i, 0)),
            pl.BlockSpec((L, wv), lambda i: (i, 1)),
            pl.BlockSpec((L, LANE), lambda i: (i, wq // LANE)),
            pl.BlockSpec((SUBLANE, L), lambda i: (0, i)),
            pl.BlockSpec((CONV_W, wq), lambda i: (0, 0)),
            pl.BlockSpec((1, wq), lambda i: (0, 0)),
            pl.BlockSpec((1, wv), lambda i: (0, 0)),
        ],
        out_specs=pl.BlockSpec((L, wv), lambda i: (i, 0)),
        scratch_shapes=[
            pltpu.VMEM((L + SUBLANE, wq), F32),
            pltpu.VMEM((A_HEADS, A_QK, A_V), F32),
            pltpu.VMEM((A_HEADS, 1, A_QK), F32),
            pltpu.VMEM((A_HEADS, 1, 1), F32),
        ],
        compiler_params=_cparams(("arbitrary",)),
        name="mlstm",
    )(p_f32, p_bf, p_bf, p_f32, if_row, conv_w, conv_b, norm_head)


def _later(n):
    j = lax.broadcasted_iota(jnp.int32, (n, n), 0)
    s = lax.broadcasted_iota(jnp.int32, (n, n), 1)
    return jnp.where(j > s, 1.0, 0.0).astype(BF16)


def _sb_logs(z):
    ls = jnp.minimum(z, 0.0) - jnp.log(1.0 + jnp.exp(-jnp.abs(z)))
    return ls, ls - z


def _sb_later_sums(lk, later2):
    hi, lo = _split_bf16(lk)
    return _dot(jnp.concatenate([hi, lo], axis=1), later2)


def _sb_tile(q, k, v, carry, later2):
    ls, lk = _sb_logs(_dot_nt(q, k))
    r_in = _sb_later_sums(lk, later2)
    w = jnp.exp(ls + r_in + carry)
    return _dot(w.astype(BF16), v), carry + r_in[:, 0:1] + lk[:, 0:1]


def _sb_kernel(q_ref, kc_ref, vc_ref, kp_ref, vp_ref, k_ref, v_ref, o_ref, acc_scr, carry_scr):
    sub = SB_SUB
    nsub = q_ref.shape[0] // sub
    step = pl.program_id(1)
    later = _later(sub)
    later2 = jnp.concatenate([later, later], axis=0)
    row = lax.broadcasted_iota(jnp.int32, (sub, sub), 0)
    col = lax.broadcasted_iota(jnp.int32, (sub, sub), 1)
    diag_past = col < row
    has_prev = step > 0

    blk = [slice(j * sub, (j + 1) * sub) for j in range(nsub)]
    qs = [q_ref[b, :] for b in blk]
    k_d = [kc_ref[b, :] for b in blk]
    v_d = [vc_ref[b, :] for b in blk]
    k_p = [kp_ref[...]] + k_d[:-1]
    v_p = [vp_ref[...]] + v_d[:-1]
    ls_d, lk_d, ls_p, lk_p = [], [], [], []
    for j in range(nsub):
        ls, lk = _sb_logs(_dot_nt(qs[j], k_d[j]))
        ls_d.append(ls)
        lk_d.append(jnp.where(diag_past, lk, 0.0))
        ls, lk = _sb_logs(_dot_nt(qs[j], k_p[j]))
        ls_p.append(ls)
        lk_p.append(jnp.where(has_prev, lk, 0.0) if j == 0 else lk)
    r_in = _sb_later_sums(jnp.concatenate(lk_d + lk_p, axis=0), later2)
    first_max = []
    for j in range(nsub):
        r_d = r_in[j * sub:(j + 1) * sub]
        r_p = r_in[(nsub + j) * sub:(nsub + j + 1) * sub]
        carry_d = r_d[:, 0:1] + lk_d[j][:, 0:1]
        w_d = jnp.where(diag_past, jnp.exp(ls_d[j] + r_d), 0.0)
        w_p = jnp.exp(ls_p[j] + r_p + carry_d)
        if j == 0:
            w_p = jnp.where(has_prev, w_p, 0.0)
        carry = carry_d + r_p[:, 0:1] + lk_p[j][:, 0:1]
        acc_scr[j] = _dot(w_d.astype(BF16), v_d[j]) + _dot(w_p.astype(BF16), v_p[j])
        carry_scr[j] = carry
        first_max.append(jnp.max(carry))

    for j in range(nsub):
        def cond(state):
            kb, mx = state
            return jnp.logical_and(kb >= 0, mx > -SB_SKIP)

        def body(state, j=j):
            kb, _ = state
            k0 = pl.multiple_of(kb * sub, sub)
            acc, carry = _sb_tile(q_ref[j * sub:(j + 1) * sub, :], k_ref[pl.ds(k0, sub), :],
                                  v_ref[pl.ds(k0, sub), :], carry_scr[j], later2)
            acc_scr[j] += acc
            carry_scr[j] = carry
            return kb - 1, jnp.max(carry)

        lax.while_loop(cond, body, (step * nsub + j - 2, first_max[j]))
        o_ref[j * sub:(j + 1) * sub, :] = acc_scr[j].astype(o_ref.dtype)


def _stick_breaking(qkv, q_col, k_col, v_col):
    t = qkv.shape[0]
    d = B_HEAD_DIM
    bq = min(SB_BQ, t)
    nsub = bq // SB_SUB
    assert t % bq == 0 and bq % SB_SUB == 0

    def cur(col):
        return pl.BlockSpec((bq, d), lambda h, i: (i, col + h))

    def prev(col):
        return pl.BlockSpec((SB_SUB, d), lambda h, i: (jnp.maximum(i * nsub - 1, 0), col + h))

    def full(col):
        return pl.BlockSpec((t, d), lambda h, i: (0, col + h))

    return pl.pallas_call(
        _sb_kernel,
        out_shape=jax.ShapeDtypeStruct((t, B_HEADS * d), BF16),
        grid=(B_HEADS, t // bq),
        in_specs=[cur(q_col), cur(k_col), cur(v_col), prev(k_col), prev(v_col),
                  full(k_col), full(v_col)],
        out_specs=pl.BlockSpec((bq, d), lambda h, i: (i, h)),
        scratch_shapes=[pltpu.VMEM((nsub, SB_SUB, d), F32), pltpu.VMEM((nsub, SB_SUB, 1), F32)],
        compiler_params=_cparams(("parallel", "arbitrary")),
        name="stick_breaking",
    )(qkv, qkv, qkv, qkv, qkv, qkv, qkv)


def _mix_kernel(g1_ref, g2_ref, ya_ref, wa_ref, yb_ref, wb_ref, o_ref):
    pa = _dot(ya_ref[...], wa_ref[...])
    pb = _dot(yb_ref[...], wb_ref[...])
    o_ref[...] = (g1_ref[...].astype(F32) * pa + g2_ref[...].astype(F32) * pb).astype(o_ref.dtype)


def _mix(gates, y_a, w_a, y_b, w_b, *, tm, tn):
    t = y_a.shape[0]
    d = w_a.shape[1]
    nt = d // tn
    return pl.pallas_call(
        _mix_kernel,
        out_shape=jax.ShapeDtypeStruct((t, d), BF16),
        grid=(t // tm, nt),
        in_specs=[
            pl.BlockSpec((tm, tn), lambda i, j: (i, j)),
            pl.BlockSpec((tm, tn), lambda i, j: (i, j + nt)),
            pl.BlockSpec((tm, y_a.shape[1]), lambda i, j: (i, 0)),
            pl.BlockSpec((w_a.shape[0], tn), lambda i, j: (0, j)),
            pl.BlockSpec((tm, y_b.shape[1]), lambda i, j: (i, 0)),
            pl.BlockSpec((w_b.shape[0], tn), lambda i, j: (0, j)),
        ],
        out_specs=pl.BlockSpec((tm, tn), lambda i, j: (i, j)),
        compiler_params=_cparams(("parallel", "arbitrary")),
        name="mix",
    )(gates, gates, y_a, w_a, y_b, w_b)


def _out_kernel(x_ref, mix_ref, wo_ref, g_ref, wrh_ref, wrl_ref, br_ref,
                x1_ref, h2_ref, ri_ref, rg_ref, cnt_ref):
    tm = x_ref.shape[0]
    x1 = x_ref[...] + _dot(mix_ref[...], wo_ref[...])
    x1_ref[...] = x1
    ms = jnp.mean(x1 * x1, axis=-1, keepdims=True)
    h2 = x1 * lax.rsqrt(ms + RMS_EPS) * g_ref[...]
    h_hi, h_lo = _split_bf16(h2)
    h2_ref[...] = h_hi
    w_hi = wrh_ref[...]
    lg = (_dot_nt(w_hi, h_hi) + _dot_nt(w_hi, h_lo) + _dot_nt(wrl_ref[...], h_hi)
          + br_ref[:, 0:1])

    @pl.when(pl.program_id(0) == 0)
    def _():
        cnt_ref[...] = jnp.zeros_like(cnt_ref)

    row = lax.broadcasted_iota(jnp.int32, (N_EXPERTS, tm), 0)
    vals, idxs, hots = [], [], []
    for _ in range(TOP_K):
        mx = jnp.max(lg, axis=0, keepdims=True)
        idx = jnp.min(jnp.where(lg == mx, row, N_EXPERTS), axis=0, keepdims=True)
        hot = row == idx
        vals.append(mx)
        idxs.append(idx)
        hots.append(hot)
        lg = jnp.where(hot, -jnp.inf, lg)

    exps = [jnp.exp(v - vals[0]) for v in vals]
    denom = exps[0]
    for e in exps[1:]:
        denom = denom + e

    chosen = hots[0]
    for hot in hots[1:]:
        chosen = jnp.logical_or(chosen, hot)
    chosen_f = jnp.where(chosen, 1.0, 0.0)
    r = lax.broadcasted_iota(jnp.int32, (tm, tm), 0)
    c = lax.broadcasted_iota(jnp.int32, (tm, tm), 1)
    earlier = jnp.where(r < c, 1.0, 0.0).astype(BF16)
    before = _dot(chosen_f.astype(BF16), earlier) + cnt_ref[:, 0:1]
    cnt_ref[...] += jnp.sum(chosen_f, axis=1, keepdims=True)

    row8 = lax.broadcasted_iota(jnp.int32, (2 * TOP_K, tm), 0)
    ri = jnp.zeros((2 * TOP_K, tm), jnp.int32)
    rg = jnp.zeros((2 * TOP_K, tm), F32)
    for k in range(TOP_K):
        rank = jnp.sum(jnp.where(hots[k], before, 0.0), axis=0, keepdims=True).astype(jnp.int32)
        ri = jnp.where(row8 == k, idxs[k], ri)
        ri = jnp.where(row8 == TOP_K + k, rank, ri)
        rg = jnp.where(row8 == k, exps[k] / denom, rg)
    ri_ref[...] = ri
    rg_ref[...] = rg


def _out_proj(x, mixed, w_o, g, wr_hi, wr_lo, b_r, *, tm):
    t, d = x.shape
    ne = wr_hi.shape[0]
    return pl.pallas_call(
        _out_kernel,
        out_shape=(jax.ShapeDtypeStruct((t, d), F32),
                   jax.ShapeDtypeStruct((t, d), BF16),
                   jax.ShapeDtypeStruct((2 * TOP_K, t), jnp.int32),
                   jax.ShapeDtypeStruct((2 * TOP_K, t), F32),
                   jax.ShapeDtypeStruct((ne, LANE), F32)),
        grid=(t // tm,),
        in_specs=[
            pl.BlockSpec((tm, d), lambda i: (i, 0)),
            pl.BlockSpec((tm, d), lambda i: (i, 0)),
            pl.BlockSpec((d, d), lambda i: (0, 0)),
            pl.BlockSpec((1, d), lambda i: (0, 0)),
            pl.BlockSpec((ne, d), lambda i: (0, 0)),
            pl.BlockSpec((ne, d), lambda i: (0, 0)),
            pl.BlockSpec((ne, LANE), lambda i: (0, 0)),
        ],
        out_specs=(pl.BlockSpec((tm, d), lambda i: (i, 0)),
                   pl.BlockSpec((tm, d), lambda i: (i, 0)),
                   pl.BlockSpec((2 * TOP_K, tm), lambda i: (0, i)),
                   pl.BlockSpec((2 * TOP_K, tm), lambda i: (0, i)),
                   pl.BlockSpec((ne, LANE), lambda i: (0, 0))),
        compiler_params=_cparams(("arbitrary",)),
        name="out_proj",
    )(x, mixed, w_o, g, wr_hi, wr_lo, b_r)


def _expert_changed(be_ref, m):
    return jnp.logical_or(m == 0, be_ref[m] != be_ref[jnp.maximum(m - 1, 0)])


def _moe_up_kernel(be_ref, nu_ref, x_ref, wg_ref, wu_ref, bg_ref, bu_ref, o_ref, wg_scr, wu_scr):
    m = pl.program_id(1)

    @pl.when(m < nu_ref[0])
    def _():
        @pl.when(_expert_changed(be_ref, m))
        def _():
            wg_scr[...] = wg_ref[0].astype(BF16)
            wu_scr[...] = wu_ref[0].astype(BF16)

        x = x_ref[...]
        gate = jnp.minimum(_dot(x, wg_scr[...]) + bg_ref[0], SWIGLU_LIMIT)
        up = jnp.clip(_dot(x, wu_scr[...]) + bu_ref[0], -SWIGLU_LIMIT, SWIGLU_LIMIT)
        o_ref[...] = ((up + 1.0) * gate * _sigmoid(SWIGLU_ALPHA * gate)).astype(o_ref.dtype)

    @pl.when(m >= nu_ref[0])
    def _():
        o_ref[...] = jnp.zeros_like(o_ref)


def _moe_up(block_expert, n_used, x_pad, w_gu, b_gu, *, tf):
    rows, d = x_pad.shape
    nb = rows // MOE_TM
    d_ff = w_gu.shape[2] // 2
    nf = d_ff // tf

    def blk(m, nu):
        return jnp.maximum(jnp.minimum(m, nu[0] - 1), 0)

    return pl.pallas_call(
        _moe_up_kernel,
        out_shape=jax.ShapeDtypeStruct((rows, d_ff), BF16),
        grid_spec=pltpu.PrefetchScalarGridSpec(
            num_scalar_prefetch=2,
            grid=(nf, nb),
            in_specs=[
                pl.BlockSpec((MOE_TM, d), lambda n, m, be, nu: (blk(m, nu), 0)),
                pl.BlockSpec((1, d, tf), lambda n, m, be, nu: (be[blk(m, nu)], 0, n)),
                pl.BlockSpec((1, d, tf), lambda n, m, be, nu: (be[blk(m, nu)], 0, n + nf)),
                pl.BlockSpec((1, 1, tf), lambda n, m, be, nu: (be[blk(m, nu)], 0, n)),
                pl.BlockSpec((1, 1, tf), lambda n, m, be, nu: (be[blk(m, nu)], 0, n + nf)),
            ],
            out_specs=pl.BlockSpec((MOE_TM, tf), lambda n, m, be, nu: (m, n)),
            scratch_shapes=[pltpu.VMEM((d, tf), BF16), pltpu.VMEM((d, tf), BF16)],
        ),
        compiler_params=_cparams(("arbitrary", "arbitrary")),
        name="moe_up",
    )(block_expert, n_used, x_pad, w_gu, w_gu, b_gu, b_gu)


def _moe_down_kernel(be_ref, nu_ref, *refs, chunk_starts):
    a_refs, (wd_ref, bd_ref, o_ref, wd_scr) = refs[:-4], refs[-4:]
    m = pl.program_id(1)

    @pl.when(m < nu_ref[0])
    def _():
        @pl.when(_expert_changed(be_ref, m))
        def _():
            wd_scr[...] = wd_ref[0].astype(BF16)

        for c, a_ref in enumerate(a_refs):
            @pl.when(jnp.logical_and(m >= chunk_starts[c], m < chunk_starts[c + 1]))
            def _(a_ref=a_ref):
                o_ref[...] = (_dot(a_ref[...], wd_scr[...]) + bd_ref[0]).astype(o_ref.dtype)

    @pl.when(m >= nu_ref[0])
    def _():
        o_ref[...] = jnp.zeros_like(o_ref)


def _moe_down(block_expert, n_used, acts, w_d, b_d, *, tn):
    d_ff = acts[0].shape[1]
    chunk_starts = [0]
    for a in acts:
        chunk_starts.append(chunk_starts[-1] + a.shape[0] // MOE_TM)
    nb = chunk_starts[-1]
    d = w_d.shape[2]

    def blk(m, nu):
        return jnp.maximum(jnp.minimum(m, nu[0] - 1), 0)

    def act_spec(c):
        lo, n_c = chunk_starts[c], chunk_starts[c + 1] - chunk_starts[c]
        return pl.BlockSpec((MOE_TM, d_ff),
                            lambda n, m, be, nu: (jnp.clip(blk(m, nu) - lo, 0, n_c - 1), 0))

    return pl.pallas_call(
        functools.partial(_moe_down_kernel, chunk_starts=tuple(chunk_starts)),
        out_shape=jax.ShapeDtypeStruct((nb * MOE_TM, d), BF16),
        grid_spec=pltpu.PrefetchScalarGridSpec(
            num_scalar_prefetch=2,
            grid=(d // tn, nb),
            in_specs=[act_spec(c) for c in range(len(acts))] + [
                pl.BlockSpec((1, d_ff, tn), lambda n, m, be, nu: (be[blk(m, nu)], 0, n)),
                pl.BlockSpec((1, 1, tn), lambda n, m, be, nu: (be[blk(m, nu)], 0, n)),
            ],
            out_specs=pl.BlockSpec((MOE_TM, tn), lambda n, m, be, nu: (m, n)),
            scratch_shapes=[pltpu.VMEM((d_ff, tn), BF16)],
        ),
        compiler_params=_cparams(("arbitrary", "arbitrary")),
        name="moe_down",
    )(block_expert, n_used, *acts, w_d, b_d)


def _combine_kernel(x_ref, y0_ref, y1_ref, y2_ref, y3_ref, rg_ref, g_ref, o_ref, *, normalize):
    x = x_ref[...]
    rg = rg_ref[...]
    for k, y_ref in enumerate((y0_ref, y1_ref, y2_ref, y3_ref)):
        x = x + rg[:, k:k + 1] * y_ref[...].astype(F32)
    if normalize:
        ms = jnp.mean(x * x, axis=-1, keepdims=True)
        x = x * lax.rsqrt(ms + RMS_EPS) * g_ref[...]
    o_ref[...] = x


def _combine(x_buf, y_rows, rg, g, *, chunk, n_chunk_tokens, normalize, tm):
    t, d = x_buf.shape
    nblk = n_chunk_tokens // tm
    x_spec = pl.BlockSpec((tm, d), lambda i: (chunk * nblk + i, 0))

    def y_spec(k):
        return pl.BlockSpec((tm, d), lambda i: (k * nblk + i, 0))

    return pl.pallas_call(
        functools.partial(_combine_kernel, normalize=normalize),
        out_shape=jax.ShapeDtypeStruct((t, d), F32),
        grid=(nblk,),
        in_specs=[x_spec] + [y_spec(k) for k in range(TOP_K)] + [
            pl.BlockSpec((tm, rg.shape[1]), lambda i: (chunk * nblk + i, 0)),
            pl.BlockSpec((1, d), lambda i: (0, 0)),
        ],
        out_specs=x_spec,
        input_output_aliases={0: 0},
        compiler_params=_cparams(("parallel",)),
        name="combine_norm",
    )(x_buf, y_rows, y_rows, y_rows, y_rows, rg, g)


def _route(ri, counts, t):
    p_n = t * TOP_K
    experts = ri[:TOP_K]
    rank = ri[TOP_K:]
    counts = counts[:, 0].astype(jnp.int32)
    padded = (counts + MOE_TM - 1) // MOE_TM * MOE_TM
    pad_end = jnp.cumsum(padded)
    pad_start = pad_end - padded
    e_ids = jnp.arange(N_EXPERTS, dtype=jnp.int32)[:, None, None]
    dest = rank + jnp.sum(jnp.where(experts[None] == e_ids, pad_start[:, None, None], 0), axis=0)
    nb = -(-p_n // MOE_TM) + N_EXPERTS
    src_tok = (jnp.arange(nb * MOE_TM, dtype=jnp.int32) % t).at[dest.reshape(p_n)].set(
        jnp.tile(jnp.arange(t, dtype=jnp.int32), TOP_K), mode="promise_in_bounds",
        unique_indices=True)
    block_start = jnp.arange(nb, dtype=jnp.int32) * MOE_TM
    block_expert = jnp.minimum(
        jnp.sum((block_start[:, None] >= pad_end[None, :]).astype(jnp.int32), axis=1),
        N_EXPERTS - 1)
    n_used = (pad_end[-1] // MOE_TM).astype(jnp.int32).reshape(1)
    return dest, src_tok, block_expert, n_used


def kernel(x, norm_mix, w_in, conv_w, conv_b, b_gates_if, norm_head, w_proj_a, w_proj_b,
           w_merge_gate, b_merge_gate, w_out, norm_ffn, w_router, b_router, w_gate_up,
           b_gate_up, w_down, b_down, norm_final):
    bn, s, d = x.shape
    assert bn == 1
    depth = norm_mix.shape[0]
    xt = x.reshape(s, d)
    tm = min(1024, s)
    aqk = A_HEADS * A_QK
    av = A_HEADS * A_V
    bw = B_HEADS * B_HEAD_DIM

    for l in range(depth):
        wl = w_in[l]
        c0 = 2 * aqk
        c1 = c0 + 2 * av
        c2 = c1 + 2 * A_HEADS
        w_f32 = jnp.concatenate(
            [wl[:, :c0], wl[:, c1:c2], jnp.zeros((d, LANE - 2 * A_HEADS), F32)], axis=1).astype(BF16)
        b_f32 = jnp.concatenate(
            [jnp.zeros((c0,), F32), b_gates_if[l], jnp.zeros((LANE - 2 * A_HEADS,), F32)])[None, :]
        w_bf = jnp.concatenate([wl[:, c0:c1], wl[:, c2:c2 + bw] * (B_HEAD_DIM ** -0.5), wl[:, c2 + bw:]],
                               axis=1).astype(BF16)
        g_mix = norm_mix[l][None, :]

        p_f32, h1, if_row = _norm_proj(xt, g_mix, w_f32, b_f32, tm=tm)
        tn = 1024
        o_tile = av // tn
        p_bf = _proj(h1, w_bf, jnp.zeros((1, w_bf.shape[1]), F32),
                     sigmoid_tiles=(o_tile, 2 * o_tile), tm=tm, tn=tn)
        n_gate_tiles = w_merge_gate.shape[2] // tn
        gates = _proj(h1, w_merge_gate[l].astype(BF16), b_merge_gate[l][None, :],
                      sigmoid_tiles=(0, n_gate_tiles), tm=tm, tn=tn)

        y_a = _mlstm(p_f32, p_bf, if_row, conv_w[l], conv_b[l][None, :], norm_head[l][None, :])
        qb = 2 * av // B_HEAD_DIM
        y_b = _stick_breaking(p_bf, qb, qb + B_HEADS, qb + 2 * B_HEADS)

        mixed = _mix(gates, y_a, w_proj_a[l].astype(BF16), y_b, w_proj_b[l].astype(BF16),
                     tm=tm, tn=512)
        wr_hi, wr_lo = _split_bf16(w_router[l].T)
        b_r = jnp.broadcast_to(b_router[l][:, None], (N_EXPERTS, LANE))
        x1, h2, ri, rg, counts = _out_proj(xt, mixed, w_out[l].astype(BF16), norm_ffn[l][None, :],
                                           wr_hi, wr_lo, b_r, tm=min(512, s))

        dest, src_tok, block_expert, n_used = _route(ri, counts, s)
        nb = block_expert.shape[0]
        bounds = [0, nb // MOE_FIRST_CHUNK_DIV, nb]
        acts = []
        for lo, hi in zip(bounds[:-1], bounds[1:]):
            x_pad = h2.at[src_tok[lo * MOE_TM:hi * MOE_TM]].get(mode="promise_in_bounds")
            acts.append(_moe_up(block_expert[lo:hi], jnp.clip(n_used - lo, 0, hi - lo),
                                x_pad, w_gate_up[l], b_gate_up[l][:, None, :], tf=1024))
        y_pad = _moe_down(block_expert, n_used, acts, w_down[l], b_down[l][:, None, :], tn=2048)
        nct = s // COMBINE_CHUNKS
        rg_col = rg.T
        xt = x1
        for j in range(COMBINE_CHUNKS):
            idx = dest[:, j * nct:(j + 1) * nct].reshape(TOP_K * nct)
            y_rows = y_pad.at[idx].get(mode="promise_in_bounds")
            xt = _combine(xt, y_rows, rg_col, norm_final[None, :], chunk=j, n_chunk_tokens=nct,
                          normalize=(l + 1 == depth), tm=min(512, nct))
    return xt.reshape(bn, s, d)
```

```python
import functools

import jax
import jax.numpy as jnp
from jax import lax
from jax.experimental import pallas as pl
from jax.experimental.pallas import tpu as pltpu

F32 = jnp.float32
BF16 = jnp.bfloat16

RMS_EPS = 1e-5
A_HEADS = 4
A_QK = 128
A_V = 256
CONV_W = 4
B_HEADS = 8
B_HEAD_DIM = 128
N_EXPERTS = 32
TOP_K = 4
SWIGLU_LIMIT = 7.0
SWIGLU_ALPHA = 1.702

LANE = 128
SUBLANE = 8
VMEM_LIMIT = 60 * 1024 * 1024

MLSTM_CHUNK = 128
SB_BQ = 1024
SB_SUB = 128
SB_SKIP = 88.0
MOE_TM = 512
MOE_FIRST_CHUNK_DIV = 5
COMBINE_CHUNKS = 4


def _cparams(sem):
    return pltpu.CompilerParams(dimension_semantics=sem, vmem_limit_bytes=VMEM_LIMIT)


def _log_sigmoid(z):
    return jnp.minimum(z, 0.0) - jnp.log1p(jnp.exp(-jnp.abs(z)))


def _sigmoid(z):
    return 1.0 / (1.0 + jnp.exp(-z))


def _split_bf16(x):
    hi = x.astype(BF16)
    lo = (x - hi.astype(F32)).astype(BF16)
    return hi, lo


def _dot(a, b):
    return jnp.dot(a, b, preferred_element_type=F32)


def _dot_nt(a, b):
    return lax.dot_general(a, b, (((1,), (1,)), ((), ())), preferred_element_type=F32)


def _dot_tn(a, b):
    return lax.dot_general(a, b, (((0,), (0,)), ((), ())), preferred_element_type=F32)


def _norm_proj_kernel(x_ref, g_ref, w_ref, b_ref, o_ref, h_ref, gt_ref):
    x = x_ref[...]
    ms = jnp.mean(x * x, axis=-1, keepdims=True)
    h = (x * lax.rsqrt(ms + RMS_EPS) * g_ref[...]).astype(BF16)
    h_ref[...] = h
    acc = _dot(h, w_ref[...]) + b_ref[...]
    o_ref[...] = acc
    gt_ref[...] = acc[:, acc.shape[1] - LANE:].T[:SUBLANE, :]


def _norm_proj(x, g, w, b, *, tm):
    t, d = x.shape
    n = w.shape[1]
    return pl.pallas_call(
        _norm_proj_kernel,
        out_shape=(jax.ShapeDtypeStruct((t, n), F32),
                   jax.ShapeDtypeStruct((t, d), BF16),
                   jax.ShapeDtypeStruct((SUBLANE, t), F32)),
        grid=(t // tm,),
        in_specs=[
            pl.BlockSpec((tm, d), lambda i: (i, 0)),
            pl.BlockSpec((1, d), lambda i: (0, 0)),
            pl.BlockSpec((d, n), lambda i: (0, 0)),
            pl.BlockSpec((1, n), lambda i: (0, 0)),
        ],
        out_specs=(pl.BlockSpec((tm, n), lambda i: (i, 0)),
                   pl.BlockSpec((tm, d), lambda i: (i, 0)),
                   pl.BlockSpec((SUBLANE, tm), lambda i: (0, i))),
        compiler_params=_cparams(("parallel",)),
        name="norm_proj",
    )(x, g, w, b)


def _proj_kernel(h_ref, w_ref, b_ref, o_ref, *, sigmoid_tiles, n_tiles):
    acc = _dot(h_ref[...], w_ref[...]) + b_ref[...]
    lo, hi = sigmoid_tiles
    if lo == 0 and hi == n_tiles:
        o_ref[...] = _sigmoid(acc).astype(o_ref.dtype)
    elif lo == hi:
        o_ref[...] = acc.astype(o_ref.dtype)
    else:
        j = pl.program_id(1)
        gated = jnp.logical_and(j >= lo, j < hi)

        @pl.when(gated)
        def _():
            o_ref[...] = _sigmoid(acc).astype(o_ref.dtype)

        @pl.when(jnp.logical_not(gated))
        def _():
            o_ref[...] = acc.astype(o_ref.dtype)


def _proj(h, w, b, *, sigmoid_tiles, tm, tn):
    t, d = h.shape
    n = w.shape[1]
    return pl.pallas_call(
        functools.partial(_proj_kernel, sigmoid_tiles=sigmoid_tiles, n_tiles=n // tn),
        out_shape=jax.ShapeDtypeStruct((t, n), BF16),
        grid=(t // tm, n // tn),
        in_specs=[
            pl.BlockSpec((tm, d), lambda i, j: (i, 0)),
            pl.BlockSpec((d, tn), lambda i, j: (0, j)),
            pl.BlockSpec((1, tn), lambda i, j: (0, j)),
        ],
        out_specs=pl.BlockSpec((tm, tn), lambda i, j: (i, j)),
        compiler_params=_cparams(("parallel", "arbitrary")),
        name="proj",
    )(h, w, b)


def _mlstm_kernel(qk_ref, v_ref, o_ref, ifc_ref, ifr_ref, cw_ref, cb_ref, nh_ref,
                  y_ref, ext_scr, c_scr, n_scr, m_scr):
    L = MLSTM_CHUNK
    step = pl.program_id(0)

    @pl.when(step == 0)
    def _():
        ext_scr[0:SUBLANE, :] = jnp.zeros((SUBLANE, ext_scr.shape[1]), F32)
        c_scr[...] = jnp.zeros_like(c_scr)
        n_scr[...] = jnp.zeros_like(n_scr)
        m_scr[...] = jnp.zeros_like(m_scr)

    raw = qk_ref[...]
    ext_scr[SUBLANE:SUBLANE + L, :] = raw
    conv = cb_ref[...]
    for j in range(CONV_W):
        off = SUBLANE - (CONV_W - 1) + j
        conv = conv + ext_scr[off:off + L, :] * cw_ref[j:j + 1, :]
    ext_scr[0:SUBLANE, :] = raw[L - SUBLANE:L, :]
    qk = conv * _sigmoid(conv)

    row = lax.broadcasted_iota(jnp.int32, (L, L), 0)
    col = lax.broadcasted_iota(jnp.int32, (L, L), 1)
    causal = row >= col
    tri_incl = jnp.where(causal, 1.0, 0.0).astype(BF16)
    tri_incl_t = jnp.where(col >= row, 1.0, 0.0).astype(BF16)

    hi, lo = _split_bf16(_log_sigmoid(ifc_ref[...]))
    a_cols = _dot(tri_incl, hi) + _dot(tri_incl, lo)
    hi, lo = _split_bf16(_log_sigmoid(ifr_ref[...]))
    a_rows = _dot(hi, tri_incl_t) + _dot(lo, tri_incl_t)

    for h in range(A_HEADS):
        q = qk[:, h * A_QK:(h + 1) * A_QK]
        k = qk[:, A_HEADS * A_QK + h * A_QK:A_HEADS * A_QK + (h + 1) * A_QK] * (A_QK ** -0.5)
        v = v_ref[:, h * A_V:(h + 1) * A_V]
        q_bf = q.astype(BF16)
        k_bf = k.astype(BF16)

        i_col = ifc_ref[:, h:h + 1]
        i_row = ifr_ref[h:h + 1, :]
        a_col = a_cols[:, A_HEADS + h:A_HEADS + h + 1]
        a_row = a_rows[A_HEADS + h:A_HEADS + h + 1, :]
        g_tot = a_col[L - 1:L, :]

        m_prev = m_scr[h]
        n_prev = n_scr[h]
        ct_prev = c_scr[h]

        m_inter = a_col + m_prev
        d_log = jnp.where(causal, a_col - a_row + i_row, -jnp.inf)
        m_t = jnp.maximum(m_inter, jnp.max(d_log, axis=-1, keepdims=True))
        p = jnp.exp(d_log - m_t) * _dot_nt(q_bf, k_bf)
        s_inter = jnp.exp(m_inter - m_t)
        num = s_inter * _dot(q_bf, ct_prev.astype(BF16)) + _dot(p.astype(BF16), v)
        den = (s_inter * jnp.sum(q * n_prev, axis=-1, keepdims=True)
               + jnp.sum(p, axis=-1, keepdims=True))
        hh = num / jnp.maximum(jnp.abs(den), jnp.exp(-m_t))

        hh = hh * lax.rsqrt(jnp.mean(hh * hh, axis=-1, keepdims=True) + RMS_EPS)
        gate = o_ref[:, h * A_V:(h + 1) * A_V].astype(F32)
        y_ref[:, h * A_V:(h + 1) * A_V] = (
            hh * nh_ref[:, h * A_V:(h + 1) * A_V] * gate).astype(y_ref.dtype)

        w_col = g_tot - a_col + i_col
        m_loc = jnp.max(w_col, axis=0, keepdims=True)
        ke = k * jnp.exp(w_col - m_loc)
        ct_loc = _dot_tn(ke.astype(BF16), v)
        n_loc = jnp.sum(ke, axis=0, keepdims=True)
        m_new = jnp.maximum(g_tot + m_prev, m_loc)
        s_old = jnp.exp(g_tot + m_prev - m_new)
        s_new = jnp.exp(m_loc - m_new)
        c_scr[h] = s_old * ct_prev + s_new * ct_loc
        n_scr[h] = s_old * n_prev + s_new * n_loc
        m_scr[h] = m_new


def _mlstm(p_f32, p_bf, if_row, conv_w, conv_b, norm_head):
    t = p_f32.shape[0]
    L = MLSTM_CHUNK
    wq = conv_w.shape[1]
    wv = norm_head.shape[1]
    return pl.pallas_call(
        _mlstm_kernel,
        out_shape=jax.ShapeDtypeStruct((t, wv), BF16),
        grid=(t // L,),
        in_specs=[
            pl.BlockSpec((L, wq), lambda i: (i, 0)),
            pl.BlockSpec((L, wv), lambda i: (i, 0)),
            pl.BlockSpec((L, wv), lambda i: (i, 1)),
            pl.BlockSpec((L, LANE), lambda i: (i, wq // LANE)),
            pl.BlockSpec((SUBLANE, L), lambda i: (0, i)),
            pl.BlockSpec((CONV_W, wq), lambda i: (0, 0)),
            pl.BlockSpec((1, wq), lambda i: (0, 0)),
            pl.BlockSpec((1, wv), lambda i: (0, 0)),
        ],
        out_specs=pl.BlockSpec((L, wv), lambda i: (i, 0)),
        scratch_shapes=[
            pltpu.VMEM((L + SUBLANE, wq), F32),
            pltpu.VMEM((A_HEADS, A_QK, A_V), F32),
            pltpu.VMEM((A_HEADS, 1, A_QK), F32),
            pltpu.VMEM((A_HEADS, 1, 1), F32),
        ],
        compiler_params=_cparams(("arbitrary",)),
        name="mlstm",
    )(p_f32, p_bf, p_bf, p_f32, if_row, conv_w, conv_b, norm_head)


def _later(n):
    j = lax.broadcasted_iota(jnp.int32, (n, n), 0)
    s = lax.broadcasted_iota(jnp.int32, (n, n), 1)
    return jnp.where(j > s, 1.0, 0.0).astype(BF16)


def _sb_logs(z):
    ls = jnp.minimum(z, 0.0) - jnp.log(1.0 + jnp.exp(-jnp.abs(z)))
    return ls, ls - z


def _sb_later_sums(lk, later2):
    hi, lo = _split_bf16(lk)
    return _dot(jnp.concatenate([hi, lo], axis=1), later2)


def _sb_tile(q, k, v, carry, later2):
    ls, lk = _sb_logs(_dot_nt(q, k))
    r_in = _sb_later_sums(lk, later2)
    w = jnp.exp(ls + r_in + carry)
    return _dot(w.astype(BF16), v), carry + r_in[:, 0:1] + lk[:, 0:1]


def _sb_kernel(q_ref, kc_ref, vc_ref, kp_ref, vp_ref, k_ref, v_ref, o_ref, acc_scr, carry_scr):
    sub = SB_SUB
    nsub = q_ref.shape[0] // sub
    step = pl.program_id(1)
    later = _later(sub)
    later2 = jnp.concatenate([later, later], axis=0)
    row = lax.broadcasted_iota(jnp.int32, (sub, sub), 0)
    col = lax.broadcasted_iota(jnp.int32, (sub, sub), 1)
    diag_past = col < row
    has_prev = step > 0

    blk = [slice(j * sub, (j + 1) * sub) for j in range(nsub)]
    qs = [q_ref[b, :] for b in blk]
    k_d = [kc_ref[b, :] for b in blk]
    v_d = [vc_ref[b, :] for b in blk]
    k_p = [kp_ref[...]] + k_d[:-1]
    v_p = [vp_ref[...]] + v_d[:-1]
    ls_d, lk_d, ls_p, lk_p = [], [], [], []
    for j in range(nsub):
        ls, lk = _sb_logs(_dot_nt(qs[j], k_d[j]))
        ls_d.append(ls)
        lk_d.append(jnp.where(diag_past, lk, 0.0))
        ls, lk = _sb_logs(_dot_nt(qs[j], k_p[j]))
        ls_p.append(ls)
        lk_p.append(jnp.where(has_prev, lk, 0.0) if j == 0 else lk)
    r_in = _sb_later_sums(jnp.concatenate(lk_d + lk_p, axis=0), later2)
    first_max = []
    for j in range(nsub):
        r_d = r_in[j * sub:(j + 1) * sub]
        r_p = r_in[(nsub + j) * sub:(nsub + j + 1) * sub]
        carry_d = r_d[:, 0:1] + lk_d[j][:, 0:1]
        w_d = jnp.where(diag_past, jnp.exp(ls_d[j] + r_d), 0.0)
        w_p = jnp.exp(ls_p[j] + r_p + carry_d)
        if j == 0:
            w_p = jnp.where(has_prev, w_p, 0.0)
        carry = carry_d + r_p[:, 0:1] + lk_p[j][:, 0:1]
        acc_scr[j] = _dot(w_d.astype(BF16), v_d[j]) + _dot(w_p.astype(BF16), v_p[j])
        carry_scr[j] = carry
        first_max.append(jnp.max(carry))

    for j in range(nsub):
        def cond(state):
            kb, mx = state
            return jnp.logical_and(kb >= 0, mx > -SB_SKIP)

        def body(state, j=j):
            kb, _ = state
            k0 = pl.multiple_of(kb * sub, sub)
            acc, carry = _sb_tile(q_ref[j * sub:(j + 1) * sub, :], k_ref[pl.ds(k0, sub), :],
                                  v_ref[pl.ds(k0, sub), :], carry_scr[j], later2)
            acc_scr[j] += acc
            carry_scr[j] = carry
            return kb - 1, jnp.max(carry)

        lax.while_loop(cond, body, (step * nsub + j - 2, first_max[j]))
        o_ref[j * sub:(j + 1) * sub, :] = acc_scr[j].astype(o_ref.dtype)


def _stick_breaking(qkv, q_col, k_col, v_col):
    t = qkv.shape[0]
    d = B_HEAD_DIM
    bq = min(SB_BQ, t)
    nsub = bq // SB_SUB
    assert t % bq == 0 and bq % SB_SUB == 0

    def cur(col):
        return pl.BlockSpec((bq, d), lambda h, i: (i, col + h))

    def prev(col):
        return pl.BlockSpec((SB_SUB, d), lambda h, i: (jnp.maximum(i * nsub - 1, 0), col + h))

    def full(col):
        return pl.BlockSpec((t, d), lambda h, i: (0, col + h))

    return pl.pallas_call(
        _sb_kernel,
        out_shape=jax.ShapeDtypeStruct((t, B_HEADS * d), BF16),
        grid=(B_HEADS, t // bq),
        in_specs=[cur(q_col), cur(k_col), cur(v_col), prev(k_col), prev(v_col),
                  full(k_col), full(v_col)],
        out_specs=pl.BlockSpec((bq, d), lambda h, i: (i, h)),
        scratch_shapes=[pltpu.VMEM((nsub, SB_SUB, d), F32), pltpu.VMEM((nsub, SB_SUB, 1), F32)],
        compiler_params=_cparams(("parallel", "arbitrary")),
        name="stick_breaking",
    )(qkv, qkv, qkv, qkv, qkv, qkv, qkv)


def _mix_kernel(g1_ref, g2_ref, ya_ref, wa_ref, yb_ref, wb_ref, o_ref):
    pa = _dot(ya_ref[...], wa_ref[...])
    pb = _dot(yb_ref[...], wb_ref[...])
    o_ref[...] = (g1_ref[...].astype(F32) * pa + g2_ref[...].astype(F32) * pb).astype(o_ref.dtype)


def _mix(gates, y_a, w_a, y_b, w_b, *, tm, tn):
    t = y_a.shape[0]
    d = w_a.shape[1]
    nt = d // tn
    return pl.pallas_call(
        _mix_kernel,
        out_shape=jax.ShapeDtypeStruct((t, d), BF16),
        grid=(t // tm, nt),
        in_specs=[
            pl.BlockSpec((tm, tn), lambda i, j: (i, j)),
            pl.BlockSpec((tm, tn), lambda i, j: (i, j + nt)),
            pl.BlockSpec((tm, y_a.shape[1]), lambda i, j: (i, 0)),
            pl.BlockSpec((w_a.shape[0], tn), lambda i, j: (0, j)),
            pl.BlockSpec((tm, y_b.shape[1]), lambda i, j: (i, 0)),
            pl.BlockSpec((w_b.shape[0], tn), lambda i, j: (0, j)),
        ],
        out_specs=pl.BlockSpec((tm, tn), lambda i, j: (i, j)),
        compiler_params=_cparams(("parallel", "arbitrary")),
        name="mix",
    )(gates, gates, y_a, w_a, y_b, w_b)


def _out_kernel(x_ref, mix_ref, wo_ref, g_ref, wrh_ref, wrl_ref, br_ref,
                x1_ref, h2_ref, ri_ref, rg_ref, cnt_ref):
    tm = x_ref.shape[0]
    x1 = x_ref[...] + _dot(mix_ref[...], wo_ref[...])
    x1_ref[...] = x1
    ms = jnp.mean(x1 * x1, axis=-1, keepdims=True)
    h2 = x1 * lax.rsqrt(ms + RMS_EPS) * g_ref[...]
    h_hi, h_lo = _split_bf16(h2)
    h2_ref[...] = h_hi
    w_hi = wrh_ref[...]
    lg = (_dot_nt(w_hi, h_hi) + _dot_nt(w_hi, h_lo) + _dot_nt(wrl_ref[...], h_hi)
          + br_ref[:, 0:1])

    @pl.when(pl.program_id(0) == 0)
    def _():
        cnt_ref[...] = jnp.zeros_like(cnt_ref)

    row = lax.broadcasted_iota(jnp.int32, (N_EXPERTS, tm), 0)
    vals, idxs, hots = [], [], []
    for _ in range(TOP_K):
        mx = jnp.max(lg, axis=0, keepdims=True)
        idx = jnp.min(jnp.where(lg == mx, row, N_EXPERTS), axis=0, keepdims=True)
        hot = row == idx
        vals.append(mx)
        idxs.append(idx)
        hots.append(hot)
        lg = jnp.where(hot, -jnp.inf, lg)

    exps = [jnp.exp(v - vals[0]) for v in vals]
    denom = exps[0]
    for e in exps[1:]:
        denom = denom + e

    chosen = hots[0]
    for hot in hots[1:]:
        chosen = jnp.logical_or(chosen, hot)
    chosen_f = jnp.where(chosen, 1.0, 0.0)
    r = lax.broadcasted_iota(jnp.int32, (tm, tm), 0)
    c = lax.broadcasted_iota(jnp.int32, (tm, tm), 1)
    earlier = jnp.where(r < c, 1.0, 0.0).astype(BF16)
    before = _dot(chosen_f.astype(BF16), earlier) + cnt_ref[:, 0:1]
    cnt_ref[...] += jnp.sum(chosen_f, axis=1, keepdims=True)

    row8 = lax.broadcasted_iota(jnp.int32, (2 * TOP_K, tm), 0)
    ri = jnp.zeros((2 * TOP_K, tm), jnp.int32)
    rg = jnp.zeros((2 * TOP_K, tm), F32)
    for k in range(TOP_K):
        rank = jnp.sum(jnp.where(hots[k], before, 0.0), axis=0, keepdims=True).astype(jnp.int32)
        ri = jnp.where(row8 == k, idxs[k], ri)
        ri = jnp.where(row8 == TOP_K + k, rank, ri)
        rg = jnp.where(row8 == k, exps[k] / denom, rg)
    ri_ref[...] = ri
    rg_ref[...] = jnp.concatenate([rg, jnp.zeros((LANE - 2 * TOP_K, tm), F32)], axis=0).T


def _out_proj(x, mixed, w_o, g, wr_hi, wr_lo, b_r, *, tm):
    t, d = x.shape
    ne = wr_hi.shape[0]
    return pl.pallas_call(
        _out_kernel,
        out_shape=(jax.ShapeDtypeStruct((t, d), F32),
                   jax.ShapeDtypeStruct((t, d), BF16),
                   jax.ShapeDtypeStruct((2 * TOP_K, t), jnp.int32),
                   jax.ShapeDtypeStruct((t, LANE), F32),
                   jax.ShapeDtypeStruct((ne, LANE), F32)),
        grid=(t // tm,),
        in_specs=[
            pl.BlockSpec((tm, d), lambda i: (i, 0)),
            pl.BlockSpec((tm, d), lambda i: (i, 0)),
            pl.BlockSpec((d, d), lambda i: (0, 0)),
            pl.BlockSpec((1, d), lambda i: (0, 0)),
            pl.BlockSpec((ne, d), lambda i: (0, 0)),
            pl.BlockSpec((ne, d), lambda i: (0, 0)),
            pl.BlockSpec((ne, LANE), lambda i: (0, 0)),
        ],
        out_specs=(pl.BlockSpec((tm, d), lambda i: (i, 0)),
                   pl.BlockSpec((tm, d), lambda i: (i, 0)),
                   pl.BlockSpec((2 * TOP_K, tm), lambda i: (0, i)),
                   pl.BlockSpec((tm, LANE), lambda i: (i, 0)),
                   pl.BlockSpec((ne, LANE), lambda i: (0, 0))),
        compiler_params=_cparams(("arbitrary",)),
        name="out_proj",
    )(x, mixed, w_o, g, wr_hi, wr_lo, b_r)


def _expert_changed(be_ref, m):
    return jnp.logical_or(m == 0, be_ref[m] != be_ref[jnp.maximum(m - 1, 0)])


def _next_run_expert(block_expert, n_used):
    nb = block_expert.shape[0]
    idx = jnp.arange(nb, dtype=jnp.int32)
    starts = jnp.logical_or(idx == 0, block_expert != jnp.roll(block_expert, 1))
    cand = jnp.where(jnp.logical_and(starts, idx < n_used[0]), idx, nb)
    later_start = jnp.concatenate([lax.cummin(cand, axis=0, reverse=True)[1:],
                                   jnp.full((1,), nb, jnp.int32)])
    at_start = jnp.sum(jnp.where(idx[None, :] == later_start[:, None], block_expert[None, :], 0), axis=1)
    return jnp.where(later_start < nb, at_start, -1).astype(jnp.int32)


def _moe_up_kernel(be_ref, nu_ref, nx_ref, x_ref, w_hbm, bg_ref, bu_ref, o_ref,
                   stage, wg_scr, wu_scr, sem):
    n = pl.program_id(0)
    m = pl.program_id(1)
    nf = pl.num_programs(0)
    tf = wg_scr.shape[1]

    def fetch(e, nn, part):
        col = pl.multiple_of((nn + part * nf) * tf, tf)
        return pltpu.make_async_copy(w_hbm.at[e, :, pl.ds(col, tf)], stage.at[part], sem.at[part])

    def start(e, nn):
        fetch(e, nn, 0).start()
        fetch(e, nn, 1).start()

    @pl.when(jnp.logical_and(jnp.logical_and(n == 0, m == 0), nu_ref[0] > 0))
    def _():
        start(be_ref[0], 0)

    @pl.when(m < nu_ref[0])
    def _():
        @pl.when(_expert_changed(be_ref, m))
        def _():
            fetch(be_ref[m], n, 0).wait()
            fetch(be_ref[m], n, 1).wait()
            wg_scr[...] = stage[0].astype(BF16)
            wu_scr[...] = stage[1].astype(BF16)
            nxt = nx_ref[m]

            @pl.when(nxt >= 0)
            def _():
                start(nxt, n)

            @pl.when(jnp.logical_and(nxt < 0, n + 1 < nf))
            def _():
                start(be_ref[0], n + 1)

        x = x_ref[...]
        gate = jnp.minimum(_dot(x, wg_scr[...]) + bg_ref[0], SWIGLU_LIMIT)
        up = jnp.clip(_dot(x, wu_scr[...]) + bu_ref[0], -SWIGLU_LIMIT, SWIGLU_LIMIT)
        o_ref[...] = ((up + 1.0) * gate * _sigmoid(SWIGLU_ALPHA * gate)).astype(o_ref.dtype)

    @pl.when(m >= nu_ref[0])
    def _():
        o_ref[...] = jnp.zeros_like(o_ref)


def _moe_up(block_expert, n_used, x_pad, w_gu, b_gu, *, tf):
    rows, d = x_pad.shape
    nb = rows // MOE_TM
    d_ff = w_gu.shape[2] // 2
    nf = d_ff // tf

    def blk(m, nu):
        return jnp.maximum(jnp.minimum(m, nu[0] - 1), 0)

    return pl.pallas_call(
        _moe_up_kernel,
        out_shape=jax.ShapeDtypeStruct((rows, d_ff), BF16),
        grid_spec=pltpu.PrefetchScalarGridSpec(
            num_scalar_prefetch=3,
            grid=(nf, nb),
            in_specs=[
                pl.BlockSpec((MOE_TM, d), lambda n, m, be, nu, nx: (blk(m, nu), 0)),
                pl.BlockSpec(memory_space=pl.ANY),
                pl.BlockSpec((1, 1, tf), lambda n, m, be, nu, nx: (be[blk(m, nu)], 0, n)),
                pl.BlockSpec((1, 1, tf), lambda n, m, be, nu, nx: (be[blk(m, nu)], 0, n + nf)),
            ],
            out_specs=pl.BlockSpec((MOE_TM, tf), lambda n, m, be, nu, nx: (m, n)),
            scratch_shapes=[pltpu.VMEM((2, d, tf), F32), pltpu.VMEM((d, tf), BF16),
                            pltpu.VMEM((d, tf), BF16), pltpu.SemaphoreType.DMA((2,))],
        ),
        compiler_params=_cparams(("arbitrary", "arbitrary")),
        name="moe_up",
    )(block_expert, n_used, _next_run_expert(block_expert, n_used), x_pad, w_gu, b_gu, b_gu)


def _moe_down_kernel(be_ref, nu_ref, nx_ref, *refs, chunk_starts):
    a_refs, (w_hbm, bd_ref, o_ref, stage, wd_scr, sem) = refs[:-6], refs[-6:]
    m = pl.program_id(0)

    def fetch(e):
        return pltpu.make_async_copy(w_hbm.at[e], stage, sem.at[0])

    @pl.when(jnp.logical_and(m == 0, nu_ref[0] > 0))
    def _():
        fetch(be_ref[0]).start()

    @pl.when(m < nu_ref[0])
    def _():
        @pl.when(_expert_changed(be_ref, m))
        def _():
            fetch(be_ref[m]).wait()
            wd_scr[...] = stage[...].astype(BF16)
            nxt = nx_ref[m]

            @pl.when(nxt >= 0)
            def _():
                fetch(nxt).start()

        for c, a_ref in enumerate(a_refs):
            @pl.when(jnp.logical_and(m >= chunk_starts[c], m < chunk_starts[c + 1]))
            def _(a_ref=a_ref):
                o_ref[...] = (_dot(a_ref[...], wd_scr[...]) + bd_ref[0]).astype(o_ref.dtype)

    @pl.when(m >= nu_ref[0])
    def _():
        o_ref[...] = jnp.zeros_like(o_ref)


def _moe_down(block_expert, n_used, acts, w_d, b_d):
    d_ff = acts[0].shape[1]
    chunk_starts = [0]
    for a in acts:
        chunk_starts.append(chunk_starts[-1] + a.shape[0] // MOE_TM)
    nb = chunk_starts[-1]
    d = w_d.shape[2]

    def blk(m, nu):
        return jnp.maximum(jnp.minimum(m, nu[0] - 1), 0)

    def act_spec(c):
        lo, n_c = chunk_starts[c], chunk_starts[c + 1] - chunk_starts[c]
        return pl.BlockSpec((MOE_TM, d_ff),
                            lambda m, be, nu, nx: (jnp.clip(blk(m, nu) - lo, 0, n_c - 1), 0))

    return pl.pallas_call(
        functools.partial(_moe_down_kernel, chunk_starts=tuple(chunk_starts)),
        out_shape=jax.ShapeDtypeStruct((nb * MOE_TM, d), BF16),
        grid_spec=pltpu.PrefetchScalarGridSpec(
            num_scalar_prefetch=3,
            grid=(nb,),
            in_specs=[act_spec(c) for c in range(len(acts))] + [
                pl.BlockSpec(memory_space=pl.ANY),
                pl.BlockSpec((1, 1, d), lambda m, be, nu, nx: (be[blk(m, nu)], 0, 0)),
            ],
            out_specs=pl.BlockSpec((MOE_TM, d), lambda m, be, nu, nx: (m, 0)),
            scratch_shapes=[pltpu.VMEM((d_ff, d), F32), pltpu.VMEM((d_ff, d), BF16),
                            pltpu.SemaphoreType.DMA((1,))],
        ),
        compiler_params=_cparams(("arbitrary",)),
        name="moe_down",
    )(block_expert, n_used, _next_run_expert(block_expert, n_used), *acts, w_d, b_d)


def _combine_kernel(x_ref, y0_ref, y1_ref, y2_ref, y3_ref, rg_ref, g_ref, o_ref, *, normalize):
    x = x_ref[...]
    rg = rg_ref[...]
    for k, y_ref in enumerate((y0_ref, y1_ref, y2_ref, y3_ref)):
        x = x + rg[:, k:k + 1] * y_ref[...].astype(F32)
    if normalize:
        ms = jnp.mean(x * x, axis=-1, keepdims=True)
        x = x * lax.rsqrt(ms + RMS_EPS) * g_ref[...]
    o_ref[...] = x


def _combine(x_buf, y_rows, rg, g, *, chunk, n_chunk_tokens, normalize, tm):
    t, d = x_buf.shape
    nblk = n_chunk_tokens // tm
    x_spec = pl.BlockSpec((tm, d), lambda i: (chunk * nblk + i, 0))

    def y_spec(k):
        return pl.BlockSpec((tm, d), lambda i: (k * nblk + i, 0))

    return pl.pallas_call(
        functools.partial(_combine_kernel, normalize=normalize),
        out_shape=jax.ShapeDtypeStruct((t, d), F32),
        grid=(nblk,),
        in_specs=[x_spec] + [y_spec(k) for k in range(TOP_K)] + [
            pl.BlockSpec((tm, rg.shape[1]), lambda i: (chunk * nblk + i, 0)),
            pl.BlockSpec((1, d), lambda i: (0, 0)),
        ],
        out_specs=x_spec,
        input_output_aliases={0: 0},
        compiler_params=_cparams(("parallel",)),
        name="combine_norm",
    )(x_buf, y_rows, y_rows, y_rows, y_rows, rg, g)


def _route(ri, counts, t):
    p_n = t * TOP_K
    experts = ri[:TOP_K]
    rank = ri[TOP_K:]
    counts = counts[:, 0].astype(jnp.int32)
    padded = (counts + MOE_TM - 1) // MOE_TM * MOE_TM
    pad_end = jnp.cumsum(padded)
    pad_start = pad_end - padded
    e_ids = jnp.arange(N_EXPERTS, dtype=jnp.int32)[:, None, None]
    dest = rank + jnp.sum(jnp.where(experts[None] == e_ids, pad_start[:, None, None], 0), axis=0)
    nb = -(-p_n // MOE_TM) + N_EXPERTS
    src_tok = (jnp.arange(nb * MOE_TM, dtype=jnp.int32) % t).at[dest.reshape(p_n)].set(
        jnp.tile(jnp.arange(t, dtype=jnp.int32), TOP_K), mode="promise_in_bounds",
        unique_indices=True)
    block_start = jnp.arange(nb, dtype=jnp.int32) * MOE_TM
    block_expert = jnp.minimum(
        jnp.sum((block_start[:, None] >= pad_end[None, :]).astype(jnp.int32), axis=1),
        N_EXPERTS - 1)
    n_used = (pad_end[-1] // MOE_TM).astype(jnp.int32).reshape(1)
    return dest, src_tok, block_expert, n_used


def kernel(x, norm_mix, w_in, conv_w, conv_b, b_gates_if, norm_head, w_proj_a, w_proj_b,
           w_merge_gate, b_merge_gate, w_out, norm_ffn, w_router, b_router, w_gate_up,
           b_gate_up, w_down, b_down, norm_final):
    bn, s, d = x.shape
    assert bn == 1
    depth = norm_mix.shape[0]
    xt = x.reshape(s, d)
    tm = min(1024, s)
    aqk = A_HEADS * A_QK
    av = A_HEADS * A_V
    bw = B_HEADS * B_HEAD_DIM

    for l in range(depth):
        wl = w_in[l]
        c0 = 2 * aqk
        c1 = c0 + 2 * av
        c2 = c1 + 2 * A_HEADS
        w_f32 = jnp.concatenate(
            [wl[:, :c0], wl[:, c1:c2], jnp.zeros((d, LANE - 2 * A_HEADS), F32)], axis=1).astype(BF16)
        b_f32 = jnp.concatenate(
            [jnp.zeros((c0,), F32), b_gates_if[l], jnp.zeros((LANE - 2 * A_HEADS,), F32)])[None, :]
        w_bf = jnp.concatenate([wl[:, c0:c1], wl[:, c2:c2 + bw] * (B_HEAD_DIM ** -0.5), wl[:, c2 + bw:]],
                               axis=1).astype(BF16)
        g_mix = norm_mix[l][None, :]

        p_f32, h1, if_row = _norm_proj(xt, g_mix, w_f32, b_f32, tm=tm)
        tn = 1024
        o_tile = av // tn
        p_bf = _proj(h1, w_bf, jnp.zeros((1, w_bf.shape[1]), F32),
                     sigmoid_tiles=(o_tile, 2 * o_tile), tm=tm, tn=tn)
        n_gate_tiles = w_merge_gate.shape[2] // tn
        gates = _proj(h1, w_merge_gate[l].astype(BF16), b_merge_gate[l][None, :],
                      sigmoid_tiles=(0, n_gate_tiles), tm=tm, tn=tn)

        y_a = _mlstm(p_f32, p_bf, if_row, conv_w[l], conv_b[l][None, :], norm_head[l][None, :])
        qb = 2 * av // B_HEAD_DIM
        y_b = _stick_breaking(p_bf, qb, qb + B_HEADS, qb + 2 * B_HEADS)

        mixed = _mix(gates, y_a, w_proj_a[l].astype(BF16), y_b, w_proj_b[l].astype(BF16),
                     tm=tm, tn=512)
        wr_hi, wr_lo = _split_bf16(w_router[l].T)
        b_r = jnp.broadcast_to(b_router[l][:, None], (N_EXPERTS, LANE))
        x1, h2, ri, rg, counts = _out_proj(xt, mixed, w_out[l].astype(BF16), norm_ffn[l][None, :],
                                           wr_hi, wr_lo, b_r, tm=min(512, s))

        dest, src_tok, block_expert, n_used = _route(ri, counts, s)
        nb = block_expert.shape[0]
        bounds = [0, nb // MOE_FIRST_CHUNK_DIV, nb]
        acts = []
        for lo, hi in zip(bounds[:-1], bounds[1:]):
            x_pad = h2.at[src_tok[lo * MOE_TM:hi * MOE_TM]].get(mode="promise_in_bounds")
            acts.append(_moe_up(block_expert[lo:hi], jnp.clip(n_used - lo, 0, hi - lo),
                                x_pad, w_gate_up[l], b_gate_up[l][:, None, :], tf=1024))
        y_pad = _moe_down(block_expert, n_used, acts, w_down[l], b_down[l][:, None, :])
        nct = s // COMBINE_CHUNKS
        xt = x1
        for j in range(COMBINE_CHUNKS):
            idx = dest[:, j * nct:(j + 1) * nct].reshape(TOP_K * nct)
            y_rows = y_pad.at[idx].get(mode="promise_in_bounds")
            xt = _combine(xt, y_rows, rg, norm_final[None, :], chunk=j, n_chunk_tokens=nct,
                          normalize=(l + 1 == depth), tm=min(512, nct))
    return xt.reshape(bn, s, d)
```

```python
import functools

import jax
import jax.numpy as jnp
from jax import lax
from jax.experimental import pallas as pl
from jax.experimental.pallas import tpu as pltpu

F32 = jnp.float32
BF16 = jnp.bfloat16

RMS_EPS = 1e-5
A_HEADS = 4
A_QK = 128
A_V = 256
CONV_W = 4
B_HEADS = 8
B_HEAD_DIM = 128
N_EXPERTS = 32
TOP_K = 4
SWIGLU_LIMIT = 7.0
SWIGLU_ALPHA = 1.702

LANE = 128
SUBLANE = 8
VMEM_LIMIT = 60 * 1024 * 1024

MLSTM_CHUNK = 128
SB_BQ = 2048
SB_SUB = 128
SB_SKIP = 88.0
MOE_TM = 512
MOE_CHUNK_TENTHS = (1, 4)
COMBINE_CHUNKS = 4


def _cparams(sem):
    return pltpu.CompilerParams(dimension_semantics=sem, vmem_limit_bytes=VMEM_LIMIT)


def _log_sigmoid(z):
    return jnp.minimum(z, 0.0) - jnp.log1p(jnp.exp(-jnp.abs(z)))


def _sigmoid(z):
    return 1.0 / (1.0 + jnp.exp(-z))


def _split_bf16(x):
    hi = x.astype(BF16)
    lo = (x - hi.astype(F32)).astype(BF16)
    return hi, lo


def _dot(a, b):
    return jnp.dot(a, b, preferred_element_type=F32)


def _dot_nt(a, b):
    return lax.dot_general(a, b, (((1,), (1,)), ((), ())), preferred_element_type=F32)


def _dot_tn(a, b):
    return lax.dot_general(a, b, (((0,), (0,)), ((), ())), preferred_element_type=F32)


def _norm_proj_kernel(x_ref, g_ref, w_ref, b_ref, o_ref, h_ref, gt_ref):
    x = x_ref[...]
    ms = jnp.mean(x * x, axis=-1, keepdims=True)
    h = (x * lax.rsqrt(ms + RMS_EPS) * g_ref[...]).astype(BF16)
    h_ref[...] = h
    acc = _dot(h, w_ref[...]) + b_ref[...]
    o_ref[...] = acc
    gt_ref[...] = acc[:, acc.shape[1] - LANE:].T[:SUBLANE, :]


def _norm_proj(x, g, w, b, *, tm):
    t, d = x.shape
    n = w.shape[1]
    return pl.pallas_call(
        _norm_proj_kernel,
        out_shape=(jax.ShapeDtypeStruct((t, n), F32),
                   jax.ShapeDtypeStruct((t, d), BF16),
                   jax.ShapeDtypeStruct((SUBLANE, t), F32)),
        grid=(t // tm,),
        in_specs=[
            pl.BlockSpec((tm, d), lambda i: (i, 0)),
            pl.BlockSpec((1, d), lambda i: (0, 0)),
            pl.BlockSpec((d, n), lambda i: (0, 0)),
            pl.BlockSpec((1, n), lambda i: (0, 0)),
        ],
        out_specs=(pl.BlockSpec((tm, n), lambda i: (i, 0)),
                   pl.BlockSpec((tm, d), lambda i: (i, 0)),
                   pl.BlockSpec((SUBLANE, tm), lambda i: (0, i))),
        compiler_params=_cparams(("parallel",)),
        name="norm_proj",
    )(x, g, w, b)


def _proj_kernel(h_ref, w_ref, b_ref, o_ref, *, sigmoid_tiles, n_tiles):
    acc = _dot(h_ref[...], w_ref[...]) + b_ref[...]
    lo, hi = sigmoid_tiles
    if lo == 0 and hi == n_tiles:
        o_ref[...] = _sigmoid(acc).astype(o_ref.dtype)
    elif lo == hi:
        o_ref[...] = acc.astype(o_ref.dtype)
    else:
        j = pl.program_id(1)
        gated = jnp.logical_and(j >= lo, j < hi)

        @pl.when(gated)
        def _():
            o_ref[...] = _sigmoid(acc).astype(o_ref.dtype)

        @pl.when(jnp.logical_not(gated))
        def _():
            o_ref[...] = acc.astype(o_ref.dtype)


def _proj(h, w, b, *, sigmoid_tiles, tm, tn):
    t, d = h.shape
    n = w.shape[1]
    return pl.pallas_call(
        functools.partial(_proj_kernel, sigmoid_tiles=sigmoid_tiles, n_tiles=n // tn),
        out_shape=jax.ShapeDtypeStruct((t, n), BF16),
        grid=(t // tm, n // tn),
        in_specs=[
            pl.BlockSpec((tm, d), lambda i, j: (i, 0)),
            pl.BlockSpec((d, tn), lambda i, j: (0, j)),
            pl.BlockSpec((1, tn), lambda i, j: (0, j)),
        ],
        out_specs=pl.BlockSpec((tm, tn), lambda i, j: (i, j)),
        compiler_params=_cparams(("parallel", "arbitrary")),
        name="proj",
    )(h, w, b)


def _mlstm_kernel(qk_ref, v_ref, o_ref, ifc_ref, ifr_ref, cw_ref, cb_ref, nh_ref,
                  y_ref, ext_scr, c_scr, n_scr, m_scr):
    L = MLSTM_CHUNK
    step = pl.program_id(0)

    @pl.when(step == 0)
    def _():
        ext_scr[0:SUBLANE, :] = jnp.zeros((SUBLANE, ext_scr.shape[1]), F32)
        c_scr[...] = jnp.zeros_like(c_scr)
        n_scr[...] = jnp.zeros_like(n_scr)
        m_scr[...] = jnp.zeros_like(m_scr)

    raw = qk_ref[...]
    ext_scr[SUBLANE:SUBLANE + L, :] = raw
    conv = cb_ref[...]
    for j in range(CONV_W):
        off = SUBLANE - (CONV_W - 1) + j
        conv = conv + ext_scr[off:off + L, :] * cw_ref[j:j + 1, :]
    ext_scr[0:SUBLANE, :] = raw[L - SUBLANE:L, :]
    qk = conv * _sigmoid(conv)

    row = lax.broadcasted_iota(jnp.int32, (L, L), 0)
    col = lax.broadcasted_iota(jnp.int32, (L, L), 1)
    causal = row >= col
    tri_incl = jnp.where(causal, 1.0, 0.0).astype(BF16)
    tri_incl_t = jnp.where(col >= row, 1.0, 0.0).astype(BF16)

    hi, lo = _split_bf16(_log_sigmoid(ifc_ref[...]))
    a_cols = _dot(tri_incl, hi) + _dot(tri_incl, lo)
    hi, lo = _split_bf16(_log_sigmoid(ifr_ref[...]))
    a_rows = _dot(hi, tri_incl_t) + _dot(lo, tri_incl_t)

    for h in range(A_HEADS):
        q = qk[:, h * A_QK:(h + 1) * A_QK]
        k = qk[:, A_HEADS * A_QK + h * A_QK:A_HEADS * A_QK + (h + 1) * A_QK] * (A_QK ** -0.5)
        v = v_ref[:, h * A_V:(h + 1) * A_V]
        q_bf = q.astype(BF16)
        k_bf = k.astype(BF16)

        i_col = ifc_ref[:, h:h + 1]
        i_row = ifr_ref[h:h + 1, :]
        a_col = a_cols[:, A_HEADS + h:A_HEADS + h + 1]
        a_row = a_rows[A_HEADS + h:A_HEADS + h + 1, :]
        g_tot = a_col[L - 1:L, :]

        m_prev = m_scr[h]
        n_prev = n_scr[h]
        ct_prev = c_scr[h]

        m_inter = a_col + m_prev
        d_log = jnp.where(causal, a_col - a_row + i_row, -jnp.inf)
        m_t = jnp.maximum(m_inter, jnp.max(d_log, axis=-1, keepdims=True))
        p = jnp.exp(d_log - m_t) * _dot_nt(q_bf, k_bf)
        s_inter = jnp.exp(m_inter - m_t)
        num = s_inter * _dot(q_bf, ct_prev.astype(BF16)) + _dot(p.astype(BF16), v)
        den = (s_inter * jnp.sum(q * n_prev, axis=-1, keepdims=True)
               + jnp.sum(p, axis=-1, keepdims=True))
        hh = num / jnp.maximum(jnp.abs(den), jnp.exp(-m_t))

        hh = hh * lax.rsqrt(jnp.mean(hh * hh, axis=-1, keepdims=True) + RMS_EPS)
        gate = o_ref[:, h * A_V:(h + 1) * A_V].astype(F32)
        y_ref[:, h * A_V:(h + 1) * A_V] = (
            hh * nh_ref[:, h * A_V:(h + 1) * A_V] * gate).astype(y_ref.dtype)

        w_col = g_tot - a_col + i_col
        m_loc = jnp.max(w_col, axis=0, keepdims=True)
        ke = k * jnp.exp(w_col - m_loc)
        ct_loc = _dot_tn(ke.astype(BF16), v)
        n_loc = jnp.sum(ke, axis=0, keepdims=True)
        m_new = jnp.maximum(g_tot + m_prev, m_loc)
        s_old = jnp.exp(g_tot + m_prev - m_new)
        s_new = jnp.exp(m_loc - m_new)
        c_scr[h] = s_old * ct_prev + s_new * ct_loc
        n_scr[h] = s_old * n_prev + s_new * n_loc
        m_scr[h] = m_new


def _mlstm(p_f32, p_bf, if_row, conv_w, conv_b, norm_head):
    t = p_f32.shape[0]
    L = MLSTM_CHUNK
    wq = conv_w.shape[1]
    wv = norm_head.shape[1]
    return pl.pallas_call(
        _mlstm_kernel,
        out_shape=jax.ShapeDtypeStruct((t, wv), BF16),
        grid=(t // L,),
        in_specs=[
            pl.BlockSpec((L, wq), lambda i: (i, 0)),
            pl.BlockSpec((L, wv), lambda i: (i, 0)),
            pl.BlockSpec((L, wv), lambda i: (i, 1)),
            pl.BlockSpec((L, LANE), lambda i: (i, wq // LANE)),
            pl.BlockSpec((SUBLANE, L), lambda i: (0, i)),
            pl.BlockSpec((CONV_W, wq), lambda i: (0, 0)),
            pl.BlockSpec((1, wq), lambda i: (0, 0)),
            pl.BlockSpec((1, wv), lambda i: (0, 0)),
        ],
        out_specs=pl.BlockSpec((L, wv), lambda i: (i, 0)),
        scratch_shapes=[
            pltpu.VMEM((L + SUBLANE, wq), F32),
            pltpu.VMEM((A_HEADS, A_QK, A_V), F32),
            pltpu.VMEM((A_HEADS, 1, A_QK), F32),
            pltpu.VMEM((A_HEADS, 1, 1), F32),
        ],
        compiler_params=_cparams(("arbitrary",)),
        name="mlstm",
    )(p_f32, p_bf, p_bf, p_f32, if_row, conv_w, conv_b, norm_head)


def _later(n):
    j = lax.broadcasted_iota(jnp.int32, (n, n), 0)
    s = lax.broadcasted_iota(jnp.int32, (n, n), 1)
    return jnp.where(j > s, 1.0, 0.0).astype(BF16)


def _sb_logs(z):
    ls = jnp.minimum(z, 0.0) - jnp.log(1.0 + jnp.exp(-jnp.abs(z)))
    return ls, ls - z


def _sb_later_sums(lk, later2):
    hi, lo = _split_bf16(lk)
    return _dot(jnp.concatenate([hi, lo], axis=1), later2)


def _sb_tile(q, k, v, carry, later2):
    ls, lk = _sb_logs(_dot_nt(q, k))
    r_in = _sb_later_sums(lk, later2)
    w = jnp.exp(ls + r_in + carry)
    return _dot(w.astype(BF16), v), carry + r_in[:, 0:1] + lk[:, 0:1]


def _sb_kernel(q_ref, kc_ref, vc_ref, kp_ref, vp_ref, k_ref, v_ref, o_ref, acc_scr, carry_scr):
    sub = SB_SUB
    nsub = q_ref.shape[0] // sub
    step = pl.program_id(1)
    later = _later(sub)
    later2 = jnp.concatenate([later, later], axis=0)
    row = lax.broadcasted_iota(jnp.int32, (sub, sub), 0)
    col = lax.broadcasted_iota(jnp.int32, (sub, sub), 1)
    diag_past = col < row
    has_prev = step > 0

    blk = [slice(j * sub, (j + 1) * sub) for j in range(nsub)]
    qs = [q_ref[b, :] for b in blk]
    k_d = [kc_ref[b, :] for b in blk]
    v_d = [vc_ref[b, :] for b in blk]
    k_p = [kp_ref[...]] + k_d[:-1]
    v_p = [vp_ref[...]] + v_d[:-1]
    ls_d, lk_d, ls_p, lk_p = [], [], [], []
    for j in range(nsub):
        ls, lk = _sb_logs(_dot_nt(qs[j], k_d[j]))
        ls_d.append(ls)
        lk_d.append(jnp.where(diag_past, lk, 0.0))
        ls, lk = _sb_logs(_dot_nt(qs[j], k_p[j]))
        ls_p.append(ls)
        lk_p.append(jnp.where(has_prev, lk, 0.0) if j == 0 else lk)
    r_in = _sb_later_sums(jnp.concatenate(lk_d + lk_p, axis=0), later2)
    first_max = []
    for j in range(nsub):
        r_d = r_in[j * sub:(j + 1) * sub]
        r_p = r_in[(nsub + j) * sub:(nsub + j + 1) * sub]
        carry_d = r_d[:, 0:1] + lk_d[j][:, 0:1]
        w_d = jnp.where(diag_past, jnp.exp(ls_d[j] + r_d), 0.0)
        w_p = jnp.exp(ls_p[j] + r_p + carry_d)
        if j == 0:
            w_p = jnp.where(has_prev, w_p, 0.0)
        carry = carry_d + r_p[:, 0:1] + lk_p[j][:, 0:1]
        acc_scr[j] = _dot(w_d.astype(BF16), v_d[j]) + _dot(w_p.astype(BF16), v_p[j])
        carry_scr[j] = carry
        first_max.append(jnp.max(carry))

    for j in range(nsub):
        def cond(state):
            kb, mx = state
            return jnp.logical_and(kb >= 0, mx > -SB_SKIP)

        def body(state, j=j):
            kb, _ = state
            k0 = pl.multiple_of(kb * sub, sub)
            acc, carry = _sb_tile(q_ref[j * sub:(j + 1) * sub, :], k_ref[pl.ds(k0, sub), :],
                                  v_ref[pl.ds(k0, sub), :], carry_scr[j], later2)
            acc_scr[j] += acc
            carry_scr[j] = carry
            return kb - 1, jnp.max(carry)

        lax.while_loop(cond, body, (step * nsub + j - 2, first_max[j]))
        o_ref[j * sub:(j + 1) * sub, :] = acc_scr[j].astype(o_ref.dtype)


def _stick_breaking(qkv, q_col, k_col, v_col):
    t = qkv.shape[0]
    d = B_HEAD_DIM
    bq = min(SB_BQ, t)
    nsub = bq // SB_SUB
    assert t % bq == 0 and bq % SB_SUB == 0

    def cur(col):
        return pl.BlockSpec((bq, d), lambda h, i: (i, col + h))

    def prev(col):
        return pl.BlockSpec((SB_SUB, d), lambda h, i: (jnp.maximum(i * nsub - 1, 0), col + h))

    def full(col):
        return pl.BlockSpec((t, d), lambda h, i: (0, col + h))

    return pl.pallas_call(
        _sb_kernel,
        out_shape=jax.ShapeDtypeStruct((t, B_HEADS * d), BF16),
        grid=(B_HEADS, t // bq),
        in_specs=[cur(q_col), cur(k_col), cur(v_col), prev(k_col), prev(v_col),
                  full(k_col), full(v_col)],
        out_specs=pl.BlockSpec((bq, d), lambda h, i: (i, h)),
        scratch_shapes=[pltpu.VMEM((nsub, SB_SUB, d), F32), pltpu.VMEM((nsub, SB_SUB, 1), F32)],
        compiler_params=_cparams(("parallel", "arbitrary")),
        name="stick_breaking",
    )(qkv, qkv, qkv, qkv, qkv, qkv, qkv)


def _mix_kernel(g1_ref, g2_ref, ya_ref, wa_ref, yb_ref, wb_ref, o_ref):
    pa = _dot(ya_ref[...], wa_ref[...])
    pb = _dot(yb_ref[...], wb_ref[...])
    o_ref[...] = (g1_ref[...].astype(F32) * pa + g2_ref[...].astype(F32) * pb).astype(o_ref.dtype)


def _mix(gates, y_a, w_a, y_b, w_b, *, tm, tn):
    t = y_a.shape[0]
    d = w_a.shape[1]
    nt = d // tn
    return pl.pallas_call(
        _mix_kernel,
        out_shape=jax.ShapeDtypeStruct((t, d), BF16),
        grid=(t // tm, nt),
        in_specs=[
            pl.BlockSpec((tm, tn), lambda i, j: (i, j)),
            pl.BlockSpec((tm, tn), lambda i, j: (i, j + nt)),
            pl.BlockSpec((tm, y_a.shape[1]), lambda i, j: (i, 0)),
            pl.BlockSpec((w_a.shape[0], tn), lambda i, j: (0, j)),
            pl.BlockSpec((tm, y_b.shape[1]), lambda i, j: (i, 0)),
            pl.BlockSpec((w_b.shape[0], tn), lambda i, j: (0, j)),
        ],
        out_specs=pl.BlockSpec((tm, tn), lambda i, j: (i, j)),
        compiler_params=_cparams(("parallel", "arbitrary")),
        name="mix",
    )(gates, gates, y_a, w_a, y_b, w_b)


def _out_kernel(x_ref, mix_ref, wo_ref, g_ref, wrh_ref, wrl_ref, br_ref,
                x1_ref, h2_ref, ri_ref, rg_ref, cnt_ref):
    tm = x_ref.shape[0]
    x1 = x_ref[...] + _dot(mix_ref[...], wo_ref[...])
    x1_ref[...] = x1
    ms = jnp.mean(x1 * x1, axis=-1, keepdims=True)
    h2 = x1 * lax.rsqrt(ms + RMS_EPS) * g_ref[...]
    h_hi, h_lo = _split_bf16(h2)
    h2_ref[...] = h_hi
    w_hi = wrh_ref[...]
    lg = (_dot_nt(w_hi, h_hi) + _dot_nt(w_hi, h_lo) + _dot_nt(wrl_ref[...], h_hi)
          + br_ref[:, 0:1])

    @pl.when(pl.program_id(0) == 0)
    def _():
        cnt_ref[...] = jnp.zeros_like(cnt_ref)

    row = lax.broadcasted_iota(jnp.int32, (N_EXPERTS, tm), 0)
    vals, idxs, hots = [], [], []
    for _ in range(TOP_K):
        mx = jnp.max(lg, axis=0, keepdims=True)
        idx = jnp.min(jnp.where(lg == mx, row, N_EXPERTS), axis=0, keepdims=True)
        hot = row == idx
        vals.append(mx)
        idxs.append(idx)
        hots.append(hot)
        lg = jnp.where(hot, -jnp.inf, lg)

    exps = [jnp.exp(v - vals[0]) for v in vals]
    denom = exps[0]
    for e in exps[1:]:
        denom = denom + e

    chosen = hots[0]
    for hot in hots[1:]:
        chosen = jnp.logical_or(chosen, hot)
    chosen_f = jnp.where(chosen, 1.0, 0.0)
    r = lax.broadcasted_iota(jnp.int32, (tm, tm), 0)
    c = lax.broadcasted_iota(jnp.int32, (tm, tm), 1)
    earlier = jnp.where(r < c, 1.0, 0.0).astype(BF16)
    before = _dot(chosen_f.astype(BF16), earlier) + cnt_ref[:, 0:1]
    cnt_ref[...] += jnp.sum(chosen_f, axis=1, keepdims=True)

    row8 = lax.broadcasted_iota(jnp.int32, (2 * TOP_K, tm), 0)
    ri = jnp.zeros((2 * TOP_K, tm), jnp.int32)
    rg = jnp.zeros((2 * TOP_K, tm), F32)
    for k in range(TOP_K):
        rank = jnp.sum(jnp.where(hots[k], before, 0.0), axis=0, keepdims=True).astype(jnp.int32)
        ri = jnp.where(row8 == k, idxs[k], ri)
        ri = jnp.where(row8 == TOP_K + k, rank, ri)
        rg = jnp.where(row8 == k, exps[k] / denom, rg)
    ri_ref[...] = ri
    rg_ref[...] = jnp.concatenate([rg, jnp.zeros((LANE - 2 * TOP_K, tm), F32)], axis=0).T


def _out_proj(x, mixed, w_o, g, wr_hi, wr_lo, b_r, *, tm):
    t, d = x.shape
    ne = wr_hi.shape[0]
    return pl.pallas_call(
        _out_kernel,
        out_shape=(jax.ShapeDtypeStruct((t, d), F32),
                   jax.ShapeDtypeStruct((t, d), BF16),
                   jax.ShapeDtypeStruct((2 * TOP_K, t), jnp.int32),
                   jax.ShapeDtypeStruct((t, LANE), F32),
                   jax.ShapeDtypeStruct((ne, LANE), F32)),
        grid=(t // tm,),
        in_specs=[
            pl.BlockSpec((tm, d), lambda i: (i, 0)),
            pl.BlockSpec((tm, d), lambda i: (i, 0)),
            pl.BlockSpec((d, d), lambda i: (0, 0)),
            pl.BlockSpec((1, d), lambda i: (0, 0)),
            pl.BlockSpec((ne, d), lambda i: (0, 0)),
            pl.BlockSpec((ne, d), lambda i: (0, 0)),
            pl.BlockSpec((ne, LANE), lambda i: (0, 0)),
        ],
        out_specs=(pl.BlockSpec((tm, d), lambda i: (i, 0)),
                   pl.BlockSpec((tm, d), lambda i: (i, 0)),
                   pl.BlockSpec((2 * TOP_K, tm), lambda i: (0, i)),
                   pl.BlockSpec((tm, LANE), lambda i: (i, 0)),
                   pl.BlockSpec((ne, LANE), lambda i: (0, 0))),
        compiler_params=_cparams(("arbitrary",)),
        name="out_proj",
    )(x, mixed, w_o, g, wr_hi, wr_lo, b_r)


def _expert_changed(be_ref, m):
    return jnp.logical_or(m == 0, be_ref[m] != be_ref[jnp.maximum(m - 1, 0)])


def _next_run_expert(block_expert, n_used):
    nb = block_expert.shape[0]
    idx = jnp.arange(nb, dtype=jnp.int32)
    starts = jnp.logical_or(idx == 0, block_expert != jnp.roll(block_expert, 1))
    cand = jnp.where(jnp.logical_and(starts, idx < n_used[0]), idx, nb)
    later_start = jnp.concatenate([lax.cummin(cand, axis=0, reverse=True)[1:],
                                   jnp.full((1,), nb, jnp.int32)])
    at_start = jnp.sum(jnp.where(idx[None, :] == later_start[:, None], block_expert[None, :], 0), axis=1)
    return jnp.where(later_start < nb, at_start, -1).astype(jnp.int32)


def _moe_up_kernel(be_ref, nu_ref, nx_ref, x_ref, w_hbm, bg_ref, bu_ref, o_ref,
                   stage, wg_scr, wu_scr, sem):
    n = pl.program_id(0)
    m = pl.program_id(1)
    nf = pl.num_programs(0)
    tf = wg_scr.shape[1]

    def fetch(e, nn, part):
        col = pl.multiple_of((nn + part * nf) * tf, tf)
        return pltpu.make_async_copy(w_hbm.at[e, :, pl.ds(col, tf)], stage.at[part], sem.at[part])

    def start(e, nn):
        fetch(e, nn, 0).start()
        fetch(e, nn, 1).start()

    @pl.when(jnp.logical_and(jnp.logical_and(n == 0, m == 0), nu_ref[0] > 0))
    def _():
        start(be_ref[0], 0)

    @pl.when(m < nu_ref[0])
    def _():
        @pl.when(_expert_changed(be_ref, m))
        def _():
            fetch(be_ref[m], n, 0).wait()
            fetch(be_ref[m], n, 1).wait()
            wg_scr[...] = stage[0].astype(BF16)
            wu_scr[...] = stage[1].astype(BF16)
            nxt = nx_ref[m]

            @pl.when(nxt >= 0)
            def _():
                start(nxt, n)

            @pl.when(jnp.logical_and(nxt < 0, n + 1 < nf))
            def _():
                start(be_ref[0], n + 1)

        x = x_ref[...]
        gate = jnp.minimum(_dot(x, wg_scr[...]) + bg_ref[0], SWIGLU_LIMIT)
        up = jnp.clip(_dot(x, wu_scr[...]) + bu_ref[0], -SWIGLU_LIMIT, SWIGLU_LIMIT)
        o_ref[...] = ((up + 1.0) * gate * _sigmoid(SWIGLU_ALPHA * gate)).astype(o_ref.dtype)

    @pl.when(m >= nu_ref[0])
    def _():
        o_ref[...] = jnp.zeros_like(o_ref)


def _moe_up(block_expert, n_used, x_pad, w_gu, b_gu, *, tf):
    rows, d = x_pad.shape
    nb = rows // MOE_TM
    d_ff = w_gu.shape[2] // 2
    nf = d_ff // tf

    def blk(m, nu):
        return jnp.maximum(jnp.minimum(m, nu[0] - 1), 0)

    return pl.pallas_call(
        _moe_up_kernel,
        out_shape=jax.ShapeDtypeStruct((rows, d_ff), BF16),
        grid_spec=pltpu.PrefetchScalarGridSpec(
            num_scalar_prefetch=3,
            grid=(nf, nb),
            in_specs=[
                pl.BlockSpec((MOE_TM, d), lambda n, m, be, nu, nx: (blk(m, nu), 0)),
                pl.BlockSpec(memory_space=pl.ANY),
                pl.BlockSpec((1, 1, tf), lambda n, m, be, nu, nx: (be[blk(m, nu)], 0, n)),
                pl.BlockSpec((1, 1, tf), lambda n, m, be, nu, nx: (be[blk(m, nu)], 0, n + nf)),
            ],
            out_specs=pl.BlockSpec((MOE_TM, tf), lambda n, m, be, nu, nx: (m, n)),
            scratch_shapes=[pltpu.VMEM((2, d, tf), F32), pltpu.VMEM((d, tf), BF16),
                            pltpu.VMEM((d, tf), BF16), pltpu.SemaphoreType.DMA((2,))],
        ),
        compiler_params=_cparams(("arbitrary", "arbitrary")),
        name="moe_up",
    )(block_expert, n_used, _next_run_expert(block_expert, n_used), x_pad, w_gu, b_gu, b_gu)


def _moe_down_kernel(be_ref, nu_ref, nx_ref, *refs, chunk_starts):
    a_refs, (w_hbm, bd_ref, o_ref, stage, wd_scr, sem) = refs[:-6], refs[-6:]
    m = pl.program_id(0)

    def fetch(e):
        return pltpu.make_async_copy(w_hbm.at[e], stage, sem.at[0])

    @pl.when(jnp.logical_and(m == 0, nu_ref[0] > 0))
    def _():
        fetch(be_ref[0]).start()

    @pl.when(m < nu_ref[0])
    def _():
        @pl.when(_expert_changed(be_ref, m))
        def _():
            fetch(be_ref[m]).wait()
            wd_scr[...] = stage[...].astype(BF16)
            nxt = nx_ref[m]

            @pl.when(nxt >= 0)
            def _():
                fetch(nxt).start()

        for c, a_ref in enumerate(a_refs):
            @pl.when(jnp.logical_and(m >= chunk_starts[c], m < chunk_starts[c + 1]))
            def _(a_ref=a_ref):
                o_ref[...] = (_dot(a_ref[...], wd_scr[...]) + bd_ref[0]).astype(o_ref.dtype)

    @pl.when(m >= nu_ref[0])
    def _():
        o_ref[...] = jnp.zeros_like(o_ref)


def _moe_down(block_expert, n_used, acts, w_d, b_d):
    d_ff = acts[0].shape[1]
    chunk_starts = [0]
    for a in acts:
        chunk_starts.append(chunk_starts[-1] + a.shape[0] // MOE_TM)
    nb = chunk_starts[-1]
    d = w_d.shape[2]

    def blk(m, nu):
        return jnp.maximum(jnp.minimum(m, nu[0] - 1), 0)

    def act_spec(c):
        lo, n_c = chunk_starts[c], chunk_starts[c + 1] - chunk_starts[c]
        return pl.BlockSpec((MOE_TM, d_ff),
                            lambda m, be, nu, nx: (jnp.clip(blk(m, nu) - lo, 0, n_c - 1), 0))

    return pl.pallas_call(
        functools.partial(_moe_down_kernel, chunk_starts=tuple(chunk_starts)),
        out_shape=jax.ShapeDtypeStruct((nb * MOE_TM, d), BF16),
        grid_spec=pltpu.PrefetchScalarGridSpec(
            num_scalar_prefetch=3,
            grid=(nb,),
            in_specs=[act_spec(c) for c in range(len(acts))] + [
                pl.BlockSpec(memory_space=pl.ANY),
                pl.BlockSpec((1, 1, d), lambda m, be, nu, nx: (be[blk(m, nu)], 0, 0)),
            ],
            out_specs=pl.BlockSpec((MOE_TM, d), lambda m, be, nu, nx: (m, 0)),
            scratch_shapes=[pltpu.VMEM((d_ff, d), F32), pltpu.VMEM((d_ff, d), BF16),
                            pltpu.SemaphoreType.DMA((1,))],
        ),
        compiler_params=_cparams(("arbitrary",)),
        name="moe_down",
    )(block_expert, n_used, _next_run_expert(block_expert, n_used), *acts, w_d, b_d)


def _combine_kernel(x_ref, y0_ref, y1_ref, y2_ref, y3_ref, rg_ref, g_ref, o_ref, *, normalize):
    x = x_ref[...]
    rg = rg_ref[...]
    for k, y_ref in enumerate((y0_ref, y1_ref, y2_ref, y3_ref)):
        x = x + rg[:, k:k + 1] * y_ref[...].astype(F32)
    if normalize:
        ms = jnp.mean(x * x, axis=-1, keepdims=True)
        x = x * lax.rsqrt(ms + RMS_EPS) * g_ref[...]
    o_ref[...] = x


def _combine(x_buf, y_rows, rg, g, *, chunk, n_chunk_tokens, normalize, tm):
    t, d = x_buf.shape
    nblk = n_chunk_tokens // tm
    x_spec = pl.BlockSpec((tm, d), lambda i: (chunk * nblk + i, 0))

    def y_spec(k):
        return pl.BlockSpec((tm, d), lambda i: (k * nblk + i, 0))

    return pl.pallas_call(
        functools.partial(_combine_kernel, normalize=normalize),
        out_shape=jax.ShapeDtypeStruct((t, d), F32),
        grid=(nblk,),
        in_specs=[x_spec] + [y_spec(k) for k in range(TOP_K)] + [
            pl.BlockSpec((tm, rg.shape[1]), lambda i: (chunk * nblk + i, 0)),
            pl.BlockSpec((1, d), lambda i: (0, 0)),
        ],
        out_specs=x_spec,
        input_output_aliases={0: 0},
        compiler_params=_cparams(("parallel",)),
        name="combine_norm",
    )(x_buf, y_rows, y_rows, y_rows, y_rows, rg, g)


def _route(ri, counts, t):
    p_n = t * TOP_K
    experts = ri[:TOP_K]
    rank = ri[TOP_K:]
    counts = counts[:, 0].astype(jnp.int32)
    padded = (counts + MOE_TM - 1) // MOE_TM * MOE_TM
    pad_end = jnp.cumsum(padded)
    pad_start = pad_end - padded
    e_ids = jnp.arange(N_EXPERTS, dtype=jnp.int32)[:, None, None]
    dest = rank + jnp.sum(jnp.where(experts[None] == e_ids, pad_start[:, None, None], 0), axis=0)
    nb = -(-p_n // MOE_TM) + N_EXPERTS
    src_tok = (jnp.arange(nb * MOE_TM, dtype=jnp.int32) % t).at[dest.reshape(p_n)].set(
        jnp.tile(jnp.arange(t, dtype=jnp.int32), TOP_K), mode="promise_in_bounds",
        unique_indices=True)
    block_start = jnp.arange(nb, dtype=jnp.int32) * MOE_TM
    block_expert = jnp.minimum(
        jnp.sum((block_start[:, None] >= pad_end[None, :]).astype(jnp.int32), axis=1),
        N_EXPERTS - 1)
    n_used = (pad_end[-1] // MOE_TM).astype(jnp.int32).reshape(1)
    return dest, src_tok, block_expert, n_used


def kernel(x, norm_mix, w_in, conv_w, conv_b, b_gates_if, norm_head, w_proj_a, w_proj_b,
           w_merge_gate, b_merge_gate, w_out, norm_ffn, w_router, b_router, w_gate_up,
           b_gate_up, w_down, b_down, norm_final):
    bn, s, d = x.shape
    assert bn == 1
    depth = norm_mix.shape[0]
    xt = x.reshape(s, d)
    tm = min(1024, s)
    aqk = A_HEADS * A_QK
    av = A_HEADS * A_V
    bw = B_HEADS * B_HEAD_DIM

    for l in range(depth):
        wl = w_in[l]
        c0 = 2 * aqk
        c1 = c0 + 2 * av
        c2 = c1 + 2 * A_HEADS
        w_f32 = jnp.concatenate(
            [wl[:, :c0], wl[:, c1:c2], jnp.zeros((d, LANE - 2 * A_HEADS), F32)], axis=1).astype(BF16)
        b_f32 = jnp.concatenate(
            [jnp.zeros((c0,), F32), b_gates_if[l], jnp.zeros((LANE - 2 * A_HEADS,), F32)])[None, :]
        w_bf = jnp.concatenate([wl[:, c0:c1], wl[:, c2:c2 + bw] * (B_HEAD_DIM ** -0.5), wl[:, c2 + bw:]],
                               axis=1).astype(BF16)
        g_mix = norm_mix[l][None, :]

        p_f32, h1, if_row = _norm_proj(xt, g_mix, w_f32, b_f32, tm=tm)
        tn = 1024
        o_tile = av // tn
        p_bf = _proj(h1, w_bf, jnp.zeros((1, w_bf.shape[1]), F32),
                     sigmoid_tiles=(o_tile, 2 * o_tile), tm=tm, tn=tn)
        n_gate_tiles = w_merge_gate.shape[2] // tn
        gates = _proj(h1, w_merge_gate[l].astype(BF16), b_merge_gate[l][None, :],
                      sigmoid_tiles=(0, n_gate_tiles), tm=tm, tn=tn)

        y_a = _mlstm(p_f32, p_bf, if_row, conv_w[l], conv_b[l][None, :], norm_head[l][None, :])
        qb = 2 * av // B_HEAD_DIM
        y_b = _stick_breaking(p_bf, qb, qb + B_HEADS, qb + 2 * B_HEADS)

        mixed = _mix(gates, y_a, w_proj_a[l].astype(BF16), y_b, w_proj_b[l].astype(BF16),
                     tm=tm, tn=d)
        wr_hi, wr_lo = _split_bf16(w_router[l].T)
        b_r = jnp.broadcast_to(b_router[l][:, None], (N_EXPERTS, LANE))
        x1, h2, ri, rg, counts = _out_proj(xt, mixed, w_out[l].astype(BF16), norm_ffn[l][None, :],
                                           wr_hi, wr_lo, b_r, tm=min(512, s))

        dest, src_tok, block_expert, n_used = _route(ri, counts, s)
        nb = block_expert.shape[0]
        bounds = [0] + [nb * f // 10 for f in MOE_CHUNK_TENTHS] + [nb]
        acts = []
        for lo, hi in zip(bounds[:-1], bounds[1:]):
            x_pad = h2.at[src_tok[lo * MOE_TM:hi * MOE_TM]].get(mode="promise_in_bounds")
            acts.append(_moe_up(block_expert[lo:hi], jnp.clip(n_used - lo, 0, hi - lo),
                                x_pad, w_gate_up[l], b_gate_up[l][:, None, :], tf=1024))
        y_pad = _moe_down(block_expert, n_used, acts, w_down[l], b_down[l][:, None, :])
        nct = s // COMBINE_CHUNKS
        xt = x1
        for j in range(COMBINE_CHUNKS):
            idx = dest[:, j * nct:(j + 1) * nct].reshape(TOP_K * nct)
            y_rows = y_pad.at[idx].get(mode="promise_in_bounds")
            xt = _combine(xt, y_rows, rg, norm_final[None, :], chunk=j, n_chunk_tokens=nct,
                          normalize=(l + 1 == depth), tm=min(512, nct))
    return xt.reshape(bn, s, d)
```

```python
import functools

import jax
import jax.numpy as jnp
from jax import lax
from jax.experimental import pallas as pl
from jax.experimental.pallas import tpu as pltpu

F32 = jnp.float32
BF16 = jnp.bfloat16

RMS_EPS = 1e-5
A_HEADS = 4
A_QK = 128
A_V = 256
CONV_W = 4
B_HEADS = 8
B_HEAD_DIM = 128
N_EXPERTS = 32
TOP_K = 4
SWIGLU_LIMIT = 7.0
SWIGLU_ALPHA = 1.702

LANE = 128
SUBLANE = 8
VMEM_LIMIT = 60 * 1024 * 1024

MLSTM_CHUNK = 128
SB_BQ = 2048
SB_SUB = 128
SB_SKIP = 88.0
MOE_TM = 512
CAST_ROWS = 32
MOE_CHUNK_TENTHS = (1, 4)
COMBINE_CHUNKS = 8


def _cparams(sem):
    return pltpu.CompilerParams(dimension_semantics=sem, vmem_limit_bytes=VMEM_LIMIT)


def _log_sigmoid(z):
    return jnp.minimum(z, 0.0) - jnp.log1p(jnp.exp(-jnp.abs(z)))


def _sigmoid(z):
    return 1.0 / (1.0 + jnp.exp(-z))


def _split_bf16(x):
    hi = x.astype(BF16)
    lo = (x - hi.astype(F32)).astype(BF16)
    return hi, lo


def _dot(a, b):
    return jnp.dot(a, b, preferred_element_type=F32)


def _dot_nt(a, b):
    return lax.dot_general(a, b, (((1,), (1,)), ((), ())), preferred_element_type=F32)


def _dot_tn(a, b):
    return lax.dot_general(a, b, (((0,), (0,)), ((), ())), preferred_element_type=F32)


def _norm_proj_kernel(x_ref, g_ref, w_ref, b_ref, o_ref, h_ref, gt_ref):
    x = x_ref[...]
    ms = jnp.mean(x * x, axis=-1, keepdims=True)
    h = (x * lax.rsqrt(ms + RMS_EPS) * g_ref[...]).astype(BF16)
    h_ref[...] = h
    acc = _dot(h, w_ref[...]) + b_ref[...]
    o_ref[...] = acc
    gt_ref[...] = acc[:, acc.shape[1] - LANE:].T[:SUBLANE, :]


def _norm_proj(x, g, w, b, *, tm):
    t, d = x.shape
    n = w.shape[1]
    return pl.pallas_call(
        _norm_proj_kernel,
        out_shape=(jax.ShapeDtypeStruct((t, n), F32),
                   jax.ShapeDtypeStruct((t, d), BF16),
                   jax.ShapeDtypeStruct((SUBLANE, t), F32)),
        grid=(t // tm,),
        in_specs=[
            pl.BlockSpec((tm, d), lambda i: (i, 0)),
            pl.BlockSpec((1, d), lambda i: (0, 0)),
            pl.BlockSpec((d, n), lambda i: (0, 0)),
            pl.BlockSpec((1, n), lambda i: (0, 0)),
        ],
        out_specs=(pl.BlockSpec((tm, n), lambda i: (i, 0)),
                   pl.BlockSpec((tm, d), lambda i: (i, 0)),
                   pl.BlockSpec((SUBLANE, tm), lambda i: (0, i))),
        compiler_params=_cparams(("parallel",)),
        name="norm_proj",
    )(x, g, w, b)


def _proj_kernel(h_ref, w_ref, b_ref, o_ref, *, sigmoid_tiles, n_tiles):
    acc = _dot(h_ref[...], w_ref[...]) + b_ref[...]
    lo, hi = sigmoid_tiles
    if lo == 0 and hi == n_tiles:
        o_ref[...] = _sigmoid(acc).astype(o_ref.dtype)
    elif lo == hi:
        o_ref[...] = acc.astype(o_ref.dtype)
    else:
        j = pl.program_id(1)
        gated = jnp.logical_and(j >= lo, j < hi)

        @pl.when(gated)
        def _():
            o_ref[...] = _sigmoid(acc).astype(o_ref.dtype)

        @pl.when(jnp.logical_not(gated))
        def _():
            o_ref[...] = acc.astype(o_ref.dtype)


def _proj(h, w, b, *, sigmoid_tiles, tm, tn):
    t, d = h.shape
    n = w.shape[1]
    return pl.pallas_call(
        functools.partial(_proj_kernel, sigmoid_tiles=sigmoid_tiles, n_tiles=n // tn),
        out_shape=jax.ShapeDtypeStruct((t, n), BF16),
        grid=(t // tm, n // tn),
        in_specs=[
            pl.BlockSpec((tm, d), lambda i, j: (i, 0)),
            pl.BlockSpec((d, tn), lambda i, j: (0, j)),
            pl.BlockSpec((1, tn), lambda i, j: (0, j)),
        ],
        out_specs=pl.BlockSpec((tm, tn), lambda i, j: (i, j)),
        compiler_params=_cparams(("parallel", "arbitrary")),
        name="proj",
    )(h, w, b)


def _mlstm_kernel(qk_ref, v_ref, o_ref, ifc_ref, ifr_ref, cw_ref, cb_ref, nh_ref,
                  y_ref, ext_scr, c_scr, n_scr, m_scr):
    L = MLSTM_CHUNK
    step = pl.program_id(0)

    @pl.when(step == 0)
    def _():
        ext_scr[0:SUBLANE, :] = jnp.zeros((SUBLANE, ext_scr.shape[1]), F32)
        c_scr[...] = jnp.zeros_like(c_scr)
        n_scr[...] = jnp.zeros_like(n_scr)
        m_scr[...] = jnp.zeros_like(m_scr)

    raw = qk_ref[...]
    ext_scr[SUBLANE:SUBLANE + L, :] = raw
    conv = cb_ref[...]
    for j in range(CONV_W):
        off = SUBLANE - (CONV_W - 1) + j
        conv = conv + ext_scr[off:off + L, :] * cw_ref[j:j + 1, :]
    ext_scr[0:SUBLANE, :] = raw[L - SUBLANE:L, :]
    qk = conv * _sigmoid(conv)

    row = lax.broadcasted_iota(jnp.int32, (L, L), 0)
    col = lax.broadcasted_iota(jnp.int32, (L, L), 1)
    causal = row >= col
    tri_incl = jnp.where(causal, 1.0, 0.0).astype(BF16)
    tri_incl_t = jnp.where(col >= row, 1.0, 0.0).astype(BF16)

    hi, lo = _split_bf16(_log_sigmoid(ifc_ref[...]))
    a_cols = _dot(tri_incl, hi) + _dot(tri_incl, lo)
    hi, lo = _split_bf16(_log_sigmoid(ifr_ref[...]))
    a_rows = _dot(hi, tri_incl_t) + _dot(lo, tri_incl_t)

    for h in range(A_HEADS):
        q = qk[:, h * A_QK:(h + 1) * A_QK]
        k = qk[:, A_HEADS * A_QK + h * A_QK:A_HEADS * A_QK + (h + 1) * A_QK] * (A_QK ** -0.5)
        v = v_ref[:, h * A_V:(h + 1) * A_V]
        q_bf = q.astype(BF16)
        k_bf = k.astype(BF16)

        i_col = ifc_ref[:, h:h + 1]
        i_row = ifr_ref[h:h + 1, :]
        a_col = a_cols[:, A_HEADS + h:A_HEADS + h + 1]
        a_row = a_rows[A_HEADS + h:A_HEADS + h + 1, :]
        g_tot = a_col[L - 1:L, :]

        m_prev = m_scr[h]
        n_prev = n_scr[h]
        ct_prev = c_scr[h]

        m_inter = a_col + m_prev
        d_log = jnp.where(causal, a_col - a_row + i_row, -jnp.inf)
        m_t = jnp.maximum(m_inter, jnp.max(d_log, axis=-1, keepdims=True))
        p = jnp.exp(d_log - m_t) * _dot_nt(q_bf, k_bf)
        s_inter = jnp.exp(m_inter - m_t)
        num = s_inter * _dot(q_bf, ct_prev.astype(BF16)) + _dot(p.astype(BF16), v)
        den = (s_inter * jnp.sum(q * n_prev, axis=-1, keepdims=True)
               + jnp.sum(p, axis=-1, keepdims=True))
        hh = num / jnp.maximum(jnp.abs(den), jnp.exp(-m_t))

        hh = hh * lax.rsqrt(jnp.mean(hh * hh, axis=-1, keepdims=True) + RMS_EPS)
        gate = o_ref[:, h * A_V:(h + 1) * A_V].astype(F32)
        y_ref[:, h * A_V:(h + 1) * A_V] = (
            hh * nh_ref[:, h * A_V:(h + 1) * A_V] * gate).astype(y_ref.dtype)

        w_col = g_tot - a_col + i_col
        m_loc = jnp.max(w_col, axis=0, keepdims=True)
        ke = k * jnp.exp(w_col - m_loc)
        ct_loc = _dot_tn(ke.astype(BF16), v)
        n_loc = jnp.sum(ke, axis=0, keepdims=True)
        m_new = jnp.maximum(g_tot + m_prev, m_loc)
        s_old = jnp.exp(g_tot + m_prev - m_new)
        s_new = jnp.exp(m_loc - m_new)
        c_scr[h] = s_old * ct_prev + s_new * ct_loc
        n_scr[h] = s_old * n_prev + s_new * n_loc
        m_scr[h] = m_new


def _mlstm(p_f32, p_bf, if_row, conv_w, conv_b, norm_head):
    t = p_f32.shape[0]
    L = MLSTM_CHUNK
    wq = conv_w.shape[1]
    wv = norm_head.shape[1]
    return pl.pallas_call(
        _mlstm_kernel,
        out_shape=jax.ShapeDtypeStruct((t, wv), BF16),
        grid=(t // L,),
        in_specs=[
            pl.BlockSpec((L, wq), lambda i: (i, 0)),
            pl.BlockSpec((L, wv), lambda i: (i, 0)),
            pl.BlockSpec((L, wv), lambda i: (i, 1)),
            pl.BlockSpec((L, LANE), lambda i: (i, wq // LANE)),
            pl.BlockSpec((SUBLANE, L), lambda i: (0, i)),
            pl.BlockSpec((CONV_W, wq), lambda i: (0, 0)),
            pl.BlockSpec((1, wq), lambda i: (0, 0)),
            pl.BlockSpec((1, wv), lambda i: (0, 0)),
        ],
        out_specs=pl.BlockSpec((L, wv), lambda i: (i, 0)),
        scratch_shapes=[
            pltpu.VMEM((L + SUBLANE, wq), F32),
            pltpu.VMEM((A_HEADS, A_QK, A_V), F32),
            pltpu.VMEM((A_HEADS, 1, A_QK), F32),
            pltpu.VMEM((A_HEADS, 1, 1), F32),
        ],
        compiler_params=_cparams(("arbitrary",)),
        name="mlstm",
    )(p_f32, p_bf, p_bf, p_f32, if_row, conv_w, conv_b, norm_head)


def _later(n):
    j = lax.broadcasted_iota(jnp.int32, (n, n), 0)
    s = lax.broadcasted_iota(jnp.int32, (n, n), 1)
    return jnp.where(j > s, 1.0, 0.0).astype(BF16)


def _sb_logs(z):
    ls = jnp.minimum(z, 0.0) - jnp.log(1.0 + jnp.exp(-jnp.abs(z)))
    return ls, ls - z


def _sb_later_sums(lk, later2):
    hi, lo = _split_bf16(lk)
    return _dot(jnp.concatenate([hi, lo], axis=1), later2)


def _sb_tile(q, k, v, carry, later2):
    ls, lk = _sb_logs(_dot_nt(q, k))
    r_in = _sb_later_sums(lk, later2)
    w = jnp.exp(ls + r_in + carry)
    return _dot(w.astype(BF16), v), carry + r_in[:, 0:1] + lk[:, 0:1]


def _sb_kernel(q_ref, kc_ref, vc_ref, kp_ref, vp_ref, k_ref, v_ref, o_ref, acc_scr, carry_scr):
    sub = SB_SUB
    nsub = q_ref.shape[0] // sub
    step = pl.program_id(1)
    later = _later(sub)
    later2 = jnp.concatenate([later, later], axis=0)
    row = lax.broadcasted_iota(jnp.int32, (sub, sub), 0)
    col = lax.broadcasted_iota(jnp.int32, (sub, sub), 1)
    diag_past = col < row
    has_prev = step > 0

    blk = [slice(j * sub, (j + 1) * sub) for j in range(nsub)]
    qs = [q_ref[b, :] for b in blk]
    k_d = [kc_ref[b, :] for b in blk]
    v_d = [vc_ref[b, :] for b in blk]
    k_p = [kp_ref[...]] + k_d[:-1]
    v_p = [vp_ref[...]] + v_d[:-1]
    ls_d, lk_d, ls_p, lk_p = [], [], [], []
    for j in range(nsub):
        ls, lk = _sb_logs(_dot_nt(qs[j], k_d[j]))
        ls_d.append(ls)
        lk_d.append(jnp.where(diag_past, lk, 0.0))
        ls, lk = _sb_logs(_dot_nt(qs[j], k_p[j]))
        ls_p.append(ls)
        lk_p.append(jnp.where(has_prev, lk, 0.0) if j == 0 else lk)
    r_in = _sb_later_sums(jnp.concatenate(lk_d + lk_p, axis=0), later2)
    first_max = []
    for j in range(nsub):
        r_d = r_in[j * sub:(j + 1) * sub]
        r_p = r_in[(nsub + j) * sub:(nsub + j + 1) * sub]
        carry_d = r_d[:, 0:1] + lk_d[j][:, 0:1]
        w_d = jnp.where(diag_past, jnp.exp(ls_d[j] + r_d), 0.0)
        w_p = jnp.exp(ls_p[j] + r_p + carry_d)
        if j == 0:
            w_p = jnp.where(has_prev, w_p, 0.0)
        carry = carry_d + r_p[:, 0:1] + lk_p[j][:, 0:1]
        acc_scr[j] = _dot(w_d.astype(BF16), v_d[j]) + _dot(w_p.astype(BF16), v_p[j])
        carry_scr[j] = carry
        first_max.append(jnp.max(carry))

    for j in range(nsub):
        def cond(state):
            kb, mx = state
            return jnp.logical_and(kb >= 0, mx > -SB_SKIP)

        def body(state, j=j):
            kb, _ = state
            k0 = pl.multiple_of(kb * sub, sub)
            acc, carry = _sb_tile(q_ref[j * sub:(j + 1) * sub, :], k_ref[pl.ds(k0, sub), :],
                                  v_ref[pl.ds(k0, sub), :], carry_scr[j], later2)
            acc_scr[j] += acc
            carry_scr[j] = carry
            return kb - 1, jnp.max(carry)

        lax.while_loop(cond, body, (step * nsub + j - 2, first_max[j]))
        o_ref[j * sub:(j + 1) * sub, :] = acc_scr[j].astype(o_ref.dtype)


def _stick_breaking(qkv, q_col, k_col, v_col):
    t = qkv.shape[0]
    d = B_HEAD_DIM
    bq = min(SB_BQ, t)
    nsub = bq // SB_SUB
    assert t % bq == 0 and bq % SB_SUB == 0

    def cur(col):
        return pl.BlockSpec((bq, d), lambda h, i: (i, col + h))

    def prev(col):
        return pl.BlockSpec((SB_SUB, d), lambda h, i: (jnp.maximum(i * nsub - 1, 0), col + h))

    def full(col):
        return pl.BlockSpec((t, d), lambda h, i: (0, col + h))

    return pl.pallas_call(
        _sb_kernel,
        out_shape=jax.ShapeDtypeStruct((t, B_HEADS * d), BF16),
        grid=(B_HEADS, t // bq),
        in_specs=[cur(q_col), cur(k_col), cur(v_col), prev(k_col), prev(v_col),
                  full(k_col), full(v_col)],
        out_specs=pl.BlockSpec((bq, d), lambda h, i: (i, h)),
        scratch_shapes=[pltpu.VMEM((nsub, SB_SUB, d), F32), pltpu.VMEM((nsub, SB_SUB, 1), F32)],
        compiler_params=_cparams(("parallel", "arbitrary")),
        name="stick_breaking",
    )(qkv, qkv, qkv, qkv, qkv, qkv, qkv)


def _mix_kernel(g1_ref, g2_ref, ya_ref, wa_ref, yb_ref, wb_ref, o_ref):
    pa = _dot(ya_ref[...], wa_ref[...])
    pb = _dot(yb_ref[...], wb_ref[...])
    o_ref[...] = (g1_ref[...].astype(F32) * pa + g2_ref[...].astype(F32) * pb).astype(o_ref.dtype)


def _mix(gates, y_a, w_a, y_b, w_b, *, tm, tn):
    t = y_a.shape[0]
    d = w_a.shape[1]
    nt = d // tn
    return pl.pallas_call(
        _mix_kernel,
        out_shape=jax.ShapeDtypeStruct((t, d), BF16),
        grid=(t // tm, nt),
        in_specs=[
            pl.BlockSpec((tm, tn), lambda i, j: (i, j)),
            pl.BlockSpec((tm, tn), lambda i, j: (i, j + nt)),
            pl.BlockSpec((tm, y_a.shape[1]), lambda i, j: (i, 0)),
            pl.BlockSpec((w_a.shape[0], tn), lambda i, j: (0, j)),
            pl.BlockSpec((tm, y_b.shape[1]), lambda i, j: (i, 0)),
            pl.BlockSpec((w_b.shape[0], tn), lambda i, j: (0, j)),
        ],
        out_specs=pl.BlockSpec((tm, tn), lambda i, j: (i, j)),
        compiler_params=_cparams(("parallel", "arbitrary")),
        name="mix",
    )(gates, gates, y_a, w_a, y_b, w_b)


def _out_kernel(x_ref, mix_ref, wo_ref, g_ref, wrh_ref, wrl_ref, br_ref,
                x1_ref, h2_ref, ri_ref, rg_ref, cnt_ref):
    tm = x_ref.shape[0]
    x1 = x_ref[...] + _dot(mix_ref[...], wo_ref[...])
    x1_ref[...] = x1
    ms = jnp.mean(x1 * x1, axis=-1, keepdims=True)
    h2 = x1 * lax.rsqrt(ms + RMS_EPS) * g_ref[...]
    h_hi, h_lo = _split_bf16(h2)
    h2_ref[...] = h_hi
    w_hi = wrh_ref[...]
    lg = (_dot_nt(w_hi, h_hi) + _dot_nt(w_hi, h_lo) + _dot_nt(wrl_ref[...], h_hi)
          + br_ref[:, 0:1])

    @pl.when(pl.program_id(0) == 0)
    def _():
        cnt_ref[...] = jnp.zeros_like(cnt_ref)

    row = lax.broadcasted_iota(jnp.int32, (N_EXPERTS, tm), 0)
    vals, idxs, hots = [], [], []
    for _ in range(TOP_K):
        mx = jnp.max(lg, axis=0, keepdims=True)
        idx = jnp.min(jnp.where(lg == mx, row, N_EXPERTS), axis=0, keepdims=True)
        hot = row == idx
        vals.append(mx)
        idxs.append(idx)
        hots.append(hot)
        lg = jnp.where(hot, -jnp.inf, lg)

    exps = [jnp.exp(v - vals[0]) for v in vals]
    denom = exps[0]
    for e in exps[1:]:
        denom = denom + e

    chosen = hots[0]
    for hot in hots[1:]:
        chosen = jnp.logical_or(chosen, hot)
    chosen_f = jnp.where(chosen, 1.0, 0.0)
    r = lax.broadcasted_iota(jnp.int32, (tm, tm), 0)
    c = lax.broadcasted_iota(jnp.int32, (tm, tm), 1)
    earlier = jnp.where(r < c, 1.0, 0.0).astype(BF16)
    before = _dot(chosen_f.astype(BF16), earlier) + cnt_ref[:, 0:1]
    cnt_ref[...] += jnp.sum(chosen_f, axis=1, keepdims=True)

    row8 = lax.broadcasted_iota(jnp.int32, (2 * TOP_K, tm), 0)
    ri = jnp.zeros((2 * TOP_K, tm), jnp.int32)
    rg = jnp.zeros((2 * TOP_K, tm), F32)
    for k in range(TOP_K):
        rank = jnp.sum(jnp.where(hots[k], before, 0.0), axis=0, keepdims=True).astype(jnp.int32)
        ri = jnp.where(row8 == k, idxs[k], ri)
        ri = jnp.where(row8 == TOP_K + k, rank, ri)
        rg = jnp.where(row8 == k, exps[k] / denom, rg)
    ri_ref[...] = ri
    rg_ref[...] = jnp.concatenate([rg, jnp.zeros((LANE - 2 * TOP_K, tm), F32)], axis=0).T


def _out_proj(x, mixed, w_o, g, wr_hi, wr_lo, b_r, *, tm):
    t, d = x.shape
    ne = wr_hi.shape[0]
    return pl.pallas_call(
        _out_kernel,
        out_shape=(jax.ShapeDtypeStruct((t, d), F32),
                   jax.ShapeDtypeStruct((t, d), BF16),
                   jax.ShapeDtypeStruct((2 * TOP_K, t), jnp.int32),
                   jax.ShapeDtypeStruct((t, LANE), F32),
                   jax.ShapeDtypeStruct((ne, LANE), F32)),
        grid=(t // tm,),
        in_specs=[
            pl.BlockSpec((tm, d), lambda i: (i, 0)),
            pl.BlockSpec((tm, d), lambda i: (i, 0)),
            pl.BlockSpec((d, d), lambda i: (0, 0)),
            pl.BlockSpec((1, d), lambda i: (0, 0)),
            pl.BlockSpec((ne, d), lambda i: (0, 0)),
            pl.BlockSpec((ne, d), lambda i: (0, 0)),
            pl.BlockSpec((ne, LANE), lambda i: (0, 0)),
        ],
        out_specs=(pl.BlockSpec((tm, d), lambda i: (i, 0)),
                   pl.BlockSpec((tm, d), lambda i: (i, 0)),
                   pl.BlockSpec((2 * TOP_K, tm), lambda i: (0, i)),
                   pl.BlockSpec((tm, LANE), lambda i: (i, 0)),
                   pl.BlockSpec((ne, LANE), lambda i: (0, 0))),
        compiler_params=_cparams(("arbitrary",)),
        name="out_proj",
    )(x, mixed, w_o, g, wr_hi, wr_lo, b_r)


def _expert_changed(be_ref, m):
    return jnp.logical_or(m == 0, be_ref[m] != be_ref[jnp.maximum(m - 1, 0)])


def _next_run_expert(block_expert, n_used):
    nb = block_expert.shape[0]
    idx = jnp.arange(nb, dtype=jnp.int32)
    starts = jnp.logical_or(idx == 0, block_expert != jnp.roll(block_expert, 1))
    cand = jnp.where(jnp.logical_and(starts, idx < n_used[0]), idx, nb)
    later_start = jnp.concatenate([lax.cummin(cand, axis=0, reverse=True)[1:],
                                   jnp.full((1,), nb, jnp.int32)])
    at_start = jnp.sum(jnp.where(idx[None, :] == later_start[:, None], block_expert[None, :], 0), axis=1)
    return jnp.where(later_start < nb, at_start, -1).astype(jnp.int32)


def _moe_up_kernel(be_ref, nu_ref, nx_ref, x_ref, w_hbm, bg_ref, bu_ref, o_ref,
                   stage, w_scr, sem):
    n = pl.program_id(0)
    m = pl.program_id(1)
    nf = pl.num_programs(0)
    tf = w_scr.shape[2]

    def fetch(e, nn, part):
        col = pl.multiple_of((nn + part * nf) * tf, tf)
        return pltpu.make_async_copy(w_hbm.at[e, :, pl.ds(col, tf)], stage.at[part], sem.at[part])

    def start(e, nn):
        fetch(e, nn, 0).start()
        fetch(e, nn, 1).start()

    @pl.when(jnp.logical_and(jnp.logical_and(n == 0, m == 0), nu_ref[0] > 0))
    def _():
        start(be_ref[0], 0)

    @pl.when(m < nu_ref[0])
    def _():
        @pl.when(_expert_changed(be_ref, m))
        def _():
            fetch(be_ref[m], n, 0).wait()
            fetch(be_ref[m], n, 1).wait()
            def convert(i, carry):
                rows = pl.ds(pl.multiple_of(i * CAST_ROWS, CAST_ROWS), CAST_ROWS)
                w_scr[:, rows, :] = stage[:, rows, :].astype(BF16)
                return carry

            lax.fori_loop(0, stage.shape[1] // CAST_ROWS, convert, 0)
            nxt = nx_ref[m]

            @pl.when(nxt >= 0)
            def _():
                start(nxt, n)

            @pl.when(jnp.logical_and(nxt < 0, n + 1 < nf))
            def _():
                start(be_ref[0], n + 1)

        x = x_ref[...]
        gate = jnp.minimum(_dot(x, w_scr[0]) + bg_ref[0], SWIGLU_LIMIT)
        up = jnp.clip(_dot(x, w_scr[1]) + bu_ref[0], -SWIGLU_LIMIT, SWIGLU_LIMIT)
        o_ref[...] = ((up + 1.0) * gate * _sigmoid(SWIGLU_ALPHA * gate)).astype(o_ref.dtype)

    @pl.when(m >= nu_ref[0])
    def _():
        o_ref[...] = jnp.zeros_like(o_ref)


def _moe_up(block_expert, n_used, x_pad, w_gu, b_gu, *, tf):
    rows, d = x_pad.shape
    nb = rows // MOE_TM
    d_ff = w_gu.shape[2] // 2
    nf = d_ff // tf

    def blk(m, nu):
        return jnp.maximum(jnp.minimum(m, nu[0] - 1), 0)

    return pl.pallas_call(
        _moe_up_kernel,
        out_shape=jax.ShapeDtypeStruct((rows, d_ff), BF16),
        grid_spec=pltpu.PrefetchScalarGridSpec(
            num_scalar_prefetch=3,
            grid=(nf, nb),
            in_specs=[
                pl.BlockSpec((MOE_TM, d), lambda n, m, be, nu, nx: (blk(m, nu), 0)),
                pl.BlockSpec(memory_space=pl.ANY),
                pl.BlockSpec((1, 1, tf), lambda n, m, be, nu, nx: (be[blk(m, nu)], 0, n)),
                pl.BlockSpec((1, 1, tf), lambda n, m, be, nu, nx: (be[blk(m, nu)], 0, n + nf)),
            ],
            out_specs=pl.BlockSpec((MOE_TM, tf), lambda n, m, be, nu, nx: (m, n)),
            scratch_shapes=[pltpu.VMEM((2, d, tf), F32), pltpu.VMEM((2, d, tf), BF16),
                            pltpu.SemaphoreType.DMA((2,))],
        ),
        compiler_params=_cparams(("arbitrary", "arbitrary")),
        name="moe_up",
    )(block_expert, n_used, _next_run_expert(block_expert, n_used), x_pad, w_gu, b_gu, b_gu)


def _moe_down_kernel(be_ref, nu_ref, nx_ref, *refs, chunk_starts):
    a_refs, (w_hbm, bd_ref, o_ref, stage, wd_scr, sem) = refs[:-6], refs[-6:]
    m = pl.program_id(0)

    def fetch(e):
        return pltpu.make_async_copy(w_hbm.at[e], stage, sem.at[0])

    @pl.when(jnp.logical_and(m == 0, nu_ref[0] > 0))
    def _():
        fetch(be_ref[0]).start()

    @pl.when(m < nu_ref[0])
    def _():
        @pl.when(_expert_changed(be_ref, m))
        def _():
            fetch(be_ref[m]).wait()
            wd_scr[...] = stage[...].astype(BF16)
            nxt = nx_ref[m]

            @pl.when(nxt >= 0)
            def _():
                fetch(nxt).start()

        for c, a_ref in enumerate(a_refs):
            @pl.when(jnp.logical_and(m >= chunk_starts[c], m < chunk_starts[c + 1]))
            def _(a_ref=a_ref):
                o_ref[...] = (_dot(a_ref[...], wd_scr[...]) + bd_ref[0]).astype(o_ref.dtype)

    @pl.when(m >= nu_ref[0])
    def _():
        o_ref[...] = jnp.zeros_like(o_ref)


def _moe_down(block_expert, n_used, acts, w_d, b_d):
    d_ff = acts[0].shape[1]
    chunk_starts = [0]
    for a in acts:
        chunk_starts.append(chunk_starts[-1] + a.shape[0] // MOE_TM)
    nb = chunk_starts[-1]
    d = w_d.shape[2]

    def blk(m, nu):
        return jnp.maximum(jnp.minimum(m, nu[0] - 1), 0)

    def act_spec(c):
        lo, n_c = chunk_starts[c], chunk_starts[c + 1] - chunk_starts[c]
        return pl.BlockSpec((MOE_TM, d_ff),
                            lambda m, be, nu, nx: (jnp.clip(blk(m, nu) - lo, 0, n_c - 1), 0))

    return pl.pallas_call(
        functools.partial(_moe_down_kernel, chunk_starts=tuple(chunk_starts)),
        out_shape=jax.ShapeDtypeStruct((nb * MOE_TM, d), BF16),
        grid_spec=pltpu.PrefetchScalarGridSpec(
            num_scalar_prefetch=3,
            grid=(nb,),
            in_specs=[act_spec(c) for c in range(len(acts))] + [
                pl.BlockSpec(memory_space=pl.ANY),
                pl.BlockSpec((1, 1, d), lambda m, be, nu, nx: (be[blk(m, nu)], 0, 0)),
            ],
            out_specs=pl.BlockSpec((MOE_TM, d), lambda m, be, nu, nx: (m, 0)),
            scratch_shapes=[pltpu.VMEM((d_ff, d), F32), pltpu.VMEM((d_ff, d), BF16),
                            pltpu.SemaphoreType.DMA((1,))],
        ),
        compiler_params=_cparams(("arbitrary",)),
        name="moe_down",
    )(block_expert, n_used, _next_run_expert(block_expert, n_used), *acts, w_d, b_d)


def _combine_kernel(x_ref, y0_ref, y1_ref, y2_ref, y3_ref, rg_ref, g_ref, o_ref, *, normalize):
    x = x_ref[...]
    rg = rg_ref[...]
    for k, y_ref in enumerate((y0_ref, y1_ref, y2_ref, y3_ref)):
        x = x + rg[:, k:k + 1] * y_ref[...].astype(F32)
    if normalize:
        ms = jnp.mean(x * x, axis=-1, keepdims=True)
        x = x * lax.rsqrt(ms + RMS_EPS) * g_ref[...]
    o_ref[...] = x


def _combine(x_buf, y_rows, rg, g, *, chunk, n_chunk_tokens, normalize, tm):
    t, d = x_buf.shape
    nblk = n_chunk_tokens // tm
    x_spec = pl.BlockSpec((tm, d), lambda i: (chunk * nblk + i, 0))

    def y_spec(k):
        return pl.BlockSpec((tm, d), lambda i: (k * nblk + i, 0))

    return pl.pallas_call(
        functools.partial(_combine_kernel, normalize=normalize),
        out_shape=jax.ShapeDtypeStruct((t, d), F32),
        grid=(nblk,),
        in_specs=[x_spec] + [y_spec(k) for k in range(TOP_K)] + [
            pl.BlockSpec((tm, rg.shape[1]), lambda i: (chunk * nblk + i, 0)),
            pl.BlockSpec((1, d), lambda i: (0, 0)),
        ],
        out_specs=x_spec,
        input_output_aliases={0: 0},
        compiler_params=_cparams(("parallel",)),
        name="combine_norm",
    )(x_buf, y_rows, y_rows, y_rows, y_rows, rg, g)


def _route(ri, counts, t):
    p_n = t * TOP_K
    experts = ri[:TOP_K]
    rank = ri[TOP_K:]
    counts = counts[:, 0].astype(jnp.int32)
    padded = (counts + MOE_TM - 1) // MOE_TM * MOE_TM
    pad_end = jnp.cumsum(padded)
    pad_start = pad_end - padded
    e_ids = jnp.arange(N_EXPERTS, dtype=jnp.int32)[:, None, None]
    dest = rank + jnp.sum(jnp.where(experts[None] == e_ids, pad_start[:, None, None], 0), axis=0)
    nb = -(-p_n // MOE_TM) + N_EXPERTS
    src_tok = (jnp.arange(nb * MOE_TM, dtype=jnp.int32) % t).at[dest.reshape(p_n)].set(
        jnp.tile(jnp.arange(t, dtype=jnp.int32), TOP_K), mode="promise_in_bounds",
        unique_indices=True)
    block_start = jnp.arange(nb, dtype=jnp.int32) * MOE_TM
    block_expert = jnp.minimum(
        jnp.sum((block_start[:, None] >= pad_end[None, :]).astype(jnp.int32), axis=1),
        N_EXPERTS - 1)
    n_used = (pad_end[-1] // MOE_TM).astype(jnp.int32).reshape(1)
    return dest, src_tok, block_expert, n_used


def kernel(x, norm_mix, w_in, conv_w, conv_b, b_gates_if, norm_head, w_proj_a, w_proj_b,
           w_merge_gate, b_merge_gate, w_out, norm_ffn, w_router, b_router, w_gate_up,
           b_gate_up, w_down, b_down, norm_final):
    bn, s, d = x.shape
    assert bn == 1
    depth = norm_mix.shape[0]
    xt = x.reshape(s, d)
    tm = min(1024, s)
    aqk = A_HEADS * A_QK
    av = A_HEADS * A_V
    bw = B_HEADS * B_HEAD_DIM

    for l in range(depth):
        wl = w_in[l]
        c0 = 2 * aqk
        c1 = c0 + 2 * av
        c2 = c1 + 2 * A_HEADS
        w_f32 = jnp.concatenate(
            [wl[:, :c0], wl[:, c1:c2], jnp.zeros((d, LANE - 2 * A_HEADS), F32)], axis=1).astype(BF16)
        b_f32 = jnp.concatenate(
            [jnp.zeros((c0,), F32), b_gates_if[l], jnp.zeros((LANE - 2 * A_HEADS,), F32)])[None, :]
        w_bf = jnp.concatenate([wl[:, c0:c1], wl[:, c2:c2 + bw] * (B_HEAD_DIM ** -0.5), wl[:, c2 + bw:]],
                               axis=1).astype(BF16)
        g_mix = norm_mix[l][None, :]

        p_f32, h1, if_row = _norm_proj(xt, g_mix, w_f32, b_f32, tm=tm)
        tn = 1024
        o_tile = av // tn
        p_bf = _proj(h1, w_bf, jnp.zeros((1, w_bf.shape[1]), F32),
                     sigmoid_tiles=(o_tile, 2 * o_tile), tm=tm, tn=tn)
        n_gate_tiles = w_merge_gate.shape[2] // tn
        gates = _proj(h1, w_merge_gate[l].astype(BF16), b_merge_gate[l][None, :],
                      sigmoid_tiles=(0, n_gate_tiles), tm=tm, tn=tn)

        y_a = _mlstm(p_f32, p_bf, if_row, conv_w[l], conv_b[l][None, :], norm_head[l][None, :])
        qb = 2 * av // B_HEAD_DIM
        y_b = _stick_breaking(p_bf, qb, qb + B_HEADS, qb + 2 * B_HEADS)

        mixed = _mix(gates, y_a, w_proj_a[l].astype(BF16), y_b, w_proj_b[l].astype(BF16),
                     tm=tm, tn=d)
        wr_hi, wr_lo = _split_bf16(w_router[l].T)
        b_r = jnp.broadcast_to(b_router[l][:, None], (N_EXPERTS, LANE))
        x1, h2, ri, rg, counts = _out_proj(xt, mixed, w_out[l].astype(BF16), norm_ffn[l][None, :],
                                           wr_hi, wr_lo, b_r, tm=min(512, s))

        dest, src_tok, block_expert, n_used = _route(ri, counts, s)
        nb = block_expert.shape[0]
        bounds = [0] + [nb * f // 10 for f in MOE_CHUNK_TENTHS] + [nb]
        acts = []
        for lo, hi in zip(bounds[:-1], bounds[1:]):
            x_pad = h2.at[src_tok[lo * MOE_TM:hi * MOE_TM]].get(mode="promise_in_bounds")
            acts.append(_moe_up(block_expert[lo:hi], jnp.clip(n_used - lo, 0, hi - lo),
                                x_pad, w_gate_up[l], b_gate_up[l][:, None, :], tf=1024))
        y_pad = _moe_down(block_expert, n_used, acts, w_down[l], b_down[l][:, None, :])
        nct = s // COMBINE_CHUNKS
        xt = x1
        for j in range(COMBINE_CHUNKS):
            idx = dest[:, j * nct:(j + 1) * nct].reshape(TOP_K * nct)
            y_rows = y_pad.at[idx].get(mode="promise_in_bounds")
            xt = _combine(xt, y_rows, rg, norm_final[None, :], chunk=j, n_chunk_tokens=nct,
                          normalize=(l + 1 == depth), tm=min(512, nct))
    return xt.reshape(bn, s, d)
```

```python
import functools

import jax
import jax.numpy as jnp
from jax import lax
from jax.experimental import pallas as pl
from jax.experimental.pallas import tpu as pltpu

F32 = jnp.float32
BF16 = jnp.bfloat16

RMS_EPS = 1e-5
A_HEADS = 4
A_QK = 128
A_V = 256
CONV_W = 4
B_HEADS = 8
B_HEAD_DIM = 128
N_EXPERTS = 32
TOP_K = 4
SWIGLU_LIMIT = 7.0
SWIGLU_ALPHA = 1.702

LANE = 128
SUBLANE = 8
VMEM_LIMIT = 60 * 1024 * 1024

MLSTM_CHUNK = 128
SB_BQ = 4096
SB_SUB = 128
SB_SKIP = 88.0
MOE_TM = 512
CAST_ROWS = 32
MOE_CHUNK_TENTHS = (1, 4)
COMBINE_CHUNKS = 8


def _cparams(sem):
    return pltpu.CompilerParams(dimension_semantics=sem, vmem_limit_bytes=VMEM_LIMIT)


def _log_sigmoid(z):
    return jnp.minimum(z, 0.0) - jnp.log1p(jnp.exp(-jnp.abs(z)))


def _sigmoid(z):
    return 1.0 / (1.0 + jnp.exp(-z))


def _split_bf16(x):
    hi = x.astype(BF16)
    lo = (x - hi.astype(F32)).astype(BF16)
    return hi, lo


def _dot(a, b):
    return jnp.dot(a, b, preferred_element_type=F32)


def _dot_nt(a, b):
    return lax.dot_general(a, b, (((1,), (1,)), ((), ())), preferred_element_type=F32)


def _dot_tn(a, b):
    return lax.dot_general(a, b, (((0,), (0,)), ((), ())), preferred_element_type=F32)


def _norm_proj_kernel(x_ref, g_ref, w_ref, b_ref, o_ref, h_ref, gt_ref):
    x = x_ref[...]
    ms = jnp.mean(x * x, axis=-1, keepdims=True)
    h = (x * lax.rsqrt(ms + RMS_EPS) * g_ref[...]).astype(BF16)
    h_ref[...] = h
    acc = _dot(h, w_ref[...]) + b_ref[...]
    o_ref[...] = acc
    gt_ref[...] = acc[:, acc.shape[1] - LANE:].T[:SUBLANE, :]


def _norm_proj(x, g, w, b, *, tm):
    t, d = x.shape
    n = w.shape[1]
    return pl.pallas_call(
        _norm_proj_kernel,
        out_shape=(jax.ShapeDtypeStruct((t, n), F32),
                   jax.ShapeDtypeStruct((t, d), BF16),
                   jax.ShapeDtypeStruct((SUBLANE, t), F32)),
        grid=(t // tm,),
        in_specs=[
            pl.BlockSpec((tm, d), lambda i: (i, 0)),
            pl.BlockSpec((1, d), lambda i: (0, 0)),
            pl.BlockSpec((d, n), lambda i: (0, 0)),
            pl.BlockSpec((1, n), lambda i: (0, 0)),
        ],
        out_specs=(pl.BlockSpec((tm, n), lambda i: (i, 0)),
                   pl.BlockSpec((tm, d), lambda i: (i, 0)),
                   pl.BlockSpec((SUBLANE, tm), lambda i: (0, i))),
        compiler_params=_cparams(("parallel",)),
        name="norm_proj",
    )(x, g, w, b)


def _proj_kernel(h_ref, w_ref, b_ref, o_ref, *, sigmoid_tiles, n_tiles):
    acc = _dot(h_ref[...], w_ref[...]) + b_ref[...]
    lo, hi = sigmoid_tiles
    if lo == 0 and hi == n_tiles:
        o_ref[...] = _sigmoid(acc).astype(o_ref.dtype)
    elif lo == hi:
        o_ref[...] = acc.astype(o_ref.dtype)
    else:
        j = pl.program_id(1)
        gated = jnp.logical_and(j >= lo, j < hi)

        @pl.when(gated)
        def _():
            o_ref[...] = _sigmoid(acc).astype(o_ref.dtype)

        @pl.when(jnp.logical_not(gated))
        def _():
            o_ref[...] = acc.astype(o_ref.dtype)


def _proj(h, w, b, *, sigmoid_tiles, tm, tn):
    t, d = h.shape
    n = w.shape[1]
    return pl.pallas_call(
        functools.partial(_proj_kernel, sigmoid_tiles=sigmoid_tiles, n_tiles=n // tn),
        out_shape=jax.ShapeDtypeStruct((t, n), BF16),
        grid=(t // tm, n // tn),
        in_specs=[
            pl.BlockSpec((tm, d), lambda i, j: (i, 0)),
            pl.BlockSpec((d, tn), lambda i, j: (0, j)),
            pl.BlockSpec((1, tn), lambda i, j: (0, j)),
        ],
        out_specs=pl.BlockSpec((tm, tn), lambda i, j: (i, j)),
        compiler_params=_cparams(("parallel", "arbitrary")),
        name="proj",
    )(h, w, b)


def _mlstm_kernel(qk_ref, v_ref, o_ref, ifc_ref, ifr_ref, cw_ref, cb_ref, nh_ref,
                  y_ref, ext_scr, c_scr, n_scr, m_scr):
    L = MLSTM_CHUNK
    step = pl.program_id(0)

    @pl.when(step == 0)
    def _():
        ext_scr[0:SUBLANE, :] = jnp.zeros((SUBLANE, ext_scr.shape[1]), F32)
        c_scr[...] = jnp.zeros_like(c_scr)
        n_scr[...] = jnp.zeros_like(n_scr)
        m_scr[...] = jnp.zeros_like(m_scr)

    raw = qk_ref[...]
    ext_scr[SUBLANE:SUBLANE + L, :] = raw
    conv = cb_ref[...]
    for j in range(CONV_W):
        off = SUBLANE - (CONV_W - 1) + j
        conv = conv + ext_scr[off:off + L, :] * cw_ref[j:j + 1, :]
    ext_scr[0:SUBLANE, :] = raw[L - SUBLANE:L, :]
    qk = conv * _sigmoid(conv)

    row = lax.broadcasted_iota(jnp.int32, (L, L), 0)
    col = lax.broadcasted_iota(jnp.int32, (L, L), 1)
    causal = row >= col
    tri_incl = jnp.where(causal, 1.0, 0.0).astype(BF16)
    tri_incl_t = jnp.where(col >= row, 1.0, 0.0).astype(BF16)

    hi, lo = _split_bf16(_log_sigmoid(ifc_ref[...]))
    a_cols = _dot(tri_incl, hi) + _dot(tri_incl, lo)
    hi, lo = _split_bf16(_log_sigmoid(ifr_ref[...]))
    a_rows = _dot(hi, tri_incl_t) + _dot(lo, tri_incl_t)

    for h in range(A_HEADS):
        q = qk[:, h * A_QK:(h + 1) * A_QK]
        k = qk[:, A_HEADS * A_QK + h * A_QK:A_HEADS * A_QK + (h + 1) * A_QK] * (A_QK ** -0.5)
        v = v_ref[:, h * A_V:(h + 1) * A_V]
        q_bf = q.astype(BF16)
        k_bf = k.astype(BF16)

        i_col = ifc_ref[:, h:h + 1]
        i_row = ifr_ref[h:h + 1, :]
        a_col = a_cols[:, A_HEADS + h:A_HEADS + h + 1]
        a_row = a_rows[A_HEADS + h:A_HEADS + h + 1, :]
        g_tot = a_col[L - 1:L, :]

        m_prev = m_scr[h]
        n_prev = n_scr[h]
        ct_prev = c_scr[h]

        m_inter = a_col + m_prev
        d_log = jnp.where(causal, a_col - a_row + i_row, -jnp.inf)
        m_t = jnp.maximum(m_inter, jnp.max(d_log, axis=-1, keepdims=True))
        p = jnp.exp(d_log - m_t) * _dot_nt(q_bf, k_bf)
        s_inter = jnp.exp(m_inter - m_t)
        num = s_inter * _dot(q_bf, ct_prev.astype(BF16)) + _dot(p.astype(BF16), v)
        den = (s_inter * jnp.sum(q * n_prev, axis=-1, keepdims=True)
               + jnp.sum(p, axis=-1, keepdims=True))
        hh = num / jnp.maximum(jnp.abs(den), jnp.exp(-m_t))

        hh = hh * lax.rsqrt(jnp.mean(hh * hh, axis=-1, keepdims=True) + RMS_EPS)
        gate = o_ref[:, h * A_V:(h + 1) * A_V].astype(F32)
        y_ref[:, h * A_V:(h + 1) * A_V] = (
            hh * nh_ref[:, h * A_V:(h + 1) * A_V] * gate).astype(y_ref.dtype)

        w_col = g_tot - a_col + i_col
        m_loc = jnp.max(w_col, axis=0, keepdims=True)
        ke = k * jnp.exp(w_col - m_loc)
        ct_loc = _dot_tn(ke.astype(BF16), v)
        n_loc = jnp.sum(ke, axis=0, keepdims=True)
        m_new = jnp.maximum(g_tot + m_prev, m_loc)
        s_old = jnp.exp(g_tot + m_prev - m_new)
        s_new = jnp.exp(m_loc - m_new)
        c_scr[h] = s_old * ct_prev + s_new * ct_loc
        n_scr[h] = s_old * n_prev + s_new * n_loc
        m_scr[h] = m_new


def _mlstm(p_f32, p_bf, if_row, conv_w, conv_b, norm_head):
    t = p_f32.shape[0]
    L = MLSTM_CHUNK
    wq = conv_w.shape[1]
    wv = norm_head.shape[1]
    return pl.pallas_call(
        _mlstm_kernel,
        out_shape=jax.ShapeDtypeStruct((t, wv), BF16),
        grid=(t // L,),
        in_specs=[
            pl.BlockSpec((L, wq), lambda i: (i, 0)),
            pl.BlockSpec((L, wv), lambda i: (i, 0)),
            pl.BlockSpec((L, wv), lambda i: (i, 1)),
            pl.BlockSpec((L, LANE), lambda i: (i, wq // LANE)),
            pl.BlockSpec((SUBLANE, L), lambda i: (0, i)),
            pl.BlockSpec((CONV_W, wq), lambda i: (0, 0)),
            pl.BlockSpec((1, wq), lambda i: (0, 0)),
            pl.BlockSpec((1, wv), lambda i: (0, 0)),
        ],
        out_specs=pl.BlockSpec((L, wv), lambda i: (i, 0)),
        scratch_shapes=[
            pltpu.VMEM((L + SUBLANE, wq), F32),
            pltpu.VMEM((A_HEADS, A_QK, A_V), F32),
            pltpu.VMEM((A_HEADS, 1, A_QK), F32),
            pltpu.VMEM((A_HEADS, 1, 1), F32),
        ],
        compiler_params=_cparams(("arbitrary",)),
        name="mlstm",
    )(p_f32, p_bf, p_bf, p_f32, if_row, conv_w, conv_b, norm_head)


def _later(n):
    j = lax.broadcasted_iota(jnp.int32, (n, n), 0)
    s = lax.broadcasted_iota(jnp.int32, (n, n), 1)
    return jnp.where(j > s, 1.0, 0.0).astype(BF16)


def _sb_logs(z):
    ls = jnp.minimum(z, 0.0) - jnp.log(1.0 + jnp.exp(-jnp.abs(z)))
    return ls, ls - z


def _sb_later_sums(lk, later2):
    hi, lo = _split_bf16(lk)
    return _dot(jnp.concatenate([hi, lo], axis=1), later2)


def _sb_tile(q, k, v, carry, later2):
    ls, lk = _sb_logs(_dot_nt(q, k))
    r_in = _sb_later_sums(lk, later2)
    w = jnp.exp(ls + r_in + carry)
    return _dot(w.astype(BF16), v), carry + r_in[:, 0:1] + lk[:, 0:1]


def _sb_kernel(q_ref, kc_ref, vc_ref, kp_ref, vp_ref, k_ref, v_ref, o_ref, acc_scr, carry_scr):
    sub = SB_SUB
    nsub = q_ref.shape[0] // sub
    step = pl.program_id(1)
    later = _later(sub)
    later2 = jnp.concatenate([later, later], axis=0)
    row = lax.broadcasted_iota(jnp.int32, (sub, sub), 0)
    col = lax.broadcasted_iota(jnp.int32, (sub, sub), 1)
    diag_past = col < row
    has_prev = step > 0

    blk = [slice(j * sub, (j + 1) * sub) for j in range(nsub)]
    qs = [q_ref[b, :] for b in blk]
    k_d = [kc_ref[b, :] for b in blk]
    v_d = [vc_ref[b, :] for b in blk]
    k_p = [kp_ref[...]] + k_d[:-1]
    v_p = [vp_ref[...]] + v_d[:-1]
    ls_d, lk_d, ls_p, lk_p = [], [], [], []
    for j in range(nsub):
        ls, lk = _sb_logs(_dot_nt(qs[j], k_d[j]))
        ls_d.append(ls)
        lk_d.append(jnp.where(diag_past, lk, 0.0))
        ls, lk = _sb_logs(_dot_nt(qs[j], k_p[j]))
        ls_p.append(ls)
        lk_p.append(jnp.where(has_prev, lk, 0.0) if j == 0 else lk)
    r_in = _sb_later_sums(jnp.concatenate(lk_d + lk_p, axis=0), later2)
    first_max = []
    for j in range(nsub):
        r_d = r_in[j * sub:(j + 1) * sub]
        r_p = r_in[(nsub + j) * sub:(nsub + j + 1) * sub]
        carry_d = r_d[:, 0:1] + lk_d[j][:, 0:1]
        w_d = jnp.where(diag_past, jnp.exp(ls_d[j] + r_d), 0.0)
        w_p = jnp.exp(ls_p[j] + r_p + carry_d)
        if j == 0:
            w_p = jnp.where(has_prev, w_p, 0.0)
        carry = carry_d + r_p[:, 0:1] + lk_p[j][:, 0:1]
        acc_scr[j] = _dot(w_d.astype(BF16), v_d[j]) + _dot(w_p.astype(BF16), v_p[j])
        carry_scr[j] = carry
        first_max.append(jnp.max(carry))

    for j in range(nsub):
        def cond(state):
            kb, mx = state
            return jnp.logical_and(kb >= 0, mx > -SB_SKIP)

        def body(state, j=j):
            kb, _ = state
            k0 = pl.multiple_of(kb * sub, sub)
            acc, carry = _sb_tile(q_ref[j * sub:(j + 1) * sub, :], k_ref[pl.ds(k0, sub), :],
                                  v_ref[pl.ds(k0, sub), :], carry_scr[j], later2)
            acc_scr[j] += acc
            carry_scr[j] = carry
            return kb - 1, jnp.max(carry)

        lax.while_loop(cond, body, (step * nsub + j - 2, first_max[j]))
        o_ref[j * sub:(j + 1) * sub, :] = acc_scr[j].astype(o_ref.dtype)


def _stick_breaking(qkv, q_col, k_col, v_col):
    t = qkv.shape[0]
    d = B_HEAD_DIM
    bq = min(SB_BQ, t)
    nsub = bq // SB_SUB
    assert t % bq == 0 and bq % SB_SUB == 0

    def cur(col):
        return pl.BlockSpec((bq, d), lambda h, i: (i, col + h))

    def prev(col):
        return pl.BlockSpec((SB_SUB, d), lambda h, i: (jnp.maximum(i * nsub - 1, 0), col + h))

    def full(col):
        return pl.BlockSpec((t, d), lambda h, i: (0, col + h))

    return pl.pallas_call(
        _sb_kernel,
        out_shape=jax.ShapeDtypeStruct((t, B_HEADS * d), BF16),
        grid=(B_HEADS, t // bq),
        in_specs=[cur(q_col), cur(k_col), cur(v_col), prev(k_col), prev(v_col),
                  full(k_col), full(v_col)],
        out_specs=pl.BlockSpec((bq, d), lambda h, i: (i, h)),
        scratch_shapes=[pltpu.VMEM((nsub, SB_SUB, d), F32), pltpu.VMEM((nsub, SB_SUB, 1), F32)],
        compiler_params=_cparams(("parallel", "arbitrary")),
        name="stick_breaking",
    )(qkv, qkv, qkv, qkv, qkv, qkv, qkv)


def _mix_kernel(g1_ref, g2_ref, ya_ref, wa_ref, yb_ref, wb_ref, o_ref):
    pa = _dot(ya_ref[...], wa_ref[...])
    pb = _dot(yb_ref[...], wb_ref[...])
    o_ref[...] = (g1_ref[...].astype(F32) * pa + g2_ref[...].astype(F32) * pb).astype(o_ref.dtype)


def _mix(gates, y_a, w_a, y_b, w_b, *, tm, tn):
    t = y_a.shape[0]
    d = w_a.shape[1]
    nt = d // tn
    return pl.pallas_call(
        _mix_kernel,
        out_shape=jax.ShapeDtypeStruct((t, d), BF16),
        grid=(t // tm, nt),
        in_specs=[
            pl.BlockSpec((tm, tn), lambda i, j: (i, j)),
            pl.BlockSpec((tm, tn), lambda i, j: (i, j + nt)),
            pl.BlockSpec((tm, y_a.shape[1]), lambda i, j: (i, 0)),
            pl.BlockSpec((w_a.shape[0], tn), lambda i, j: (0, j)),
            pl.BlockSpec((tm, y_b.shape[1]), lambda i, j: (i, 0)),
            pl.BlockSpec((w_b.shape[0], tn), lambda i, j: (0, j)),
        ],
        out_specs=pl.BlockSpec((tm, tn), lambda i, j: (i, j)),
        compiler_params=_cparams(("parallel", "arbitrary")),
        name="mix",
    )(gates, gates, y_a, w_a, y_b, w_b)


def _out_kernel(x_ref, mix_ref, wo_ref, g_ref, wrh_ref, wrl_ref, br_ref,
                x1_ref, h2_ref, ri_ref, rg_ref, cnt_ref):
    tm = x_ref.shape[0]
    x1 = x_ref[...] + _dot(mix_ref[...], wo_ref[...])
    x1_ref[...] = x1
    ms = jnp.mean(x1 * x1, axis=-1, keepdims=True)
    h2 = x1 * lax.rsqrt(ms + RMS_EPS) * g_ref[...]
    h_hi, h_lo = _split_bf16(h2)
    h2_ref[...] = h_hi
    w_hi = wrh_ref[...]
    lg = (_dot_nt(w_hi, h_hi) + _dot_nt(w_hi, h_lo) + _dot_nt(wrl_ref[...], h_hi)
          + br_ref[:, 0:1])

    @pl.when(pl.program_id(0) == 0)
    def _():
        cnt_ref[...] = jnp.zeros_like(cnt_ref)

    row = lax.broadcasted_iota(jnp.int32, (N_EXPERTS, tm), 0)
    vals, idxs, hots = [], [], []
    for _ in range(TOP_K):
        mx = jnp.max(lg, axis=0, keepdims=True)
        idx = jnp.min(jnp.where(lg == mx, row, N_EXPERTS), axis=0, keepdims=True)
        hot = row == idx
        vals.append(mx)
        idxs.append(idx)
        hots.append(hot)
        lg = jnp.where(hot, -jnp.inf, lg)

    exps = [jnp.exp(v - vals[0]) for v in vals]
    denom = exps[0]
    for e in exps[1:]:
        denom = denom + e

    chosen = hots[0]
    for hot in hots[1:]:
        chosen = jnp.logical_or(chosen, hot)
    chosen_f = jnp.where(chosen, 1.0, 0.0)
    r = lax.broadcasted_iota(jnp.int32, (tm, tm), 0)
    c = lax.broadcasted_iota(jnp.int32, (tm, tm), 1)
    earlier = jnp.where(r < c, 1.0, 0.0).astype(BF16)
    before = _dot(chosen_f.astype(BF16), earlier) + cnt_ref[:, 0:1]
    cnt_ref[...] += jnp.sum(chosen_f, axis=1, keepdims=True)

    row8 = lax.broadcasted_iota(jnp.int32, (2 * TOP_K, tm), 0)
    ri = jnp.zeros((2 * TOP_K, tm), jnp.int32)
    rg = jnp.zeros((2 * TOP_K, tm), F32)
    for k in range(TOP_K):
        rank = jnp.sum(jnp.where(hots[k], before, 0.0), axis=0, keepdims=True).astype(jnp.int32)
        ri = jnp.where(row8 == k, idxs[k], ri)
        ri = jnp.where(row8 == TOP_K + k, rank, ri)
        rg = jnp.where(row8 == k, exps[k] / denom, rg)
    ri_ref[...] = ri
    rg_ref[...] = jnp.concatenate([rg, jnp.zeros((LANE - 2 * TOP_K, tm), F32)], axis=0).T


def _out_proj(x, mixed, w_o, g, wr_hi, wr_lo, b_r, *, tm):
    t, d = x.shape
    ne = wr_hi.shape[0]
    return pl.pallas_call(
        _out_kernel,
        out_shape=(jax.ShapeDtypeStruct((t, d), F32),
                   jax.ShapeDtypeStruct((t, d), BF16),
                   jax.ShapeDtypeStruct((2 * TOP_K, t), jnp.int32),
                   jax.ShapeDtypeStruct((t, LANE), F32),
                   jax.ShapeDtypeStruct((ne, LANE), F32)),
        grid=(t // tm,),
        in_specs=[
            pl.BlockSpec((tm, d), lambda i: (i, 0)),
            pl.BlockSpec((tm, d), lambda i: (i, 0)),
            pl.BlockSpec((d, d), lambda i: (0, 0)),
            pl.BlockSpec((1, d), lambda i: (0, 0)),
            pl.BlockSpec((ne, d), lambda i: (0, 0)),
            pl.BlockSpec((ne, d), lambda i: (0, 0)),
            pl.BlockSpec((ne, LANE), lambda i: (0, 0)),
        ],
        out_specs=(pl.BlockSpec((tm, d), lambda i: (i, 0)),
                   pl.BlockSpec((tm, d), lambda i: (i, 0)),
                   pl.BlockSpec((2 * TOP_K, tm), lambda i: (0, i)),
                   pl.BlockSpec((tm, LANE), lambda i: (i, 0)),
                   pl.BlockSpec((ne, LANE), lambda i: (0, 0))),
        compiler_params=_cparams(("arbitrary",)),
        name="out_proj",
    )(x, mixed, w_o, g, wr_hi, wr_lo, b_r)


def _expert_changed(be_ref, m):
    return jnp.logical_or(m == 0, be_ref[m] != be_ref[jnp.maximum(m - 1, 0)])


def _next_run_expert(block_expert, n_used):
    nb = block_expert.shape[0]
    idx = jnp.arange(nb, dtype=jnp.int32)
    starts = jnp.logical_or(idx == 0, block_expert != jnp.roll(block_expert, 1))
    cand = jnp.where(jnp.logical_and(starts, idx < n_used[0]), idx, nb)
    later_start = jnp.concatenate([lax.cummin(cand, axis=0, reverse=True)[1:],
                                   jnp.full((1,), nb, jnp.int32)])
    at_start = jnp.sum(jnp.where(idx[None, :] == later_start[:, None], block_expert[None, :], 0), axis=1)
    return jnp.where(later_start < nb, at_start, -1).astype(jnp.int32)


def _moe_up_kernel(be_ref, nu_ref, nx_ref, x_ref, w_hbm, bg_ref, bu_ref, o_ref,
                   stage, w_scr, sem):
    n = pl.program_id(0)
    m = pl.program_id(1)
    nf = pl.num_programs(0)
    tf = w_scr.shape[2]

    def fetch(e, nn, part):
        col = pl.multiple_of((nn + part * nf) * tf, tf)
        return pltpu.make_async_copy(w_hbm.at[e, :, pl.ds(col, tf)], stage.at[part], sem.at[part])

    def start(e, nn):
        fetch(e, nn, 0).start()
        fetch(e, nn, 1).start()

    @pl.when(jnp.logical_and(jnp.logical_and(n == 0, m == 0), nu_ref[0] > 0))
    def _():
        start(be_ref[0], 0)

    @pl.when(m < nu_ref[0])
    def _():
        @pl.when(_expert_changed(be_ref, m))
        def _():
            fetch(be_ref[m], n, 0).wait()
            fetch(be_ref[m], n, 1).wait()
            def convert(i, carry):
                rows = pl.ds(pl.multiple_of(i * CAST_ROWS, CAST_ROWS), CAST_ROWS)
                w_scr[:, rows, :] = stage[:, rows, :].astype(BF16)
                return carry

            lax.fori_loop(0, stage.shape[1] // CAST_ROWS, convert, 0)
            nxt = nx_ref[m]

            @pl.when(nxt >= 0)
            def _():
                start(nxt, n)

            @pl.when(jnp.logical_and(nxt < 0, n + 1 < nf))
            def _():
                start(be_ref[0], n + 1)

        x = x_ref[...]
        gate = jnp.minimum(_dot(x, w_scr[0]) + bg_ref[0], SWIGLU_LIMIT)
        up = jnp.clip(_dot(x, w_scr[1]) + bu_ref[0], -SWIGLU_LIMIT, SWIGLU_LIMIT)
        o_ref[...] = ((up + 1.0) * gate * _sigmoid(SWIGLU_ALPHA * gate)).astype(o_ref.dtype)

    @pl.when(m >= nu_ref[0])
    def _():
        o_ref[...] = jnp.zeros_like(o_ref)


def _moe_up(block_expert, n_used, x_pad, w_gu, b_gu, *, tf):
    rows, d = x_pad.shape
    nb = rows // MOE_TM
    d_ff = w_gu.shape[2] // 2
    nf = d_ff // tf

    def blk(m, nu):
        return jnp.maximum(jnp.minimum(m, nu[0] - 1), 0)

    return pl.pallas_call(
        _moe_up_kernel,
        out_shape=jax.ShapeDtypeStruct((rows, d_ff), BF16),
        grid_spec=pltpu.PrefetchScalarGridSpec(
            num_scalar_prefetch=3,
            grid=(nf, nb),
            in_specs=[
                pl.BlockSpec((MOE_TM, d), lambda n, m, be, nu, nx: (blk(m, nu), 0)),
                pl.BlockSpec(memory_space=pl.ANY),
                pl.BlockSpec((1, 1, tf), lambda n, m, be, nu, nx: (be[blk(m, nu)], 0, n)),
                pl.BlockSpec((1, 1, tf), lambda n, m, be, nu, nx: (be[blk(m, nu)], 0, n + nf)),
            ],
            out_specs=pl.BlockSpec((MOE_TM, tf), lambda n, m, be, nu, nx: (m, n)),
            scratch_shapes=[pltpu.VMEM((2, d, tf), F32), pltpu.VMEM((2, d, tf), BF16),
                            pltpu.SemaphoreType.DMA((2,))],
        ),
        compiler_params=_cparams(("arbitrary", "arbitrary")),
        name="moe_up",
    )(block_expert, n_used, _next_run_expert(block_expert, n_used), x_pad, w_gu, b_gu, b_gu)


def _moe_down_kernel(be_ref, nu_ref, nx_ref, *refs, chunk_starts):
    a_refs, (w_hbm, bd_ref, o_ref, stage, wd_scr, sem) = refs[:-6], refs[-6:]
    m = pl.program_id(0)

    def fetch(e):
        return pltpu.make_async_copy(w_hbm.at[e], stage, sem.at[0])

    @pl.when(jnp.logical_and(m == 0, nu_ref[0] > 0))
    def _():
        fetch(be_ref[0]).start()

    @pl.when(m < nu_ref[0])
    def _():
        @pl.when(_expert_changed(be_ref, m))
        def _():
            fetch(be_ref[m]).wait()
            wd_scr[...] = stage[...].astype(BF16)
            nxt = nx_ref[m]

            @pl.when(nxt >= 0)
            def _():
                fetch(nxt).start()

        for c, a_ref in enumerate(a_refs):
            @pl.when(jnp.logical_and(m >= chunk_starts[c], m < chunk_starts[c + 1]))
            def _(a_ref=a_ref):
                o_ref[...] = (_dot(a_ref[...], wd_scr[...]) + bd_ref[0]).astype(o_ref.dtype)

    @pl.when(m >= nu_ref[0])
    def _():
        o_ref[...] = jnp.zeros_like(o_ref)


def _moe_down(block_expert, n_used, acts, w_d, b_d):
    d_ff = acts[0].shape[1]
    chunk_starts = [0]
    for a in acts:
        chunk_starts.append(chunk_starts[-1] + a.shape[0] // MOE_TM)
    nb = chunk_starts[-1]
    d = w_d.shape[2]

    def blk(m, nu):
        return jnp.maximum(jnp.minimum(m, nu[0] - 1), 0)

    def act_spec(c):
        lo, n_c = chunk_starts[c], chunk_starts[c + 1] - chunk_starts[c]
        return pl.BlockSpec((MOE_TM, d_ff),
                            lambda m, be, nu, nx: (jnp.clip(blk(m, nu) - lo, 0, n_c - 1), 0))

    return pl.pallas_call(
        functools.partial(_moe_down_kernel, chunk_starts=tuple(chunk_starts)),
        out_shape=jax.ShapeDtypeStruct((nb * MOE_TM, d), BF16),
        grid_spec=pltpu.PrefetchScalarGridSpec(
            num_scalar_prefetch=3,
            grid=(nb,),
            in_specs=[act_spec(c) for c in range(len(acts))] + [
                pl.BlockSpec(memory_space=pl.ANY),
                pl.BlockSpec((1, 1, d), lambda m, be, nu, nx: (be[blk(m, nu)], 0, 0)),
            ],
            out_specs=pl.BlockSpec((MOE_TM, d), lambda m, be, nu, nx: (m, 0)),
            scratch_shapes=[pltpu.VMEM((d_ff, d), F32), pltpu.VMEM((d_ff, d), BF16),
                            pltpu.SemaphoreType.DMA((1,))],
        ),
        compiler_params=_cparams(("arbitrary",)),
        name="moe_down",
    )(block_expert, n_used, _next_run_expert(block_expert, n_used), *acts, w_d, b_d)


def _combine_kernel(x_ref, y0_ref, y1_ref, y2_ref, y3_ref, rg_ref, g_ref, o_ref, *, normalize):
    x = x_ref[...]
    rg = rg_ref[...]
    for k, y_ref in enumerate((y0_ref, y1_ref, y2_ref, y3_ref)):
        x = x + rg[:, k:k + 1] * y_ref[...].astype(F32)
    if normalize:
        ms = jnp.mean(x * x, axis=-1, keepdims=True)
        x = x * lax.rsqrt(ms + RMS_EPS) * g_ref[...]
    o_ref[...] = x


def _combine(x_buf, y_rows, rg, g, *, chunk, n_chunk_tokens, normalize, tm):
    t, d = x_buf.shape
    nblk = n_chunk_tokens // tm
    x_spec = pl.BlockSpec((tm, d), lambda i: (chunk * nblk + i, 0))

    def y_spec(k):
        return pl.BlockSpec((tm, d), lambda i: (k * nblk + i, 0))

    return pl.pallas_call(
        functools.partial(_combine_kernel, normalize=normalize),
        out_shape=jax.ShapeDtypeStruct((t, d), F32),
        grid=(nblk,),
        in_specs=[x_spec] + [y_spec(k) for k in range(TOP_K)] + [
            pl.BlockSpec((tm, rg.shape[1]), lambda i: (chunk * nblk + i, 0)),
            pl.BlockSpec((1, d), lambda i: (0, 0)),
        ],
        out_specs=x_spec,
        input_output_aliases={0: 0},
        compiler_params=_cparams(("parallel",)),
        name="combine_norm",
    )(x_buf, y_rows, y_rows, y_rows, y_rows, rg, g)


def _route(ri, counts, t):
    p_n = t * TOP_K
    experts = ri[:TOP_K]
    rank = ri[TOP_K:]
    counts = counts[:, 0].astype(jnp.int32)
    padded = (counts + MOE_TM - 1) // MOE_TM * MOE_TM
    pad_end = jnp.cumsum(padded)
    pad_start = pad_end - padded
    e_ids = jnp.arange(N_EXPERTS, dtype=jnp.int32)[:, None, None]
    dest = rank + jnp.sum(jnp.where(experts[None] == e_ids, pad_start[:, None, None], 0), axis=0)
    nb = -(-p_n // MOE_TM) + N_EXPERTS
    src_tok = (jnp.arange(nb * MOE_TM, dtype=jnp.int32) % t).at[dest.reshape(p_n)].set(
        jnp.tile(jnp.arange(t, dtype=jnp.int32), TOP_K), mode="promise_in_bounds",
        unique_indices=True)
    block_start = jnp.arange(nb, dtype=jnp.int32) * MOE_TM
    block_expert = jnp.minimum(
        jnp.sum((block_start[:, None] >= pad_end[None, :]).astype(jnp.int32), axis=1),
        N_EXPERTS - 1)
    n_used = (pad_end[-1] // MOE_TM).astype(jnp.int32).reshape(1)
    return dest, src_tok, block_expert, n_used


def kernel(x, norm_mix, w_in, conv_w, conv_b, b_gates_if, norm_head, w_proj_a, w_proj_b,
           w_merge_gate, b_merge_gate, w_out, norm_ffn, w_router, b_router, w_gate_up,
           b_gate_up, w_down, b_down, norm_final):
    bn, s, d = x.shape
    assert bn == 1
    depth = norm_mix.shape[0]
    xt = x.reshape(s, d)
    tm = min(1024, s)
    aqk = A_HEADS * A_QK
    av = A_HEADS * A_V
    bw = B_HEADS * B_HEAD_DIM

    for l in range(depth):
        wl = w_in[l]
        c0 = 2 * aqk
        c1 = c0 + 2 * av
        c2 = c1 + 2 * A_HEADS
        w_f32 = jnp.concatenate(
            [wl[:, :c0], wl[:, c1:c2], jnp.zeros((d, LANE - 2 * A_HEADS), F32)], axis=1).astype(BF16)
        b_f32 = jnp.concatenate(
            [jnp.zeros((c0,), F32), b_gates_if[l], jnp.zeros((LANE - 2 * A_HEADS,), F32)])[None, :]
        w_bf = jnp.concatenate([wl[:, c0:c1], wl[:, c2:c2 + bw] * (B_HEAD_DIM ** -0.5), wl[:, c2 + bw:]],
                               axis=1).astype(BF16)
        g_mix = norm_mix[l][None, :]

        p_f32, h1, if_row = _norm_proj(xt, g_mix, w_f32, b_f32, tm=tm)
        tn = 1024
        o_tile = av // tn
        p_bf = _proj(h1, w_bf, jnp.zeros((1, w_bf.shape[1]), F32),
                     sigmoid_tiles=(o_tile, 2 * o_tile), tm=tm, tn=tn)
        gate_tn = 2 * tn
        gates = _proj(h1, w_merge_gate[l].astype(BF16), b_merge_gate[l][None, :],
                      sigmoid_tiles=(0, w_merge_gate.shape[2] // gate_tn), tm=tm, tn=gate_tn)

        y_a = _mlstm(p_f32, p_bf, if_row, conv_w[l], conv_b[l][None, :], norm_head[l][None, :])
        qb = 2 * av // B_HEAD_DIM
        y_b = _stick_breaking(p_bf, qb, qb + B_HEADS, qb + 2 * B_HEADS)

        mixed = _mix(gates, y_a, w_proj_a[l].astype(BF16), y_b, w_proj_b[l].astype(BF16),
                     tm=tm, tn=d)
        wr_hi, wr_lo = _split_bf16(w_router[l].T)
        b_r = jnp.broadcast_to(b_router[l][:, None], (N_EXPERTS, LANE))
        x1, h2, ri, rg, counts = _out_proj(xt, mixed, w_out[l].astype(BF16), norm_ffn[l][None, :],
                                           wr_hi, wr_lo, b_r, tm=min(512, s))

        dest, src_tok, block_expert, n_used = _route(ri, counts, s)
        nb = block_expert.shape[0]
        bounds = [0] + [nb * f // 10 for f in MOE_CHUNK_TENTHS] + [nb]
        acts = []
        for lo, hi in zip(bounds[:-1], bounds[1:]):
            x_pad = h2.at[src_tok[lo * MOE_TM:hi * MOE_TM]].get(mode="promise_in_bounds")
            acts.append(_moe_up(block_expert[lo:hi], jnp.clip(n_used - lo, 0, hi - lo),
                                x_pad, w_gate_up[l], b_gate_up[l][:, None, :], tf=1024))
        y_pad = _moe_down(block_expert, n_used, acts, w_down[l], b_down[l][:, None, :])
        nct = s // COMBINE_CHUNKS
        xt = x1
        for j in range(COMBINE_CHUNKS):
            idx = dest[:, j * nct:(j + 1) * nct].reshape(TOP_K * nct)
            y_rows = y_pad.at[idx].get(mode="promise_in_bounds")
            xt = _combine(xt, y_rows, rg, norm_final[None, :], chunk=j, n_chunk_tokens=nct,
                          normalize=(l + 1 == depth), tm=min(512, nct))
    return xt.reshape(bn, s, d)
```

```python
import functools

import jax
import jax.numpy as jnp
from jax import lax
from jax.experimental import pallas as pl
from jax.experimental.pallas import tpu as pltpu

F32 = jnp.float32
BF16 = jnp.bfloat16

RMS_EPS = 1e-5
A_HEADS = 4
A_QK = 128
A_V = 256
CONV_W = 4
B_HEADS = 8
B_HEAD_DIM = 128
N_EXPERTS = 32
TOP_K = 4
SWIGLU_LIMIT = 7.0
SWIGLU_ALPHA = 1.702

LANE = 128
SUBLANE = 8
VMEM_LIMIT = 60 * 1024 * 1024

MLSTM_CHUNK = 128
SB_BQ = 4096
SB_SUB = 128
SB_SKIP = 88.0
MOE_TM = 512
CAST_ROWS = 32
MOE_CHUNK_TENTHS = (1, 4)
COMBINE_CHUNKS = 16


def _cparams(sem):
    return pltpu.CompilerParams(dimension_semantics=sem, vmem_limit_bytes=VMEM_LIMIT)


def _log_sigmoid(z):
    return jnp.minimum(z, 0.0) - jnp.log1p(jnp.exp(-jnp.abs(z)))


def _sigmoid(z):
    return 1.0 / (1.0 + jnp.exp(-z))


def _split_bf16(x):
    hi = x.astype(BF16)
    lo = (x - hi.astype(F32)).astype(BF16)
    return hi, lo


def _dot(a, b):
    return jnp.dot(a, b, preferred_element_type=F32)


def _dot_nt(a, b):
    return lax.dot_general(a, b, (((1,), (1,)), ((), ())), preferred_element_type=F32)


def _dot_tn(a, b):
    return lax.dot_general(a, b, (((0,), (0,)), ((), ())), preferred_element_type=F32)


def _norm_proj_kernel(x_ref, g_ref, w_ref, b_ref, o_ref, h_ref, gt_ref):
    x = x_ref[...]
    ms = jnp.mean(x * x, axis=-1, keepdims=True)
    h = (x * lax.rsqrt(ms + RMS_EPS) * g_ref[...]).astype(BF16)
    h_ref[...] = h
    acc = _dot(h, w_ref[...]) + b_ref[...]
    o_ref[...] = acc
    gt_ref[...] = acc[:, acc.shape[1] - LANE:].T[:SUBLANE, :]


def _norm_proj(x, g, w, b, *, tm):
    t, d = x.shape
    n = w.shape[1]
    return pl.pallas_call(
        _norm_proj_kernel,
        out_shape=(jax.ShapeDtypeStruct((t, n), F32),
                   jax.ShapeDtypeStruct((t, d), BF16),
                   jax.ShapeDtypeStruct((SUBLANE, t), F32)),
        grid=(t // tm,),
        in_specs=[
            pl.BlockSpec((tm, d), lambda i: (i, 0)),
            pl.BlockSpec((1, d), lambda i: (0, 0)),
            pl.BlockSpec((d, n), lambda i: (0, 0)),
            pl.BlockSpec((1, n), lambda i: (0, 0)),
        ],
        out_specs=(pl.BlockSpec((tm, n), lambda i: (i, 0)),
                   pl.BlockSpec((tm, d), lambda i: (i, 0)),
                   pl.BlockSpec((SUBLANE, tm), lambda i: (0, i))),
        compiler_params=_cparams(("parallel",)),
        name="norm_proj",
    )(x, g, w, b)


def _proj_kernel(h_ref, w_ref, b_ref, o_ref, *, sigmoid_tiles, n_tiles):
    acc = _dot(h_ref[...], w_ref[...]) + b_ref[...]
    lo, hi = sigmoid_tiles
    if lo == 0 and hi == n_tiles:
        o_ref[...] = _sigmoid(acc).astype(o_ref.dtype)
    elif lo == hi:
        o_ref[...] = acc.astype(o_ref.dtype)
    else:
        j = pl.program_id(1)
        gated = jnp.logical_and(j >= lo, j < hi)

        @pl.when(gated)
        def _():
            o_ref[...] = _sigmoid(acc).astype(o_ref.dtype)

        @pl.when(jnp.logical_not(gated))
        def _():
            o_ref[...] = acc.astype(o_ref.dtype)


def _proj(h, w, b, *, sigmoid_tiles, tm, tn):
    t, d = h.shape
    n = w.shape[1]
    return pl.pallas_call(
        functools.partial(_proj_kernel, sigmoid_tiles=sigmoid_tiles, n_tiles=n // tn),
        out_shape=jax.ShapeDtypeStruct((t, n), BF16),
        grid=(t // tm, n // tn),
        in_specs=[
            pl.BlockSpec((tm, d), lambda i, j: (i, 0)),
            pl.BlockSpec((d, tn), lambda i, j: (0, j)),
            pl.BlockSpec((1, tn), lambda i, j: (0, j)),
        ],
        out_specs=pl.BlockSpec((tm, tn), lambda i, j: (i, j)),
        compiler_params=_cparams(("parallel", "arbitrary")),
        name="proj",
    )(h, w, b)


def _mlstm_kernel(qk_ref, v_ref, o_ref, ifc_ref, ifr_ref, cw_ref, cb_ref, nh_ref,
                  y_ref, ext_scr, c_scr, n_scr, m_scr):
    L = MLSTM_CHUNK
    step = pl.program_id(0)

    @pl.when(step == 0)
    def _():
        ext_scr[0:SUBLANE, :] = jnp.zeros((SUBLANE, ext_scr.shape[1]), F32)
        c_scr[...] = jnp.zeros_like(c_scr)
        n_scr[...] = jnp.zeros_like(n_scr)
        m_scr[...] = jnp.zeros_like(m_scr)

    raw = qk_ref[...]
    ext_scr[SUBLANE:SUBLANE + L, :] = raw
    conv = cb_ref[...]
    for j in range(CONV_W):
        off = SUBLANE - (CONV_W - 1) + j
        conv = conv + ext_scr[off:off + L, :] * cw_ref[j:j + 1, :]
    ext_scr[0:SUBLANE, :] = raw[L - SUBLANE:L, :]
    qk = conv * _sigmoid(conv)

    row = lax.broadcasted_iota(jnp.int32, (L, L), 0)
    col = lax.broadcasted_iota(jnp.int32, (L, L), 1)
    causal = row >= col
    tri_incl = jnp.where(causal, 1.0, 0.0).astype(BF16)
    tri_incl_t = jnp.where(col >= row, 1.0, 0.0).astype(BF16)

    hi, lo = _split_bf16(_log_sigmoid(ifc_ref[...]))
    a_cols = _dot(tri_incl, hi) + _dot(tri_incl, lo)
    hi, lo = _split_bf16(_log_sigmoid(ifr_ref[...]))
    a_rows = _dot(hi, tri_incl_t) + _dot(lo, tri_incl_t)

    for h in range(A_HEADS):
        q = qk[:, h * A_QK:(h + 1) * A_QK]
        k = qk[:, A_HEADS * A_QK + h * A_QK:A_HEADS * A_QK + (h + 1) * A_QK] * (A_QK ** -0.5)
        v = v_ref[:, h * A_V:(h + 1) * A_V]
        q_bf = q.astype(BF16)
        k_bf = k.astype(BF16)

        i_col = ifc_ref[:, h:h + 1]
        i_row = ifr_ref[h:h + 1, :]
        a_col = a_cols[:, A_HEADS + h:A_HEADS + h + 1]
        a_row = a_rows[A_HEADS + h:A_HEADS + h + 1, :]
        g_tot = a_col[L - 1:L, :]

        m_prev = m_scr[h]
        n_prev = n_scr[h]
        ct_prev = c_scr[h]

        m_inter = a_col + m_prev
        d_log = jnp.where(causal, a_col - a_row + i_row, -jnp.inf)
        m_t = jnp.maximum(m_inter, jnp.max(d_log, axis=-1, keepdims=True))
        p = jnp.exp(d_log - m_t) * _dot_nt(q_bf, k_bf)
        s_inter = jnp.exp(m_inter - m_t)
        num = s_inter * _dot(q_bf, ct_prev.astype(BF16)) + _dot(p.astype(BF16), v)
        den = (s_inter * jnp.sum(q * n_prev, axis=-1, keepdims=True)
               + jnp.sum(p, axis=-1, keepdims=True))
        hh = num / jnp.maximum(jnp.abs(den), jnp.exp(-m_t))

        hh = hh * lax.rsqrt(jnp.mean(hh * hh, axis=-1, keepdims=True) + RMS_EPS)
        gate = o_ref[:, h * A_V:(h + 1) * A_V].astype(F32)
        y_ref[:, h * A_V:(h + 1) * A_V] = (
            hh * nh_ref[:, h * A_V:(h + 1) * A_V] * gate).astype(y_ref.dtype)

        w_col = g_tot - a_col + i_col
        m_loc = jnp.max(w_col, axis=0, keepdims=True)
        ke = k * jnp.exp(w_col - m_loc)
        ct_loc = _dot_tn(ke.astype(BF16), v)
        n_loc = jnp.sum(ke, axis=0, keepdims=True)
        m_new = jnp.maximum(g_tot + m_prev, m_loc)
        s_old = jnp.exp(g_tot + m_prev - m_new)
        s_new = jnp.exp(m_loc - m_new)
        c_scr[h] = s_old * ct_prev + s_new * ct_loc
        n_scr[h] = s_old * n_prev + s_new * n_loc
        m_scr[h] = m_new


def _mlstm(p_f32, p_bf, if_row, conv_w, conv_b, norm_head):
    t = p_f32.shape[0]
    L = MLSTM_CHUNK
    wq = conv_w.shape[1]
    wv = norm_head.shape[1]
    return pl.pallas_call(
        _mlstm_kernel,
        out_shape=jax.ShapeDtypeStruct((t, wv), BF16),
        grid=(t // L,),
        in_specs=[
            pl.BlockSpec((L, wq), lambda i: (i, 0)),
            pl.BlockSpec((L, wv), lambda i: (i, 0)),
            pl.BlockSpec((L, wv), lambda i: (i, 1)),
            pl.BlockSpec((L, LANE), lambda i: (i, wq // LANE)),
            pl.BlockSpec((SUBLANE, L), lambda i: (0, i)),
            pl.BlockSpec((CONV_W, wq), lambda i: (0, 0)),
            pl.BlockSpec((1, wq), lambda i: (0, 0)),
            pl.BlockSpec((1, wv), lambda i: (0, 0)),
        ],
        out_specs=pl.BlockSpec((L, wv), lambda i: (i, 0)),
        scratch_shapes=[
            pltpu.VMEM((L + SUBLANE, wq), F32),
            pltpu.VMEM((A_HEADS, A_QK, A_V), F32),
            pltpu.VMEM((A_HEADS, 1, A_QK), F32),
            pltpu.VMEM((A_HEADS, 1, 1), F32),
        ],
        compiler_params=_cparams(("arbitrary",)),
        name="mlstm",
    )(p_f32, p_bf, p_bf, p_f32, if_row, conv_w, conv_b, norm_head)


def _later(n):
    j = lax.broadcasted_iota(jnp.int32, (n, n), 0)
    s = lax.broadcasted_iota(jnp.int32, (n, n), 1)
    return jnp.where(j > s, 1.0, 0.0).astype(BF16)


def _sb_logs(z):
    ls = jnp.minimum(z, 0.0) - jnp.log(1.0 + jnp.exp(-jnp.abs(z)))
    return ls, ls - z


def _sb_later_sums(lk, later2):
    hi, lo = _split_bf16(lk)
    return _dot(jnp.concatenate([hi, lo], axis=1), later2)


def _sb_tile(q, k, v, carry, later2):
    ls, lk = _sb_logs(_dot_nt(q, k))
    r_in = _sb_later_sums(lk, later2)
    w = jnp.exp(ls + r_in + carry)
    return _dot(w.astype(BF16), v), carry + r_in[:, 0:1] + lk[:, 0:1]


def _sb_kernel(q_ref, kc_ref, vc_ref, kp_ref, vp_ref, k_ref, v_ref, o_ref, acc_scr, carry_scr):
    sub = SB_SUB
    nsub = q_ref.shape[0] // sub
    step = pl.program_id(1)
    later = _later(sub)
    later2 = jnp.concatenate([later, later], axis=0)
    row = lax.broadcasted_iota(jnp.int32, (sub, sub), 0)
    col = lax.broadcasted_iota(jnp.int32, (sub, sub), 1)
    diag_past = col < row
    has_prev = step > 0

    blk = [slice(j * sub, (j + 1) * sub) for j in range(nsub)]
    qs = [q_ref[b, :] for b in blk]
    k_d = [kc_ref[b, :] for b in blk]
    v_d = [vc_ref[b, :] for b in blk]
    k_p = [kp_ref[...]] + k_d[:-1]
    v_p = [vp_ref[...]] + v_d[:-1]
    ls_d, lk_d, ls_p, lk_p = [], [], [], []
    for j in range(nsub):
        ls, lk = _sb_logs(_dot_nt(qs[j], k_d[j]))
        ls_d.append(ls)
        lk_d.append(jnp.where(diag_past, lk, 0.0))
        ls, lk = _sb_logs(_dot_nt(qs[j], k_p[j]))
        ls_p.append(ls)
        lk_p.append(jnp.where(has_prev, lk, 0.0) if j == 0 else lk)
    r_in = _sb_later_sums(jnp.concatenate(lk_d + lk_p, axis=0), later2)
    first_max = []
    for j in range(nsub):
        r_d = r_in[j * sub:(j + 1) * sub]
        r_p = r_in[(nsub + j) * sub:(nsub + j + 1) * sub]
        carry_d = r_d[:, 0:1] + lk_d[j][:, 0:1]
        w_d = jnp.where(diag_past, jnp.exp(ls_d[j] + r_d), 0.0)
        w_p = jnp.exp(ls_p[j] + r_p + carry_d)
        if j == 0:
            w_p = jnp.where(has_prev, w_p, 0.0)
        carry = carry_d + r_p[:, 0:1] + lk_p[j][:, 0:1]
        acc_scr[j] = _dot(w_d.astype(BF16), v_d[j]) + _dot(w_p.astype(BF16), v_p[j])
        carry_scr[j] = carry
        first_max.append(jnp.max(carry))

    for j in range(nsub):
        def cond(state):
            kb, mx = state
            return jnp.logical_and(kb >= 0, mx > -SB_SKIP)

        def body(state, j=j):
            kb, _ = state
            k0 = pl.multiple_of(kb * sub, sub)
            acc, carry = _sb_tile(q_ref[j * sub:(j + 1) * sub, :], k_ref[pl.ds(k0, sub), :],
                                  v_ref[pl.ds(k0, sub), :], carry_scr[j], later2)
            acc_scr[j] += acc
            carry_scr[j] = carry
            return kb - 1, jnp.max(carry)

        lax.while_loop(cond, body, (step * nsub + j - 2, first_max[j]))
        o_ref[j * sub:(j + 1) * sub, :] = acc_scr[j].astype(o_ref.dtype)


def _stick_breaking(qkv, q_col, k_col, v_col):
    t = qkv.shape[0]
    d = B_HEAD_DIM
    bq = min(SB_BQ, t)
    nsub = bq // SB_SUB
    assert t % bq == 0 and bq % SB_SUB == 0

    def cur(col):
        return pl.BlockSpec((bq, d), lambda h, i: (i, col + h))

    def prev(col):
        return pl.BlockSpec((SB_SUB, d), lambda h, i: (jnp.maximum(i * nsub - 1, 0), col + h))

    def full(col):
        return pl.BlockSpec((t, d), lambda h, i: (0, col + h))

    return pl.pallas_call(
        _sb_kernel,
        out_shape=jax.ShapeDtypeStruct((t, B_HEADS * d), BF16),
        grid=(B_HEADS, t // bq),
        in_specs=[cur(q_col), cur(k_col), cur(v_col), prev(k_col), prev(v_col),
                  full(k_col), full(v_col)],
        out_specs=pl.BlockSpec((bq, d), lambda h, i: (i, h)),
        scratch_shapes=[pltpu.VMEM((nsub, SB_SUB, d), F32), pltpu.VMEM((nsub, SB_SUB, 1), F32)],
        compiler_params=_cparams(("parallel", "arbitrary")),
        name="stick_breaking",
    )(qkv, qkv, qkv, qkv, qkv, qkv, qkv)


def _mix_kernel(g1_ref, g2_ref, ya_ref, wa_ref, yb_ref, wb_ref, o_ref):
    pa = _dot(ya_ref[...], wa_ref[...])
    pb = _dot(yb_ref[...], wb_ref[...])
    o_ref[...] = (g1_ref[...].astype(F32) * pa + g2_ref[...].astype(F32) * pb).astype(o_ref.dtype)


def _mix(gates, y_a, w_a, y_b, w_b, *, tm, tn):
    t = y_a.shape[0]
    d = w_a.shape[1]
    nt = d // tn
    return pl.pallas_call(
        _mix_kernel,
        out_shape=jax.ShapeDtypeStruct((t, d), BF16),
        grid=(t // tm, nt),
        in_specs=[
            pl.BlockSpec((tm, tn), lambda i, j: (i, j)),
            pl.BlockSpec((tm, tn), lambda i, j: (i, j + nt)),
            pl.BlockSpec((tm, y_a.shape[1]), lambda i, j: (i, 0)),
            pl.BlockSpec((w_a.shape[0], tn), lambda i, j: (0, j)),
            pl.BlockSpec((tm, y_b.shape[1]), lambda i, j: (i, 0)),
            pl.BlockSpec((w_b.shape[0], tn), lambda i, j: (0, j)),
        ],
        out_specs=pl.BlockSpec((tm, tn), lambda i, j: (i, j)),
        compiler_params=_cparams(("parallel", "arbitrary")),
        name="mix",
    )(gates, gates, y_a, w_a, y_b, w_b)


def _out_kernel(x_ref, mix_ref, wo_ref, g_ref, wrh_ref, wrl_ref, br_ref,
                x1_ref, h2_ref, ri_ref, rg_ref, cnt_ref):
    tm = x_ref.shape[0]
    x1 = x_ref[...] + _dot(mix_ref[...], wo_ref[...])
    x1_ref[...] = x1
    ms = jnp.mean(x1 * x1, axis=-1, keepdims=True)
    h2 = x1 * lax.rsqrt(ms + RMS_EPS) * g_ref[...]
    h_hi, h_lo = _split_bf16(h2)
    h2_ref[...] = h_hi
    w_hi = wrh_ref[...]
    lg = (_dot_nt(w_hi, h_hi) + _dot_nt(w_hi, h_lo) + _dot_nt(wrl_ref[...], h_hi)
          + br_ref[:, 0:1])

    @pl.when(pl.program_id(0) == 0)
    def _():
        cnt_ref[...] = jnp.zeros_like(cnt_ref)

    row = lax.broadcasted_iota(jnp.int32, (N_EXPERTS, tm), 0)
    vals, idxs, hots = [], [], []
    for _ in range(TOP_K):
        mx = jnp.max(lg, axis=0, keepdims=True)
        idx = jnp.min(jnp.where(lg == mx, row, N_EXPERTS), axis=0, keepdims=True)
        hot = row == idx
        vals.append(mx)
        idxs.append(idx)
        hots.append(hot)
        lg = jnp.where(hot, -jnp.inf, lg)

    exps = [jnp.exp(v - vals[0]) for v in vals]
    denom = exps[0]
    for e in exps[1:]:
        denom = denom + e

    chosen = hots[0]
    for hot in hots[1:]:
        chosen = jnp.logical_or(chosen, hot)
    chosen_f = jnp.where(chosen, 1.0, 0.0)
    r = lax.broadcasted_iota(jnp.int32, (tm, tm), 0)
    c = lax.broadcasted_iota(jnp.int32, (tm, tm), 1)
    earlier = jnp.where(r < c, 1.0, 0.0).astype(BF16)
    before = _dot(chosen_f.astype(BF16), earlier) + cnt_ref[:, 0:1]
    cnt_ref[...] += jnp.sum(chosen_f, axis=1, keepdims=True)

    row8 = lax.broadcasted_iota(jnp.int32, (2 * TOP_K, tm), 0)
    ri = jnp.zeros((2 * TOP_K, tm), jnp.int32)
    rg = jnp.zeros((2 * TOP_K, tm), F32)
    for k in range(TOP_K):
        rank = jnp.sum(jnp.where(hots[k], before, 0.0), axis=0, keepdims=True).astype(jnp.int32)
        ri = jnp.where(row8 == k, idxs[k], ri)
        ri = jnp.where(row8 == TOP_K + k, rank, ri)
        rg = jnp.where(row8 == k, exps[k] / denom, rg)
    ri_ref[...] = ri
    rg_ref[...] = jnp.concatenate([rg, jnp.zeros((LANE - 2 * TOP_K, tm), F32)], axis=0).T


def _out_proj(x, mixed, w_o, g, wr_hi, wr_lo, b_r, *, tm):
    t, d = x.shape
    ne = wr_hi.shape[0]
    return pl.pallas_call(
        _out_kernel,
        out_shape=(jax.ShapeDtypeStruct((t, d), F32),
                   jax.ShapeDtypeStruct((t, d), BF16),
                   jax.ShapeDtypeStruct((2 * TOP_K, t), jnp.int32),
                   jax.ShapeDtypeStruct((t, LANE), F32),
                   jax.ShapeDtypeStruct((ne, LANE), F32)),
        grid=(t // tm,),
        in_specs=[
            pl.BlockSpec((tm, d), lambda i: (i, 0)),
            pl.BlockSpec((tm, d), lambda i: (i, 0)),
            pl.BlockSpec((d, d), lambda i: (0, 0)),
            pl.BlockSpec((1, d), lambda i: (0, 0)),
            pl.BlockSpec((ne, d), lambda i: (0, 0)),
            pl.BlockSpec((ne, d), lambda i: (0, 0)),
            pl.BlockSpec((ne, LANE), lambda i: (0, 0)),
        ],
        out_specs=(pl.BlockSpec((tm, d), lambda i: (i, 0)),
                   pl.BlockSpec((tm, d), lambda i: (i, 0)),
                   pl.BlockSpec((2 * TOP_K, tm), lambda i: (0, i)),
                   pl.BlockSpec((tm, LANE), lambda i: (i, 0)),
                   pl.BlockSpec((ne, LANE), lambda i: (0, 0))),
        compiler_params=_cparams(("arbitrary",)),
        name="out_proj",
    )(x, mixed, w_o, g, wr_hi, wr_lo, b_r)


def _expert_changed(be_ref, m):
    return jnp.logical_or(m == 0, be_ref[m] != be_ref[jnp.maximum(m - 1, 0)])


def _next_run_expert(block_expert, n_used):
    nb = block_expert.shape[0]
    idx = jnp.arange(nb, dtype=jnp.int32)
    starts = jnp.logical_or(idx == 0, block_expert != jnp.roll(block_expert, 1))
    cand = jnp.where(jnp.logical_and(starts, idx < n_used[0]), idx, nb)
    later_start = jnp.concatenate([lax.cummin(cand, axis=0, reverse=True)[1:],
                                   jnp.full((1,), nb, jnp.int32)])
    at_start = jnp.sum(jnp.where(idx[None, :] == later_start[:, None], block_expert[None, :], 0), axis=1)
    return jnp.where(later_start < nb, at_start, -1).astype(jnp.int32)


def _moe_up_kernel(be_ref, nu_ref, nx_ref, x_ref, w_hbm, bg_ref, bu_ref, o_ref,
                   stage, w_scr, sem):
    n = pl.program_id(0)
    m = pl.program_id(1)
    nf = pl.num_programs(0)
    tf = w_scr.shape[2]

    def fetch(e, nn, part):
        col = pl.multiple_of((nn + part * nf) * tf, tf)
        return pltpu.make_async_copy(w_hbm.at[e, :, pl.ds(col, tf)], stage.at[part], sem.at[part])

    def start(e, nn):
        fetch(e, nn, 0).start()
        fetch(e, nn, 1).start()

    @pl.when(jnp.logical_and(jnp.logical_and(n == 0, m == 0), nu_ref[0] > 0))
    def _():
        start(be_ref[0], 0)

    @pl.when(m < nu_ref[0])
    def _():
        @pl.when(_expert_changed(be_ref, m))
        def _():
            fetch(be_ref[m], n, 0).wait()
            fetch(be_ref[m], n, 1).wait()
            def convert(i, carry):
                rows = pl.ds(pl.multiple_of(i * CAST_ROWS, CAST_ROWS), CAST_ROWS)
                w_scr[:, rows, :] = stage[:, rows, :].astype(BF16)
                return carry

            lax.fori_loop(0, stage.shape[1] // CAST_ROWS, convert, 0)
            nxt = nx_ref[m]

            @pl.when(nxt >= 0)
            def _():
                start(nxt, n)

            @pl.when(jnp.logical_and(nxt < 0, n + 1 < nf))
            def _():
                start(be_ref[0], n + 1)

        x = x_ref[...]
        gate = jnp.minimum(_dot(x, w_scr[0]) + bg_ref[0], SWIGLU_LIMIT)
        up = jnp.clip(_dot(x, w_scr[1]) + bu_ref[0], -SWIGLU_LIMIT, SWIGLU_LIMIT)
        o_ref[...] = ((up + 1.0) * gate * _sigmoid(SWIGLU_ALPHA * gate)).astype(o_ref.dtype)

    @pl.when(m >= nu_ref[0])
    def _():
        o_ref[...] = jnp.zeros_like(o_ref)


def _moe_up(block_expert, n_used, x_pad, w_gu, b_gu, *, tf):
    rows, d = x_pad.shape
    nb = rows // MOE_TM
    d_ff = w_gu.shape[2] // 2
    nf = d_ff // tf

    def blk(m, nu):
        return jnp.maximum(jnp.minimum(m, nu[0] - 1), 0)

    return pl.pallas_call(
        _moe_up_kernel,
        out_shape=jax.ShapeDtypeStruct((rows, d_ff), BF16),
        grid_spec=pltpu.PrefetchScalarGridSpec(
            num_scalar_prefetch=3,
            grid=(nf, nb),
            in_specs=[
                pl.BlockSpec((MOE_TM, d), lambda n, m, be, nu, nx: (blk(m, nu), 0)),
                pl.BlockSpec(memory_space=pl.ANY),
                pl.BlockSpec((1, 1, tf), lambda n, m, be, nu, nx: (be[blk(m, nu)], 0, n)),
                pl.BlockSpec((1, 1, tf), lambda n, m, be, nu, nx: (be[blk(m, nu)], 0, n + nf)),
            ],
            out_specs=pl.BlockSpec((MOE_TM, tf), lambda n, m, be, nu, nx: (m, n)),
            scratch_shapes=[pltpu.VMEM((2, d, tf), F32), pltpu.VMEM((2, d, tf), BF16),
                            pltpu.SemaphoreType.DMA((2,))],
        ),
        compiler_params=_cparams(("arbitrary", "arbitrary")),
        name="moe_up",
    )(block_expert, n_used, _next_run_expert(block_expert, n_used), x_pad, w_gu, b_gu, b_gu)


def _moe_down_kernel(be_ref, nu_ref, nx_ref, *refs, chunk_starts):
    a_refs, (w_hbm, bd_ref, o_ref, stage, wd_scr, sem) = refs[:-6], refs[-6:]
    m = pl.program_id(0)

    def fetch(e):
        return pltpu.make_async_copy(w_hbm.at[e], stage, sem.at[0])

    @pl.when(jnp.logical_and(m == 0, nu_ref[0] > 0))
    def _():
        fetch(be_ref[0]).start()

    @pl.when(m < nu_ref[0])
    def _():
        @pl.when(_expert_changed(be_ref, m))
        def _():
            fetch(be_ref[m]).wait()
            wd_scr[...] = stage[...].astype(BF16)
            nxt = nx_ref[m]

            @pl.when(nxt >= 0)
            def _():
                fetch(nxt).start()

        for c, a_ref in enumerate(a_refs):
            @pl.when(jnp.logical_and(m >= chunk_starts[c], m < chunk_starts[c + 1]))
            def _(a_ref=a_ref):
                o_ref[...] = (_dot(a_ref[...], wd_scr[...]) + bd_ref[0]).astype(o_ref.dtype)

    @pl.when(m >= nu_ref[0])
    def _():
        o_ref[...] = jnp.zeros_like(o_ref)


def _moe_down(block_expert, n_used, acts, w_d, b_d):
    d_ff = acts[0].shape[1]
    chunk_starts = [0]
    for a in acts:
        chunk_starts.append(chunk_starts[-1] + a.shape[0] // MOE_TM)
    nb = chunk_starts[-1]
    d = w_d.shape[2]

    def blk(m, nu):
        return jnp.maximum(jnp.minimum(m, nu[0] - 1), 0)

    def act_spec(c):
        lo, n_c = chunk_starts[c], chunk_starts[c + 1] - chunk_starts[c]
        return pl.BlockSpec((MOE_TM, d_ff),
                            lambda m, be, nu, nx: (jnp.clip(blk(m, nu) - lo, 0, n_c - 1), 0))

    return pl.pallas_call(
        functools.partial(_moe_down_kernel, chunk_starts=tuple(chunk_starts)),
        out_shape=jax.ShapeDtypeStruct((nb * MOE_TM, d), BF16),
        grid_spec=pltpu.PrefetchScalarGridSpec(
            num_scalar_prefetch=3,
            grid=(nb,),
            in_specs=[act_spec(c) for c in range(len(acts))] + [
                pl.BlockSpec(memory_space=pl.ANY),
                pl.BlockSpec((1, 1, d), lambda m, be, nu, nx: (be[blk(m, nu)], 0, 0)),
            ],
            out_specs=pl.BlockSpec((MOE_TM, d), lambda m, be, nu, nx: (m, 0)),
            scratch_shapes=[pltpu.VMEM((d_ff, d), F32), pltpu.VMEM((d_ff, d), BF16),
                            pltpu.SemaphoreType.DMA((1,))],
        ),
        compiler_params=_cparams(("arbitrary",)),
        name="moe_down",
    )(block_expert, n_used, _next_run_expert(block_expert, n_used), *acts, w_d, b_d)


def _combine_kernel(x_ref, y0_ref, y1_ref, y2_ref, y3_ref, rg_ref, g_ref, o_ref, *, normalize):
    x = x_ref[...]
    rg = rg_ref[...]
    for k, y_ref in enumerate((y0_ref, y1_ref, y2_ref, y3_ref)):
        x = x + rg[:, k:k + 1] * y_ref[...].astype(F32)
    if normalize:
        ms = jnp.mean(x * x, axis=-1, keepdims=True)
        x = x * lax.rsqrt(ms + RMS_EPS) * g_ref[...]
    o_ref[...] = x


def _combine(x_buf, y_rows, rg, g, *, chunk, n_chunk_tokens, normalize, tm):
    t, d = x_buf.shape
    nblk = n_chunk_tokens // tm
    x_spec = pl.BlockSpec((tm, d), lambda i: (chunk * nblk + i, 0))

    def y_spec(k):
        return pl.BlockSpec((tm, d), lambda i: (k * nblk + i, 0))

    return pl.pallas_call(
        functools.partial(_combine_kernel, normalize=normalize),
        out_shape=jax.ShapeDtypeStruct((t, d), F32),
        grid=(nblk,),
        in_specs=[x_spec] + [y_spec(k) for k in range(TOP_K)] + [
            pl.BlockSpec((tm, rg.shape[1]), lambda i: (chunk * nblk + i, 0)),
            pl.BlockSpec((1, d), lambda i: (0, 0)),
        ],
        out_specs=x_spec,
        input_output_aliases={0: 0},
        compiler_params=_cparams(("parallel",)),
        name="combine_norm",
    )(x_buf, y_rows, y_rows, y_rows, y_rows, rg, g)


def _route(ri, counts, t):
    p_n = t * TOP_K
    experts = ri[:TOP_K]
    rank = ri[TOP_K:]
    counts = counts[:, 0].astype(jnp.int32)
    padded = (counts + MOE_TM - 1) // MOE_TM * MOE_TM
    pad_end = jnp.cumsum(padded)
    pad_start = pad_end - padded
    e_ids = jnp.arange(N_EXPERTS, dtype=jnp.int32)[:, None, None]
    dest = rank + jnp.sum(jnp.where(experts[None] == e_ids, pad_start[:, None, None], 0), axis=0)
    nb = -(-p_n // MOE_TM) + N_EXPERTS
    src_tok = (jnp.arange(nb * MOE_TM, dtype=jnp.int32) % t).at[dest.reshape(p_n)].set(
        jnp.tile(jnp.arange(t, dtype=jnp.int32), TOP_K), mode="promise_in_bounds",
        unique_indices=True)
    block_start = jnp.arange(nb, dtype=jnp.int32) * MOE_TM
    block_expert = jnp.minimum(
        jnp.sum((block_start[:, None] >= pad_end[None, :]).astype(jnp.int32), axis=1),
        N_EXPERTS - 1)
    n_used = (pad_end[-1] // MOE_TM).astype(jnp.int32).reshape(1)
    return dest, src_tok, block_expert, n_used


def kernel(x, norm_mix, w_in, conv_w, conv_b, b_gates_if, norm_head, w_proj_a, w_proj_b,
           w_merge_gate, b_merge_gate, w_out, norm_ffn, w_router, b_router, w_gate_up,
           b_gate_up, w_down, b_down, norm_final):
    bn, s, d = x.shape
    assert bn == 1
    depth = norm_mix.shape[0]
    xt = x.reshape(s, d)
    tm = min(1024, s)
    aqk = A_HEADS * A_QK
    av = A_HEADS * A_V
    bw = B_HEADS * B_HEAD_DIM

    for l in range(depth):
        wl = w_in[l]
        c0 = 2 * aqk
        c1 = c0 + 2 * av
        c2 = c1 + 2 * A_HEADS
        w_f32 = jnp.concatenate(
            [wl[:, :c0], wl[:, c1:c2], jnp.zeros((d, LANE - 2 * A_HEADS), F32)], axis=1).astype(BF16)
        b_f32 = jnp.concatenate(
            [jnp.zeros((c0,), F32), b_gates_if[l], jnp.zeros((LANE - 2 * A_HEADS,), F32)])[None, :]
        w_bf = jnp.concatenate([wl[:, c0:c1], wl[:, c2:c2 + bw] * (B_HEAD_DIM ** -0.5), wl[:, c2 + bw:]],
                               axis=1).astype(BF16)
        g_mix = norm_mix[l][None, :]

        p_f32, h1, if_row = _norm_proj(xt, g_mix, w_f32, b_f32, tm=tm)
        tn = 1024
        o_tile = av // tn
        p_bf = _proj(h1, w_bf, jnp.zeros((1, w_bf.shape[1]), F32),
                     sigmoid_tiles=(o_tile, 2 * o_tile), tm=tm, tn=tn)
        gate_tn = 2 * tn
        gates = _proj(h1, w_merge_gate[l].astype(BF16), b_merge_gate[l][None, :],
                      sigmoid_tiles=(0, w_merge_gate.shape[2] // gate_tn), tm=tm, tn=gate_tn)

        y_a = _mlstm(p_f32, p_bf, if_row, conv_w[l], conv_b[l][None, :], norm_head[l][None, :])
        qb = 2 * av // B_HEAD_DIM
        y_b = _stick_breaking(p_bf, qb, qb + B_HEADS, qb + 2 * B_HEADS)

        mixed = _mix(gates, y_a, w_proj_a[l].astype(BF16), y_b, w_proj_b[l].astype(BF16),
                     tm=tm, tn=d)
        wr_hi, wr_lo = _split_bf16(w_router[l].T)
        b_r = jnp.broadcast_to(b_router[l][:, None], (N_EXPERTS, LANE))
        x1, h2, ri, rg, counts = _out_proj(xt, mixed, w_out[l].astype(BF16), norm_ffn[l][None, :],
                                           wr_hi, wr_lo, b_r, tm=min(512, s))

        dest, src_tok, block_expert, n_used = _route(ri, counts, s)
        nb = block_expert.shape[0]
        bounds = [0] + [nb * f // 10 for f in MOE_CHUNK_TENTHS] + [nb]
        acts = []
        for lo, hi in zip(bounds[:-1], bounds[1:]):
            x_pad = h2.at[src_tok[lo * MOE_TM:hi * MOE_TM]].get(mode="promise_in_bounds")
            acts.append(_moe_up(block_expert[lo:hi], jnp.clip(n_used - lo, 0, hi - lo),
                                x_pad, w_gate_up[l], b_gate_up[l][:, None, :], tf=1024))
        y_pad = _moe_down(block_expert, n_used, acts, w_down[l], b_down[l][:, None, :])
        nct = s // COMBINE_CHUNKS
        xt = x1
        for j in range(COMBINE_CHUNKS):
            idx = dest[:, j * nct:(j + 1) * nct].reshape(TOP_K * nct)
            y_rows = y_pad.at[idx].get(mode="promise_in_bounds")
            xt = _combine(xt, y_rows, rg, norm_final[None, :], chunk=j, n_chunk_tokens=nct,
                          normalize=(l + 1 == depth), tm=min(512, nct))
    return xt.reshape(bn, s, d)
```

```python
import functools

import jax
import jax.numpy as jnp
from jax import lax
from jax.experimental import pallas as pl
from jax.experimental.pallas import tpu as pltpu

F32 = jnp.float32
BF16 = jnp.bfloat16

RMS_EPS = 1e-5
A_HEADS = 4
A_QK = 128
A_V = 256
CONV_W = 4
B_HEADS = 8
B_HEAD_DIM = 128
N_EXPERTS = 32
TOP_K = 4
SWIGLU_LIMIT = 7.0
SWIGLU_ALPHA = 1.702

LANE = 128
SUBLANE = 8
VMEM_LIMIT = 60 * 1024 * 1024

MLSTM_CHUNK = 128
SB_BQ = 4096
SB_SUB = 128
SB_SKIP = 88.0
MOE_TM = 512
CAST_ROWS = 32
MOE_CHUNK_TENTHS = (1, 4)
COMBINE_CHUNKS = 8


def _cparams(sem):
    return pltpu.CompilerParams(dimension_semantics=sem, vmem_limit_bytes=VMEM_LIMIT)


def _log_sigmoid(z):
    return jnp.minimum(z, 0.0) - jnp.log1p(jnp.exp(-jnp.abs(z)))


def _sigmoid(z):
    return 1.0 / (1.0 + jnp.exp(-z))


def _split_bf16(x):
    hi = x.astype(BF16)
    lo = (x - hi.astype(F32)).astype(BF16)
    return hi, lo


def _dot(a, b):
    return jnp.dot(a, b, preferred_element_type=F32)


def _dot_nt(a, b):
    return lax.dot_general(a, b, (((1,), (1,)), ((), ())), preferred_element_type=F32)


def _dot_tn(a, b):
    return lax.dot_general(a, b, (((0,), (0,)), ((), ())), preferred_element_type=F32)


def _norm_proj_kernel(x_ref, g_ref, w_ref, b_ref, o_ref, h_ref, gt_ref):
    x = x_ref[...]
    ms = jnp.mean(x * x, axis=-1, keepdims=True)
    h = (x * lax.rsqrt(ms + RMS_EPS) * g_ref[...]).astype(BF16)
    h_ref[...] = h
    acc = _dot(h, w_ref[...]) + b_ref[...]
    o_ref[...] = acc
    gt_ref[...] = acc[:, acc.shape[1] - LANE:].T[:SUBLANE, :]


def _norm_proj(x, g, w, b, *, tm):
    t, d = x.shape
    n = w.shape[1]
    return pl.pallas_call(
        _norm_proj_kernel,
        out_shape=(jax.ShapeDtypeStruct((t, n), F32),
                   jax.ShapeDtypeStruct((t, d), BF16),
                   jax.ShapeDtypeStruct((SUBLANE, t), F32)),
        grid=(t // tm,),
        in_specs=[
            pl.BlockSpec((tm, d), lambda i: (i, 0)),
            pl.BlockSpec((1, d), lambda i: (0, 0)),
            pl.BlockSpec((d, n), lambda i: (0, 0)),
            pl.BlockSpec((1, n), lambda i: (0, 0)),
        ],
        out_specs=(pl.BlockSpec((tm, n), lambda i: (i, 0)),
                   pl.BlockSpec((tm, d), lambda i: (i, 0)),
                   pl.BlockSpec((SUBLANE, tm), lambda i: (0, i))),
        compiler_params=_cparams(("parallel",)),
        name="norm_proj",
    )(x, g, w, b)


def _proj_kernel(h_ref, w_ref, b_ref, o_ref, *, sigmoid_tiles, n_tiles):
    acc = _dot(h_ref[...], w_ref[...]) + b_ref[...]
    lo, hi = sigmoid_tiles
    if lo == 0 and hi == n_tiles:
        o_ref[...] = _sigmoid(acc).astype(o_ref.dtype)
    elif lo == hi:
        o_ref[...] = acc.astype(o_ref.dtype)
    else:
        j = pl.program_id(1)
        gated = jnp.logical_and(j >= lo, j < hi)

        @pl.when(gated)
        def _():
            o_ref[...] = _sigmoid(acc).astype(o_ref.dtype)

        @pl.when(jnp.logical_not(gated))
        def _():
            o_ref[...] = acc.astype(o_ref.dtype)


def _proj(h, w, b, *, sigmoid_tiles, tm, tn):
    t, d = h.shape
    n = w.shape[1]
    return pl.pallas_call(
        functools.partial(_proj_kernel, sigmoid_tiles=sigmoid_tiles, n_tiles=n // tn),
        out_shape=jax.ShapeDtypeStruct((t, n), BF16),
        grid=(t // tm, n // tn),
        in_specs=[
            pl.BlockSpec((tm, d), lambda i, j: (i, 0)),
            pl.BlockSpec((d, tn), lambda i, j: (0, j)),
            pl.BlockSpec((1, tn), lambda i, j: (0, j)),
        ],
        out_specs=pl.BlockSpec((tm, tn), lambda i, j: (i, j)),
        compiler_params=_cparams(("parallel", "arbitrary")),
        name="proj",
    )(h, w, b)


def _mlstm_kernel(qk_ref, v_ref, o_ref, ifc_ref, ifr_ref, cw_ref, cb_ref, nh_ref,
                  y_ref, ext_scr, c_scr, n_scr, m_scr):
    L = MLSTM_CHUNK
    step = pl.program_id(0)

    @pl.when(step == 0)
    def _():
        ext_scr[0:SUBLANE, :] = jnp.zeros((SUBLANE, ext_scr.shape[1]), F32)
        c_scr[...] = jnp.zeros_like(c_scr)
        n_scr[...] = jnp.zeros_like(n_scr)
        m_scr[...] = jnp.zeros_like(m_scr)

    raw = qk_ref[...]
    ext_scr[SUBLANE:SUBLANE + L, :] = raw
    conv = cb_ref[...]
    for j in range(CONV_W):
        off = SUBLANE - (CONV_W - 1) + j
        conv = conv + ext_scr[off:off + L, :] * cw_ref[j:j + 1, :]
    ext_scr[0:SUBLANE, :] = raw[L - SUBLANE:L, :]
    qk = conv * _sigmoid(conv)

    row = lax.broadcasted_iota(jnp.int32, (L, L), 0)
    col = lax.broadcasted_iota(jnp.int32, (L, L), 1)
    causal = row >= col
    tri_incl = jnp.where(causal, 1.0, 0.0).astype(BF16)
    tri_incl_t = jnp.where(col >= row, 1.0, 0.0).astype(BF16)

    hi, lo = _split_bf16(_log_sigmoid(ifc_ref[...]))
    a_cols = _dot(tri_incl, hi) + _dot(tri_incl, lo)
    hi, lo = _split_bf16(_log_sigmoid(ifr_ref[...]))
    a_rows = _dot(hi, tri_incl_t) + _dot(lo, tri_incl_t)

    for h in range(A_HEADS):
        q = qk[:, h * A_QK:(h + 1) * A_QK]
        k = qk[:, A_HEADS * A_QK + h * A_QK:A_HEADS * A_QK + (h + 1) * A_QK] * (A_QK ** -0.5)
        v = v_ref[:, h * A_V:(h + 1) * A_V]
        q_bf = q.astype(BF16)
        k_bf = k.astype(BF16)

        i_col = ifc_ref[:, h:h + 1]
        i_row = ifr_ref[h:h + 1, :]
        a_col = a_cols[:, A_HEADS + h:A_HEADS + h + 1]
        a_row = a_rows[A_HEADS + h:A_HEADS + h + 1, :]
        g_tot = a_col[L - 1:L, :]

        m_prev = m_scr[h]
        n_prev = n_scr[h]
        ct_prev = c_scr[h]

        m_inter = a_col + m_prev
        d_log = jnp.where(causal, a_col - a_row + i_row, -jnp.inf)
        m_t = jnp.maximum(m_inter, jnp.max(d_log, axis=-1, keepdims=True))
        p = jnp.exp(d_log - m_t) * _dot_nt(q_bf, k_bf)
        s_inter = jnp.exp(m_inter - m_t)
        num = s_inter * _dot(q_bf, ct_prev.astype(BF16)) + _dot(p.astype(BF16), v)
        den = (s_inter * jnp.sum(q * n_prev, axis=-1, keepdims=True)
               + jnp.sum(p, axis=-1, keepdims=True))
        hh = num / jnp.maximum(jnp.abs(den), jnp.exp(-m_t))

        hh = hh * lax.rsqrt(jnp.mean(hh * hh, axis=-1, keepdims=True) + RMS_EPS)
        gate = o_ref[:, h * A_V:(h + 1) * A_V].astype(F32)
        y_ref[:, h * A_V:(h + 1) * A_V] = (
            hh * nh_ref[:, h * A_V:(h + 1) * A_V] * gate).astype(y_ref.dtype)

        w_col = g_tot - a_col + i_col
        m_loc = jnp.max(w_col, axis=0, keepdims=True)
        ke = k * jnp.exp(w_col - m_loc)
        ct_loc = _dot_tn(ke.astype(BF16), v)
        n_loc = jnp.sum(ke, axis=0, keepdims=True)
        m_new = jnp.maximum(g_tot + m_prev, m_loc)
        s_old = jnp.exp(g_tot + m_prev - m_new)
        s_new = jnp.exp(m_loc - m_new)
        c_scr[h] = s_old * ct_prev + s_new * ct_loc
        n_scr[h] = s_old * n_prev + s_new * n_loc
        m_scr[h] = m_new


def _mlstm(p_f32, p_bf, if_row, conv_w, conv_b, norm_head):
    t = p_f32.shape[0]
    L = MLSTM_CHUNK
    wq = conv_w.shape[1]
    wv = norm_head.shape[1]
    return pl.pallas_call(
        _mlstm_kernel,
        out_shape=jax.ShapeDtypeStruct((t, wv), BF16),
        grid=(t // L,),
        in_specs=[
            pl.BlockSpec((L, wq), lambda i: (i, 0)),
            pl.BlockSpec((L, wv), lambda i: (i, 0)),
            pl.BlockSpec((L, wv), lambda i: (i, 1)),
            pl.BlockSpec((L, LANE), lambda i: (i, wq // LANE)),
            pl.BlockSpec((SUBLANE, L), lambda i: (0, i)),
            pl.BlockSpec((CONV_W, wq), lambda i: (0, 0)),
            pl.BlockSpec((1, wq), lambda i: (0, 0)),
            pl.BlockSpec((1, wv), lambda i: (0, 0)),
        ],
        out_specs=pl.BlockSpec((L, wv), lambda i: (i, 0)),
        scratch_shapes=[
            pltpu.VMEM((L + SUBLANE, wq), F32),
            pltpu.VMEM((A_HEADS, A_QK, A_V), F32),
            pltpu.VMEM((A_HEADS, 1, A_QK), F32),
            pltpu.VMEM((A_HEADS, 1, 1), F32),
        ],
        compiler_params=_cparams(("arbitrary",)),
        name="mlstm",
    )(p_f32, p_bf, p_bf, p_f32, if_row, conv_w, conv_b, norm_head)


def _later(n):
    j = lax.broadcasted_iota(jnp.int32, (n, n), 0)
    s = lax.broadcasted_iota(jnp.int32, (n, n), 1)
    return jnp.where(j > s, 1.0, 0.0).astype(BF16)


def _sb_logs(z):
    ls = jnp.minimum(z, 0.0) - jnp.log(1.0 + jnp.exp(-jnp.abs(z)))
    return ls, ls - z


def _sb_later_sums(lk, later2):
    hi, lo = _split_bf16(lk)
    return _dot(jnp.concatenate([hi, lo], axis=1), later2)


def _sb_tile(q, k, v, carry, later2):
    ls, lk = _sb_logs(_dot_nt(q, k))
    r_in = _sb_later_sums(lk, later2)
    w = jnp.exp(ls + r_in + carry)
    return _dot(w.astype(BF16), v), carry + r_in[:, 0:1] + lk[:, 0:1]


def _sb_kernel(q_ref, kc_ref, vc_ref, kp_ref, vp_ref, k_ref, v_ref, o_ref, acc_scr, carry_scr):
    sub = SB_SUB
    nsub = q_ref.shape[0] // sub
    step = pl.program_id(1)
    later = _later(sub)
    later2 = jnp.concatenate([later, later], axis=0)
    row = lax.broadcasted_iota(jnp.int32, (sub, sub), 0)
    col = lax.broadcasted_iota(jnp.int32, (sub, sub), 1)
    diag_past = col < row
    has_prev = step > 0

    blk = [slice(j * sub, (j + 1) * sub) for j in range(nsub)]
    qs = [q_ref[b, :] for b in blk]
    k_d = [kc_ref[b, :] for b in blk]
    v_d = [vc_ref[b, :] for b in blk]
    k_p = [kp_ref[...]] + k_d[:-1]
    v_p = [vp_ref[...]] + v_d[:-1]
    ls_d, lk_d, ls_p, lk_p = [], [], [], []
    for j in range(nsub):
        ls, lk = _sb_logs(_dot_nt(qs[j], k_d[j]))
        ls_d.append(ls)
        lk_d.append(jnp.where(diag_past, lk, 0.0))
        ls, lk = _sb_logs(_dot_nt(qs[j], k_p[j]))
        ls_p.append(ls)
        lk_p.append(jnp.where(has_prev, lk, 0.0) if j == 0 else lk)
    r_in = _sb_later_sums(jnp.concatenate(lk_d + lk_p, axis=0), later2)
    first_max = []
    for j in range(nsub):
        r_d = r_in[j * sub:(j + 1) * sub]
        r_p = r_in[(nsub + j) * sub:(nsub + j + 1) * sub]
        carry_d = r_d[:, 0:1] + lk_d[j][:, 0:1]
        w_d = jnp.where(diag_past, jnp.exp(ls_d[j] + r_d), 0.0)
        w_p = jnp.exp(ls_p[j] + r_p + carry_d)
        if j == 0:
            w_p = jnp.where(has_prev, w_p, 0.0)
        carry = carry_d + r_p[:, 0:1] + lk_p[j][:, 0:1]
        acc_scr[j] = _dot(w_d.astype(BF16), v_d[j]) + _dot(w_p.astype(BF16), v_p[j])
        carry_scr[j] = carry
        first_max.append(jnp.max(carry))

    for j in range(nsub):
        def cond(state):
            kb, mx = state
            return jnp.logical_and(kb >= 0, mx > -SB_SKIP)

        def body(state, j=j):
            kb, _ = state
            k0 = pl.multiple_of(kb * sub, sub)
            acc, carry = _sb_tile(q_ref[j * sub:(j + 1) * sub, :], k_ref[pl.ds(k0, sub), :],
                                  v_ref[pl.ds(k0, sub), :], carry_scr[j], later2)
            acc_scr[j] += acc
            carry_scr[j] = carry
            return kb - 1, jnp.max(carry)

        lax.while_loop(cond, body, (step * nsub + j - 2, first_max[j]))
        o_ref[j * sub:(j + 1) * sub, :] = acc_scr[j].astype(o_ref.dtype)


def _stick_breaking(qkv, q_col, k_col, v_col):
    t = qkv.shape[0]
    d = B_HEAD_DIM
    bq = min(SB_BQ, t)
    nsub = bq // SB_SUB
    assert t % bq == 0 and bq % SB_SUB == 0

    def cur(col):
        return pl.BlockSpec((bq, d), lambda h, i: (i, col + h))

    def prev(col):
        return pl.BlockSpec((SB_SUB, d), lambda h, i: (jnp.maximum(i * nsub - 1, 0), col + h))

    def full(col):
        return pl.BlockSpec((t, d), lambda h, i: (0, col + h))

    return pl.pallas_call(
        _sb_kernel,
        out_shape=jax.ShapeDtypeStruct((t, B_HEADS * d), BF16),
        grid=(B_HEADS, t // bq),
        in_specs=[cur(q_col), cur(k_col), cur(v_col), prev(k_col), prev(v_col),
                  full(k_col), full(v_col)],
        out_specs=pl.BlockSpec((bq, d), lambda h, i: (i, h)),
        scratch_shapes=[pltpu.VMEM((nsub, SB_SUB, d), F32), pltpu.VMEM((nsub, SB_SUB, 1), F32)],
        compiler_params=_cparams(("parallel", "arbitrary")),
        name="stick_breaking",
    )(qkv, qkv, qkv, qkv, qkv, qkv, qkv)


def _mix_kernel(g1_ref, g2_ref, ya_ref, wa_ref, yb_ref, wb_ref, o_ref):
    pa = _dot(ya_ref[...], wa_ref[...])
    pb = _dot(yb_ref[...], wb_ref[...])
    o_ref[...] = (g1_ref[...].astype(F32) * pa + g2_ref[...].astype(F32) * pb).astype(o_ref.dtype)


def _mix(gates, y_a, w_a, y_b, w_b, *, tm, tn):
    t = y_a.shape[0]
    d = w_a.shape[1]
    nt = d // tn
    return pl.pallas_call(
        _mix_kernel,
        out_shape=jax.ShapeDtypeStruct((t, d), BF16),
        grid=(t // tm, nt),
        in_specs=[
            pl.BlockSpec((tm, tn), lambda i, j: (i, j)),
            pl.BlockSpec((tm, tn), lambda i, j: (i, j + nt)),
            pl.BlockSpec((tm, y_a.shape[1]), lambda i, j: (i, 0)),
            pl.BlockSpec((w_a.shape[0], tn), lambda i, j: (0, j)),
            pl.BlockSpec((tm, y_b.shape[1]), lambda i, j: (i, 0)),
            pl.BlockSpec((w_b.shape[0], tn), lambda i, j: (0, j)),
        ],
        out_specs=pl.BlockSpec((tm, tn), lambda i, j: (i, j)),
        compiler_params=_cparams(("parallel", "arbitrary")),
        name="mix",
    )(gates, gates, y_a, w_a, y_b, w_b)


def _out_kernel(x_ref, g1_ref, g2_ref, ya_ref, wa_ref, yb_ref, wb_ref, wo_ref, g_ref,
                wrh_ref, wrl_ref, br_ref, x1_ref, h2_ref, ri_ref, rg_ref, cnt_ref):
    tm = x_ref.shape[0]
    mixed = (g1_ref[...].astype(F32) * _dot(ya_ref[...], wa_ref[...])
             + g2_ref[...].astype(F32) * _dot(yb_ref[...], wb_ref[...])).astype(BF16)
    x1 = x_ref[...] + _dot(mixed, wo_ref[...])
    x1_ref[...] = x1
    ms = jnp.mean(x1 * x1, axis=-1, keepdims=True)
    h2 = x1 * lax.rsqrt(ms + RMS_EPS) * g_ref[...]
    h_hi, h_lo = _split_bf16(h2)
    h2_ref[...] = h_hi
    w_hi = wrh_ref[...]
    lg = (_dot_nt(w_hi, h_hi) + _dot_nt(w_hi, h_lo) + _dot_nt(wrl_ref[...], h_hi)
          + br_ref[:, 0:1])

    @pl.when(pl.program_id(0) == 0)
    def _():
        cnt_ref[...] = jnp.zeros_like(cnt_ref)

    row = lax.broadcasted_iota(jnp.int32, (N_EXPERTS, tm), 0)
    vals, idxs, hots = [], [], []
    for _ in range(TOP_K):
        mx = jnp.max(lg, axis=0, keepdims=True)
        idx = jnp.min(jnp.where(lg == mx, row, N_EXPERTS), axis=0, keepdims=True)
        hot = row == idx
        vals.append(mx)
        idxs.append(idx)
        hots.append(hot)
        lg = jnp.where(hot, -jnp.inf, lg)

    exps = [jnp.exp(v - vals[0]) for v in vals]
    denom = exps[0]
    for e in exps[1:]:
        denom = denom + e

    chosen = hots[0]
    for hot in hots[1:]:
        chosen = jnp.logical_or(chosen, hot)
    chosen_f = jnp.where(chosen, 1.0, 0.0)
    r = lax.broadcasted_iota(jnp.int32, (tm, tm), 0)
    c = lax.broadcasted_iota(jnp.int32, (tm, tm), 1)
    earlier = jnp.where(r < c, 1.0, 0.0).astype(BF16)
    before = _dot(chosen_f.astype(BF16), earlier) + cnt_ref[:, 0:1]
    cnt_ref[...] += jnp.sum(chosen_f, axis=1, keepdims=True)

    row8 = lax.broadcasted_iota(jnp.int32, (2 * TOP_K, tm), 0)
    ri = jnp.zeros((2 * TOP_K, tm), jnp.int32)
    rg = jnp.zeros((2 * TOP_K, tm), F32)
    for k in range(TOP_K):
        rank = jnp.sum(jnp.where(hots[k], before, 0.0), axis=0, keepdims=True).astype(jnp.int32)
        ri = jnp.where(row8 == k, idxs[k], ri)
        ri = jnp.where(row8 == TOP_K + k, rank, ri)
        rg = jnp.where(row8 == k, exps[k] / denom, rg)
    ri_ref[...] = ri
    rg_ref[...] = jnp.concatenate([rg, jnp.zeros((LANE - 2 * TOP_K, tm), F32)], axis=0).T


def _out_proj(x, gates, y_a, w_a, y_b, w_b, w_o, g, wr_hi, wr_lo, b_r, *, tm):
    t, d = x.shape
    ne = wr_hi.shape[0]

    def resident(shape):
        return pl.BlockSpec(shape, lambda i: (0, 0), pipeline_mode=pl.Buffered(1))

    return pl.pallas_call(
        _out_kernel,
        out_shape=(jax.ShapeDtypeStruct((t, d), F32),
                   jax.ShapeDtypeStruct((t, d), BF16),
                   jax.ShapeDtypeStruct((2 * TOP_K, t), jnp.int32),
                   jax.ShapeDtypeStruct((t, LANE), F32),
                   jax.ShapeDtypeStruct((ne, LANE), F32)),
        grid=(t // tm,),
        in_specs=[
            pl.BlockSpec((tm, d), lambda i: (i, 0)),
            pl.BlockSpec((tm, d), lambda i: (i, 0)),
            pl.BlockSpec((tm, d), lambda i: (i, 1)),
            pl.BlockSpec((tm, y_a.shape[1]), lambda i: (i, 0)),
            resident(w_a.shape),
            pl.BlockSpec((tm, y_b.shape[1]), lambda i: (i, 0)),
            resident(w_b.shape),
            resident((d, d)),
            pl.BlockSpec((1, d), lambda i: (0, 0)),
            pl.BlockSpec((ne, d), lambda i: (0, 0)),
            pl.BlockSpec((ne, d), lambda i: (0, 0)),
            pl.BlockSpec((ne, LANE), lambda i: (0, 0)),
        ],
        out_specs=(pl.BlockSpec((tm, d), lambda i: (i, 0)),
                   pl.BlockSpec((tm, d), lambda i: (i, 0)),
                   pl.BlockSpec((2 * TOP_K, tm), lambda i: (0, i)),
                   pl.BlockSpec((tm, LANE), lambda i: (i, 0)),
                   pl.BlockSpec((ne, LANE), lambda i: (0, 0))),
        compiler_params=_cparams(("arbitrary",)),
        name="out_proj",
    )(x, gates, gates, y_a, w_a, y_b, w_b, w_o, g, wr_hi, wr_lo, b_r)


def _expert_changed(be_ref, m):
    return jnp.logical_or(m == 0, be_ref[m] != be_ref[jnp.maximum(m - 1, 0)])


def _next_run_expert(block_expert, n_used):
    nb = block_expert.shape[0]
    idx = jnp.arange(nb, dtype=jnp.int32)
    starts = jnp.logical_or(idx == 0, block_expert != jnp.roll(block_expert, 1))
    cand = jnp.where(jnp.logical_and(starts, idx < n_used[0]), idx, nb)
    later_start = jnp.concatenate([lax.cummin(cand, axis=0, reverse=True)[1:],
                                   jnp.full((1,), nb, jnp.int32)])
    at_start = jnp.sum(jnp.where(idx[None, :] == later_start[:, None], block_expert[None, :], 0), axis=1)
    return jnp.where(later_start < nb, at_start, -1).astype(jnp.int32)


def _moe_up_kernel(be_ref, nu_ref, nx_ref, x_ref, w_hbm, bg_ref, bu_ref, o_ref,
                   stage, w_scr, sem):
    n = pl.program_id(0)
    m = pl.program_id(1)
    nf = pl.num_programs(0)
    tf = w_scr.shape[2]

    def fetch(e, nn, part):
        col = pl.multiple_of((nn + part * nf) * tf, tf)
        return pltpu.make_async_copy(w_hbm.at[e, :, pl.ds(col, tf)], stage.at[part], sem.at[part])

    def start(e, nn):
        fetch(e, nn, 0).start()
        fetch(e, nn, 1).start()

    @pl.when(jnp.logical_and(jnp.logical_and(n == 0, m == 0), nu_ref[0] > 0))
    def _():
        start(be_ref[0], 0)

    @pl.when(m < nu_ref[0])
    def _():
        @pl.when(_expert_changed(be_ref, m))
        def _():
            fetch(be_ref[m], n, 0).wait()
            fetch(be_ref[m], n, 1).wait()
            def convert(i, carry):
                rows = pl.ds(pl.multiple_of(i * CAST_ROWS, CAST_ROWS), CAST_ROWS)
                w_scr[:, rows, :] = stage[:, rows, :].astype(BF16)
                return carry

            lax.fori_loop(0, stage.shape[1] // CAST_ROWS, convert, 0)
            nxt = nx_ref[m]

            @pl.when(nxt >= 0)
            def _():
                start(nxt, n)

            @pl.when(jnp.logical_and(nxt < 0, n + 1 < nf))
            def _():
                start(be_ref[0], n + 1)

        x = x_ref[...]
        gate = jnp.minimum(_dot(x, w_scr[0]) + bg_ref[0], SWIGLU_LIMIT)
        up = jnp.clip(_dot(x, w_scr[1]) + bu_ref[0], -SWIGLU_LIMIT, SWIGLU_LIMIT)
        o_ref[...] = ((up + 1.0) * gate * _sigmoid(SWIGLU_ALPHA * gate)).astype(o_ref.dtype)

    @pl.when(m >= nu_ref[0])
    def _():
        o_ref[...] = jnp.zeros_like(o_ref)


def _moe_up(block_expert, n_used, x_pad, w_gu, b_gu, *, tf):
    rows, d = x_pad.shape
    nb = rows // MOE_TM
    d_ff = w_gu.shape[2] // 2
    nf = d_ff // tf

    def blk(m, nu):
        return jnp.maximum(jnp.minimum(m, nu[0] - 1), 0)

    return pl.pallas_call(
        _moe_up_kernel,
        out_shape=jax.ShapeDtypeStruct((rows, d_ff), BF16),
        grid_spec=pltpu.PrefetchScalarGridSpec(
            num_scalar_prefetch=3,
            grid=(nf, nb),
            in_specs=[
                pl.BlockSpec((MOE_TM, d), lambda n, m, be, nu, nx: (blk(m, nu), 0)),
                pl.BlockSpec(memory_space=pl.ANY),
                pl.BlockSpec((1, 1, tf), lambda n, m, be, nu, nx: (be[blk(m, nu)], 0, n)),
                pl.BlockSpec((1, 1, tf), lambda n, m, be, nu, nx: (be[blk(m, nu)], 0, n + nf)),
            ],
            out_specs=pl.BlockSpec((MOE_TM, tf), lambda n, m, be, nu, nx: (m, n)),
            scratch_shapes=[pltpu.VMEM((2, d, tf), F32), pltpu.VMEM((2, d, tf), BF16),
                            pltpu.SemaphoreType.DMA((2,))],
        ),
        compiler_params=_cparams(("arbitrary", "arbitrary")),
        name="moe_up",
    )(block_expert, n_used, _next_run_expert(block_expert, n_used), x_pad, w_gu, b_gu, b_gu)


def _moe_down_kernel(be_ref, nu_ref, nx_ref, *refs, chunk_starts):
    a_refs, (w_hbm, bd_ref, o_ref, stage, wd_scr, sem) = refs[:-6], refs[-6:]
    m = pl.program_id(0)

    def fetch(e):
        return pltpu.make_async_copy(w_hbm.at[e], stage, sem.at[0])

    @pl.when(jnp.logical_and(m == 0, nu_ref[0] > 0))
    def _():
        fetch(be_ref[0]).start()

    @pl.when(m < nu_ref[0])
    def _():
        @pl.when(_expert_changed(be_ref, m))
        def _():
            fetch(be_ref[m]).wait()
            wd_scr[...] = stage[...].astype(BF16)
            nxt = nx_ref[m]

            @pl.when(nxt >= 0)
            def _():
                fetch(nxt).start()

        for c, a_ref in enumerate(a_refs):
            @pl.when(jnp.logical_and(m >= chunk_starts[c], m < chunk_starts[c + 1]))
            def _(a_ref=a_ref):
                o_ref[...] = (_dot(a_ref[...], wd_scr[...]) + bd_ref[0]).astype(o_ref.dtype)

    @pl.when(m >= nu_ref[0])
    def _():
        o_ref[...] = jnp.zeros_like(o_ref)


def _moe_down(block_expert, n_used, acts, w_d, b_d):
    d_ff = acts[0].shape[1]
    chunk_starts = [0]
    for a in acts:
        chunk_starts.append(chunk_starts[-1] + a.shape[0] // MOE_TM)
    nb = chunk_starts[-1]
    d = w_d.shape[2]

    def blk(m, nu):
        return jnp.maximum(jnp.minimum(m, nu[0] - 1), 0)

    def act_spec(c):
        lo, n_c = chunk_starts[c], chunk_starts[c + 1] - chunk_starts[c]
        return pl.BlockSpec((MOE_TM, d_ff),
                            lambda m, be, nu, nx: (jnp.clip(blk(m, nu) - lo, 0, n_c - 1), 0))

    return pl.pallas_call(
        functools.partial(_moe_down_kernel, chunk_starts=tuple(chunk_starts)),
        out_shape=jax.ShapeDtypeStruct((nb * MOE_TM, d), BF16),
        grid_spec=pltpu.PrefetchScalarGridSpec(
            num_scalar_prefetch=3,
            grid=(nb,),
            in_specs=[act_spec(c) for c in range(len(acts))] + [
                pl.BlockSpec(memory_space=pl.ANY),
                pl.BlockSpec((1, 1, d), lambda m, be, nu, nx: (be[blk(m, nu)], 0, 0)),
            ],
            out_specs=pl.BlockSpec((MOE_TM, d), lambda m, be, nu, nx: (m, 0)),
            scratch_shapes=[pltpu.VMEM((d_ff, d), F32), pltpu.VMEM((d_ff, d), BF16),
                            pltpu.SemaphoreType.DMA((1,))],
        ),
        compiler_params=_cparams(("arbitrary",)),
        name="moe_down",
    )(block_expert, n_used, _next_run_expert(block_expert, n_used), *acts, w_d, b_d)


def _combine_kernel(x_ref, y0_ref, y1_ref, y2_ref, y3_ref, rg_ref, g_ref, o_ref, *, normalize):
    x = x_ref[...]
    rg = rg_ref[...]
    for k, y_ref in enumerate((y0_ref, y1_ref, y2_ref, y3_ref)):
        x = x + rg[:, k:k + 1] * y_ref[...].astype(F32)
    if normalize:
        ms = jnp.mean(x * x, axis=-1, keepdims=True)
        x = x * lax.rsqrt(ms + RMS_EPS) * g_ref[...]
    o_ref[...] = x


def _combine(x_buf, y_rows, rg, g, *, chunk, n_chunk_tokens, normalize, tm):
    t, d = x_buf.shape
    nblk = n_chunk_tokens // tm
    x_spec = pl.BlockSpec((tm, d), lambda i: (chunk * nblk + i, 0))

    def y_spec(k):
        return pl.BlockSpec((tm, d), lambda i: (k * nblk + i, 0))

    return pl.pallas_call(
        functools.partial(_combine_kernel, normalize=normalize),
        out_shape=jax.ShapeDtypeStruct((t, d), F32),
        grid=(nblk,),
        in_specs=[x_spec] + [y_spec(k) for k in range(TOP_K)] + [
            pl.BlockSpec((tm, rg.shape[1]), lambda i: (chunk * nblk + i, 0)),
            pl.BlockSpec((1, d), lambda i: (0, 0)),
        ],
        out_specs=x_spec,
        input_output_aliases={0: 0},
        compiler_params=_cparams(("parallel",)),
        name="combine_norm",
    )(x_buf, y_rows, y_rows, y_rows, y_rows, rg, g)


def _route(ri, counts, t):
    p_n = t * TOP_K
    experts = ri[:TOP_K]
    rank = ri[TOP_K:]
    counts = counts[:, 0].astype(jnp.int32)
    padded = (counts + MOE_TM - 1) // MOE_TM * MOE_TM
    pad_end = jnp.cumsum(padded)
    pad_start = pad_end - padded
    e_ids = jnp.arange(N_EXPERTS, dtype=jnp.int32)[:, None, None]
    dest = rank + jnp.sum(jnp.where(experts[None] == e_ids, pad_start[:, None, None], 0), axis=0)
    nb = -(-p_n // MOE_TM) + N_EXPERTS
    src_tok = (jnp.arange(nb * MOE_TM, dtype=jnp.int32) % t).at[dest.reshape(p_n)].set(
        jnp.tile(jnp.arange(t, dtype=jnp.int32), TOP_K), mode="promise_in_bounds",
        unique_indices=True)
    block_start = jnp.arange(nb, dtype=jnp.int32) * MOE_TM
    block_expert = jnp.minimum(
        jnp.sum((block_start[:, None] >= pad_end[None, :]).astype(jnp.int32), axis=1),
        N_EXPERTS - 1)
    n_used = (pad_end[-1] // MOE_TM).astype(jnp.int32).reshape(1)
    return dest, src_tok, block_expert, n_used


def kernel(x, norm_mix, w_in, conv_w, conv_b, b_gates_if, norm_head, w_proj_a, w_proj_b,
           w_merge_gate, b_merge_gate, w_out, norm_ffn, w_router, b_router, w_gate_up,
           b_gate_up, w_down, b_down, norm_final):
    bn, s, d = x.shape
    assert bn == 1
    depth = norm_mix.shape[0]
    xt = x.reshape(s, d)
    tm = min(1024, s)
    aqk = A_HEADS * A_QK
    av = A_HEADS * A_V
    bw = B_HEADS * B_HEAD_DIM

    for l in range(depth):
        wl = w_in[l]
        c0 = 2 * aqk
        c1 = c0 + 2 * av
        c2 = c1 + 2 * A_HEADS
        w_f32 = jnp.concatenate(
            [wl[:, :c0], wl[:, c1:c2], jnp.zeros((d, LANE - 2 * A_HEADS), F32)], axis=1).astype(BF16)
        b_f32 = jnp.concatenate(
            [jnp.zeros((c0,), F32), b_gates_if[l], jnp.zeros((LANE - 2 * A_HEADS,), F32)])[None, :]
        w_bf = jnp.concatenate([wl[:, c0:c1], wl[:, c2:c2 + bw] * (B_HEAD_DIM ** -0.5), wl[:, c2 + bw:]],
                               axis=1).astype(BF16)
        g_mix = norm_mix[l][None, :]

        p_f32, h1, if_row = _norm_proj(xt, g_mix, w_f32, b_f32, tm=tm)
        tn = 1024
        o_tile = av // tn
        p_bf = _proj(h1, w_bf, jnp.zeros((1, w_bf.shape[1]), F32),
                     sigmoid_tiles=(o_tile, 2 * o_tile), tm=tm, tn=tn)
        gate_tn = 2 * tn
        gates = _proj(h1, w_merge_gate[l].astype(BF16), b_merge_gate[l][None, :],
                      sigmoid_tiles=(0, w_merge_gate.shape[2] // gate_tn), tm=tm, tn=gate_tn)

        y_a = _mlstm(p_f32, p_bf, if_row, conv_w[l], conv_b[l][None, :], norm_head[l][None, :])
        qb = 2 * av // B_HEAD_DIM
        y_b = _stick_breaking(p_bf, qb, qb + B_HEADS, qb + 2 * B_HEADS)

        wr_hi, wr_lo = _split_bf16(w_router[l].T)
        b_r = jnp.broadcast_to(b_router[l][:, None], (N_EXPERTS, LANE))
        x1, h2, ri, rg, counts = _out_proj(
            xt, gates, y_a, w_proj_a[l].astype(BF16), y_b, w_proj_b[l].astype(BF16),
            w_out[l].astype(BF16), norm_ffn[l][None, :], wr_hi, wr_lo, b_r, tm=min(512, s))

        dest, src_tok, block_expert, n_used = _route(ri, counts, s)
        nb = block_expert.shape[0]
        bounds = [0] + [nb * f // 10 for f in MOE_CHUNK_TENTHS] + [nb]
        acts = []
        for lo, hi in zip(bounds[:-1], bounds[1:]):
            x_pad = h2.at[src_tok[lo * MOE_TM:hi * MOE_TM]].get(mode="promise_in_bounds")
            acts.append(_moe_up(block_expert[lo:hi], jnp.clip(n_used - lo, 0, hi - lo),
                                x_pad, w_gate_up[l], b_gate_up[l][:, None, :], tf=1024))
        y_pad = _moe_down(block_expert, n_used, acts, w_down[l], b_down[l][:, None, :])
        nct = s // COMBINE_CHUNKS
        xt = x1
        for j in range(COMBINE_CHUNKS):
            idx = dest[:, j * nct:(j + 1) * nct].reshape(TOP_K * nct)
            y_rows = y_pad.at[idx].get(mode="promise_in_bounds")
            xt = _combine(xt, y_rows, rg, norm_final[None, :], chunk=j, n_chunk_tokens=nct,
                          normalize=(l + 1 == depth), tm=min(512, nct))
    return xt.reshape(bn, s, d)
```
